```python
import math, functools
import jax, jax.numpy as jnp
from jax import lax
import numpy as np

D_MODEL = 1024
BATCH = 8
SEQ = 2048
DEPTH = 2

GRID_W = 64
CTX_LEN = 256
HEAD_DIM = 64
ROPE_THETA = 10000.0
ROPE_FREQS = HEAD_DIM // 4
Q_BLOCK = 128
A_HEADS = 8
A_KV_HEADS = 2
B_HEADS = 4
B_V_DIM = 2 * HEAD_DIM
C_HEADS = 8
C_KV_HEADS = 2
C_WINDOW = 128
D_HEADS = 8
NA_ROWS = 8
NA_COLS = 16
EVEN_SPLITS = (A_HEADS * HEAD_DIM, A_KV_HEADS * HEAD_DIM, A_KV_HEADS * HEAD_DIM,
               2 * B_HEADS * HEAD_DIM, 2 * B_HEADS * HEAD_DIM, B_HEADS * B_V_DIM)
ODD_SPLITS = (C_HEADS * HEAD_DIM, C_KV_HEADS * HEAD_DIM, C_KV_HEADS * HEAD_DIM,
              D_HEADS * HEAD_DIM, D_HEADS * HEAD_DIM, D_HEADS * HEAD_DIM)
IN_WIDTH = sum(EVEN_SPLITS)
MIX_WIDTH = A_HEADS * HEAD_DIM + B_HEADS * B_V_DIM
D_FF = 4 * D_MODEL
N_EVEN = (DEPTH + 1) // 2
N_ODD = DEPTH // 2
EPS = 1e-6
NEG = -1e30

kernel_name = "hybrid_dit_prefix_block"


def split_cols(p, sizes):
    return jnp.split(p, np.cumsum(sizes)[:-1].tolist(), axis=-1)


def rms_norm(x, g):
    xf = x.astype(jnp.float32)
    y = xf * lax.rsqrt(jnp.mean(jnp.square(xf), axis=-1, keepdims=True) + EPS)
    return (y * g.astype(jnp.float32)).astype(x.dtype)


def modulate(h, shift, scale):
    return h * (1 + scale) + shift


def lambda_init(layer):
    return 0.8 - 0.6 * math.exp(-0.3 * layer)


def axial_rope_tables(T, dtype):
    t = jnp.arange(T, dtype=jnp.int32)
    pos = jnp.stack([t // GRID_W, t % GRID_W], axis=-1).astype(jnp.float32)
    inv = ROPE_THETA ** (-jnp.arange(ROPE_FREQS, dtype=jnp.float32) / ROPE_FREQS)
    ang = pos[..., None] * inv
    return jnp.cos(ang).astype(dtype), jnp.sin(ang).astype(dtype)


def apply_axial_rope(x, cos, sin):
    xs = x.reshape(x.shape[:-1] + (2, 2, ROPE_FREQS))
    x1, x2 = xs[..., 0, :], xs[..., 1, :]
    c, s = cos[:, None], sin[:, None]
    return jnp.stack([x1 * c - x2 * s, x2 * c + x1 * s], axis=-2).reshape(x.shape)


def map_query_blocks(fn, *qs):
    B, T = qs[0].shape[:2]
    nb = T // Q_BLOCK
    blocks = tuple(jnp.moveaxis(a.reshape((B, nb, Q_BLOCK) + a.shape[2:]), 1, 0) for a in qs)
    out = lax.map(lambda args: fn(*args), blocks)
    return jnp.moveaxis(out, 0, 1).reshape((B, T) + out.shape[3:])


def gqa_attend(q, k, v):
    s = jnp.einsum('bqkgd,bskd->bkgqs', q, k).astype(jnp.float32)
    p = jax.nn.softmax(s, axis=-1).astype(v.dtype)
    return jnp.einsum('bkgqs,bskd->bqkgd', p, v)


def diff_attend(q, k, v, lam):
    s = jnp.einsum('bqhjd,bshjd->bhjqs', q, k).astype(jnp.float32)
    p = jax.nn.softmax(s, axis=-1)
    a = (p[:, :, 0] - lam * p[:, :, 1]).astype(v.dtype)
    return jnp.einsum('bhqs,bshe->bqhe', a, v)


def sink_attend(q, k, v, sink):
    s = jnp.einsum('bqkgd,bskd->bkgqs', q, k).astype(jnp.float32)
    sk = jnp.broadcast_to(sink.astype(jnp.float32)[None, :, :, None, None], s.shape[:-1] + (1,))
    p = jax.nn.softmax(jnp.concatenate([sk, s], axis=-1), axis=-1)[..., 1:].astype(v.dtype)
    return jnp.einsum('bkgqs,bskd->bqkgd', p, v)


def windowed_attend(q, k, v, k_ctx, v_ctx, sink):
    B, T = q.shape[:2]
    L = k_ctx.shape[1]
    nb = T // C_WINDOW

    def band(a):
        ap = jnp.pad(a, ((0, 0), (C_WINDOW, C_WINDOW), (0, 0), (0, 0)))
        ap = ap.reshape((B, nb + 2, C_WINDOW) + a.shape[2:])
        return jnp.concatenate([ap[:, :-2], ap[:, 1:-1], ap[:, 2:]], axis=2)

    kb, vb = band(k), band(v)
    qb = q.reshape((B, nb, C_WINDOW) + q.shape[2:])
    i = jnp.arange(C_WINDOW)[:, None]
    j = jnp.arange(3 * C_WINDOW)[None, :]
    kpos = jnp.arange(nb)[:, None, None] * C_WINDOW - C_WINDOW + j
    valid = (jnp.abs(j - C_WINDOW - i) <= C_WINDOW)[None] & (kpos >= 0) & (kpos < T)
    s_loc = jnp.einsum('bnqkgd,bnskd->bnkgqs', qb, kb).astype(jnp.float32)
    s_loc = jnp.where(valid[None, :, None, None], s_loc, NEG)
    s_ctx = jnp.einsum('bnqkgd,bskd->bnkgqs', qb, k_ctx).astype(jnp.float32)
    sk = jnp.broadcast_to(sink.astype(jnp.float32)[None, None, :, :, None, None], s_ctx.shape[:-1] + (1,))
    p = jax.nn.softmax(jnp.concatenate([sk, s_ctx, s_loc], axis=-1), axis=-1)
    p_ctx = p[..., 1:1 + L].astype(v.dtype)
    p_loc = p[..., 1 + L:].astype(v.dtype)
    o = (jnp.einsum('bnkgqs,bskd->bnqkgd', p_ctx, v_ctx)
         + jnp.einsum('bnkgqs,bnskd->bnqkgd', p_loc, vb))
    return o.reshape(q.shape)


def neighborhood_attend(q, k, v, k_ctx, v_ctx, rpb):
    B, T, H, Dh = q.shape
    rows = T // GRID_W
    kh = min(NA_ROWS, rows)
    kw = NA_COLS
    L = k_ctx.shape[1]
    r = jnp.arange(rows)
    rs = jnp.clip(r - kh // 2, 0, rows - kh)
    row_idx = rs[:, None] + jnp.arange(kh)[None, :]

    def gather_rows(a):
        return a.reshape(B, rows, GRID_W, H, Dh)[:, row_idx].reshape(B, rows, kh * GRID_W, H, Dh)

    kg, vg = gather_rows(k), gather_rows(v)
    qg = q.reshape(B, rows, GRID_W, H, Dh)
    cq = jnp.arange(GRID_W)
    cs = jnp.clip(cq - kw // 2, 0, GRID_W - kw)
    ck = jnp.tile(jnp.arange(GRID_W), kh)
    col_valid = (ck[None] >= cs[:, None]) & (ck[None] < cs[:, None] + kw)
    dr = jnp.repeat(row_idx - r[:, None], GRID_W, axis=1)
    dc = jnp.clip(ck[None] - cq[:, None], -(kw - 1), kw - 1)
    bias = rpb[:, dr[:, None, :] + NA_ROWS - 1, dc[None] + NA_COLS - 1]
    bias = jnp.moveaxis(bias, 0, 1).astype(jnp.float32)
    s_loc = jnp.einsum('brqhd,brshd->brhqs', qg, kg).astype(jnp.float32) + bias
    s_loc = jnp.where(col_valid, s_loc, NEG)
    s_ctx = jnp.einsum('brqhd,bshd->brhqs', qg, k_ctx).astype(jnp.float32)
    p = jax.nn.softmax(jnp.concatenate([s_ctx, s_loc], axis=-1), axis=-1)
    p_ctx = p[..., :L].astype(v.dtype)
    p_loc = p[..., L:].astype(v.dtype)
    o = (jnp.einsum('brhqs,bshd->brqhd', p_ctx, v_ctx)
         + jnp.einsum('brhqs,brshd->brqhd', p_loc, vg))
    return o.reshape(B, T, H * Dh)


def even_mixer(h_x, h_c, w_in, w_out, qk_g, lam_p, subln_g, lam0, cos, sin, with_ctx):
    scale = HEAD_DIM ** -0.5
    g_a = A_HEADS // A_KV_HEADS

    def project(h, rope):
        n, s = h.shape[:2]
        qa, ka, va, qb, kb, vb = split_cols(h @ w_in, EVEN_SPLITS)
        qa = rms_norm(qa.reshape(n, s, A_HEADS, HEAD_DIM), qk_g[0])
        ka = rms_norm(ka.reshape(n, s, A_KV_HEADS, HEAD_DIM), qk_g[1])
        qb = qb.reshape(n, s, 2 * B_HEADS, HEAD_DIM)
        kb = kb.reshape(n, s, 2 * B_HEADS, HEAD_DIM)
        if rope:
            qa, ka = apply_axial_rope(qa, cos, sin), apply_axial_rope(ka, cos, sin)
            qb, kb = apply_axial_rope(qb, cos, sin), apply_axial_rope(kb, cos, sin)
        return ((qa * scale).reshape(n, s, A_KV_HEADS, g_a, HEAD_DIM), ka,
                va.reshape(n, s, A_KV_HEADS, HEAD_DIM),
                (qb * scale).reshape(n, s, B_HEADS, 2, HEAD_DIM),
                kb.reshape(n, s, B_HEADS, 2, HEAD_DIM),
                vb.reshape(n, s, B_HEADS, B_V_DIM))

    lam = (jnp.exp(jnp.sum(lam_p[0] * lam_p[1]).astype(jnp.float32))
           - jnp.exp(jnp.sum(lam_p[2] * lam_p[3]).astype(jnp.float32)) + lam0)
    qa_x, ka_x, va_x, qb_x, kb_x, vb_x = project(h_x, True)
    qa_c, ka_c, va_c, qb_c, kb_c, vb_c = project(h_c, False)
    ka_all = jnp.concatenate([ka_c, ka_x], axis=1)
    va_all = jnp.concatenate([va_c, va_x], axis=1)
    kb_all = jnp.concatenate([kb_c, kb_x], axis=1)
    vb_all = jnp.concatenate([vb_c, vb_x], axis=1)
    B, T = h_x.shape[:2]
    o_a = map_query_blocks(lambda q: gqa_attend(q, ka_all, va_all), qa_x).reshape(B, T, -1)
    o_b = map_query_blocks(lambda q: diff_attend(q, kb_all, vb_all, lam), qb_x)
    o_b = (rms_norm(o_b, subln_g) * (1 - lam0)).reshape(B, T, -1)
    out_x = jnp.concatenate([o_a, o_b], axis=-1) @ w_out
    out_c = None
    if with_ctx:
        n, L = h_c.shape[:2]
        oa_c = gqa_attend(qa_c, ka_c, va_c).reshape(n, L, -1)
        ob_c = (rms_norm(diff_attend(qb_c, kb_c, vb_c, lam), subln_g) * (1 - lam0)).reshape(n, L, -1)
        out_c = jnp.concatenate([oa_c, ob_c], axis=-1) @ w_out
    return out_x, out_c


def odd_mixer(h_x, h_c, w_in, w_out, sink, rpb, cos, sin, with_ctx):
    scale = HEAD_DIM ** -0.5
    g_c = C_HEADS // C_KV_HEADS
    sink = sink.reshape(C_KV_HEADS, g_c)

    def project(h, rope):
        n, s = h.shape[:2]
        qc, kc, vc, qd, kd, vd = split_cols(h @ w_in, ODD_SPLITS)
        qc = qc.reshape(n, s, C_HEADS, HEAD_DIM)
        kc = kc.reshape(n, s, C_KV_HEADS, HEAD_DIM)
        if rope:
            qc, kc = apply_axial_rope(qc, cos, sin), apply_axial_rope(kc, cos, sin)
        return ((qc * scale).reshape(n, s, C_KV_HEADS, g_c, HEAD_DIM), kc,
                vc.reshape(n, s, C_KV_HEADS, HEAD_DIM),
                qd.reshape(n, s, D_HEADS, HEAD_DIM) * scale,
                kd.reshape(n, s, D_HEADS, HEAD_DIM),
                vd.reshape(n, s, D_HEADS, HEAD_DIM))

    qc_x, kc_x, vc_x, qd_x, kd_x, vd_x = project(h_x, True)
    qc_c, kc_c, vc_c, qd_c, kd_c, vd_c = project(h_c, False)
    B, T = h_x.shape[:2]
    o_c = windowed_attend(qc_x, kc_x, vc_x, kc_c, vc_c, sink).reshape(B, T, -1)
    o_d = neighborhood_attend(qd_x, kd_x, vd_x, kd_c, vd_c, rpb)
    out_x = jnp.concatenate([o_c, o_d], axis=-1) @ w_out
    out_c = None
    if with_ctx:
        n, L = h_c.shape[:2]
        oc_c = sink_attend(qc_c, kc_c, vc_c, sink).reshape(n, L, -1)
        od_c = gqa_attend(qd_c[:, :, :, None], kd_c, vd_c).reshape(n, L, -1)
        out_c = jnp.concatenate([oc_c, od_c], axis=-1) @ w_out
    return out_x, out_c


def squared_relu_mlp(h, w1, w2):
    return jnp.square(jax.nn.relu(h @ w1)) @ w2


def setup_inputs(seed: int = 0) -> dict:
    key = jax.random.key(seed)
    ks = jax.random.split(key, 16)
    f32 = jnp.float32

    def nrm(k, shape, s):
        return jax.random.normal(k, shape, f32) * s

    return {
        "x": nrm(ks[0], (BATCH, SEQ, D_MODEL), 1.0),
        "c": nrm(ks[1], (BATCH, D_MODEL), 1.0),
        "ctx": nrm(ks[2], (BATCH, CTX_LEN, D_MODEL), 1.0),
        "c_ctx": nrm(ks[3], (D_MODEL,), 1.0),
        "w_mod": nrm(ks[4], (DEPTH, D_MODEL, 6 * D_MODEL), D_MODEL ** -0.5),
        "b_mod": nrm(ks[5], (DEPTH, 6 * D_MODEL), 0.02),
        "norm_g": 1.0 + nrm(ks[6], (DEPTH, 4, D_MODEL), 0.02),
        "w_in": nrm(ks[7], (DEPTH, D_MODEL, IN_WIDTH), D_MODEL ** -0.5),
        "w_out": nrm(ks[8], (DEPTH, MIX_WIDTH, D_MODEL), MIX_WIDTH ** -0.5),
        "w_mlp_in": nrm(ks[9], (DEPTH, D_MODEL, D_FF), D_MODEL ** -0.5),
        "w_mlp_out": nrm(ks[10], (DEPTH, D_FF, D_MODEL), D_FF ** -0.5),
        "qk_norm_a": 1.0 + nrm(ks[11], (N_EVEN, 2, HEAD_DIM), 0.02),
        "diff_lambda": nrm(ks[12], (N_EVEN, 4, HEAD_DIM), 0.1),
        "diff_subln": 1.0 + nrm(ks[13], (N_EVEN, B_V_DIM), 0.02),
        "sink_c": nrm(ks[14], (N_ODD, C_HEADS), 0.5),
        "rpb_d": nrm(ks[15], (N_ODD, D_HEADS, 2 * NA_ROWS - 1, 2 * NA_COLS - 1), 0.1),
    }


def reference(x, c, ctx, c_ctx, w_mod, b_mod, norm_g, w_in, w_out, w_mlp_in, w_mlp_out,
              qk_norm_a, diff_lambda, diff_subln, sink_c, rpb_d):
    T = x.shape[1]
    cos, sin = axial_rope_tables(T, x.dtype)
    s_x = jax.nn.silu(c)
    s_c = jax.nn.silu(c_ctx)
    for l in range(DEPTH):
        last = l == DEPTH - 1
        mod_x = (s_x @ w_mod[l] + b_mod[l])[:, None, :]
        mod_c = s_c @ w_mod[l] + b_mod[l]
        sh1_x, sc1_x, g1_x, sh2_x, sc2_x, g2_x = jnp.split(mod_x, 6, axis=-1)
        sh1_c, sc1_c, g1_c, sh2_c, sc2_c, g2_c = jnp.split(mod_c, 6, axis=-1)
        h_x = modulate(rms_norm(x, norm_g[l, 0]), sh1_x, sc1_x)
        h_c = modulate(rms_norm(ctx, norm_g[l, 0]), sh1_c, sc1_c)
        if l % 2 == 0:
            i = l // 2
            m_x, m_c = even_mixer(h_x, h_c, w_in[l], w_out[l], qk_norm_a[i], diff_lambda[i],
                                  diff_subln[i], lambda_init(l), cos, sin, not last)
        else:
            i = l // 2
            m_x, m_c = odd_mixer(h_x, h_c, w_in[l], w_out[l], sink_c[i], rpb_d[i],
                                 cos, sin, not last)
        x = x + g1_x * rms_norm(m_x, norm_g[l, 1])
        f_x = squared_relu_mlp(modulate(rms_norm(x, norm_g[l, 2]), sh2_x, sc2_x), w_mlp_in[l], w_mlp_out[l])
        x = x + g2_x * rms_norm(f_x, norm_g[l, 3])
        if not last:
            ctx = ctx + g1_c * rms_norm(m_c, norm_g[l, 1])
            f_c = squared_relu_mlp(modulate(rms_norm(ctx, norm_g[l, 2]), sh2_c, sc2_c), w_mlp_in[l], w_mlp_out[l])
            ctx = ctx + g2_c * rms_norm(f_c, norm_g[l, 3])
    return x
```

```python
import functools
import math

import numpy as np
import jax
import jax.numpy as jnp
from jax import lax
from jax.experimental import pallas as pl
from jax.experimental.pallas import tpu as pltpu

F32 = jnp.float32
BF16 = jnp.bfloat16

GRID_W = 64
HEAD_DIM = 64
ROPE_THETA = 10000.0
ROPE_FREQS = HEAD_DIM // 4
N_SLOTS = 8
PAIR = 2 * HEAD_DIM
Q_WIDTH = N_SLOTS * HEAD_DIM
KV_HEADS = 2
C_WINDOW = 128
NA_ROWS = 8
NA_COLS = 16
EPS = 1e-6
NEG = -1e30
Q_SCALE = HEAD_DIM ** -0.5
GQA_HEAD_ORDER = (0, 4, 1, 5, 2, 6, 3, 7)

VMEM_LIMIT_BYTES = 56 * 1024 * 1024


def _compiler_params(n_grid_dims):
    return pltpu.CompilerParams(dimension_semantics=("arbitrary",) * n_grid_dims,
                                vmem_limit_bytes=VMEM_LIMIT_BYTES)


def _dot(a, b):
    return jnp.dot(a, b, preferred_element_type=F32)


def _dot_nt(a, b):
    return lax.dot_general(a, b, (((1,), (1,)), ((), ())), preferred_element_type=F32)


def _rms(x, gain):
    ms = jnp.mean(x * x, axis=-1, keepdims=True)
    return x * lax.rsqrt(ms + EPS) * gain


def _lane_is_low(shape):
    return lax.broadcasted_iota(jnp.int32, shape, len(shape) - 1) < HEAD_DIM


def _mod_kernel(c_ref, w_ref, b_ref, o_ref):
    c = c_ref[...]
    s = c * jax.nn.sigmoid(c)
    o_ref[0] = _dot(s.astype(BF16), w_ref[0].astype(BF16)) + b_ref[0]


def _modulation(cc, w_mod, b_mod):
    depth, d, n = w_mod.shape
    rows = cc.shape[0]
    tn = n // 4
    return pl.pallas_call(
        _mod_kernel,
        grid=(depth, n // tn),
        in_specs=[pl.BlockSpec((rows, d), lambda l, j: (0, 0)),
                  pl.BlockSpec((1, d, tn), lambda l, j: (l, 0, j)),
                  pl.BlockSpec((1, 1, tn), lambda l, j: (l, 0, j))],
        out_specs=pl.BlockSpec((1, rows, tn), lambda l, j: (l, 0, j)),
        out_shape=jax.ShapeDtypeStruct((depth, rows, n), F32),
        compiler_params=_compiler_params(2),
        name="modulation",
    )(cc, w_mod, b_mod.reshape(depth, 1, n))


def _in_kernel(*refs, even, rope):
    if rope:
        tok_ref, mod_ref, g_ref, w_ref, qkg_ref, seg_ref, rope_ref = refs[:7]
        outs = refs[7:]
    else:
        tok_ref, mod_ref, g_ref, w_ref, qkg_ref, seg_ref = refs[:6]
        rope_ref = None
        outs = refs[6:]
    q1_ref, kv1_ref, q2_ref, k2_ref, v2_ref = outs
    d = tok_ref.shape[-1]

    h = _rms(tok_ref[0], g_ref[...])
    h = h * (1.0 + mod_ref[0, :, d:2 * d]) + mod_ref[0, :, 0:d]
    hb = h.astype(BF16)
    low = _lane_is_low((h.shape[0], PAIR))

    def head_norm(c, gain):
        sq = c * c
        hi = sq.astype(BF16)
        lo = (sq - hi.astype(F32)).astype(BF16)
        ms = (_dot(hi, seg_ref[...]) + _dot(lo, seg_ref[...])) * (1.0 / HEAD_DIM)
        return c * lax.rsqrt(ms + EPS) * gain

    def rotary(c):
        up = pltpu.roll(c, PAIR - ROPE_FREQS, axis=1)
        down = pltpu.roll(c, ROPE_FREQS, axis=1)
        return c * rope_ref[0] + up * rope_ref[1] + down * rope_ref[2]

    def chunk(p, j):
        return p[:, j * PAIR:(j + 1) * PAIR]

    def store_slots(q_ref, p, gain, use_rope):
        for j in range(N_SLOTS // 2):
            c = chunk(p, j)
            if gain is not None:
                c = head_norm(c, gain)
            if use_rope:
                c = rotary(c)
            c = c * Q_SCALE
            q_ref[0, 2 * j] = jnp.where(low, c, 0.0).astype(BF16)
            q_ref[0, 2 * j + 1] = jnp.where(low, 0.0, c).astype(BF16)

    p = _dot(hb, w_ref[:, 0:Q_WIDTH])
    store_slots(q1_ref, p, qkg_ref[0:1] if even else None, rope)
    p = _dot(hb, w_ref[:, Q_WIDTH:Q_WIDTH + 2 * PAIR])
    k = chunk(p, 0)
    if even:
        k = head_norm(k, qkg_ref[1:2])
    if rope:
        k = rotary(k)
    kv1_ref[0, :, 0:PAIR] = k.astype(BF16)
    kv1_ref[0, :, PAIR:2 * PAIR] = chunk(p, 1).astype(BF16)
    o = Q_WIDTH + 2 * PAIR
    p = _dot(hb, w_ref[:, o:o + Q_WIDTH])
    store_slots(q2_ref, p, None, rope and even)
    o += Q_WIDTH
    p = _dot(hb, w_ref[:, o:o + Q_WIDTH])
    for j in range(N_SLOTS // 2):
        c = chunk(p, j)
        if rope and even:
            c = rotary(c)
        k2_ref[0, :, j * PAIR:(j + 1) * PAIR] = c.astype(BF16)
    o += Q_WIDTH
    v2_ref[0] = _dot(hb, w_ref[:, o:o + Q_WIDTH]).astype(BF16)


def _in_proj(tok, mod, gain, w, qk_gain, seg, rope_tab, *, even, tm):
    b, s, d = tok.shape
    n = w.shape[1]
    bm = mod.shape[0]
    rope = rope_tab is not None
    mod_map = (lambda i, j: (i, 0, 0)) if bm > 1 else (lambda i, j: (0, 0, 0))
    in_specs = [pl.BlockSpec((1, tm, d), lambda i, j: (i, j, 0)),
                pl.BlockSpec((1, 1, mod.shape[-1]), mod_map),
                pl.BlockSpec((1, d), lambda i, j: (0, 0)),
                pl.BlockSpec((d, n), lambda i, j: (0, 0)),
                pl.BlockSpec(qk_gain.shape, lambda i, j: (0, 0)),
                pl.BlockSpec(seg.shape, lambda i, j: (0, 0))]
    args = [tok, mod, gain, w, qk_gain, seg]
    if rope:
        in_specs.append(pl.BlockSpec((3, tm, PAIR), lambda i, j: (0, j, 0)))
        args.append(rope_tab)
    slot_spec = pl.BlockSpec((1, N_SLOTS, tm, PAIR), lambda i, j: (i, 0, j, 0))
    out_specs = [slot_spec,
                 pl.BlockSpec((1, tm, 2 * PAIR), lambda i, j: (i, j, 0)),
                 slot_spec,
                 pl.BlockSpec((1, tm, Q_WIDTH), lambda i, j: (i, j, 0)),
                 pl.BlockSpec((1, tm, Q_WIDTH), lambda i, j: (i, j, 0))]
    out_shape = [jax.ShapeDtypeStruct((b, N_SLOTS, s, PAIR), BF16),
                 jax.ShapeDtypeStruct((b, s, 2 * PAIR), BF16),
                 jax.ShapeDtypeStruct((b, N_SLOTS, s, PAIR), BF16),
                 jax.ShapeDtypeStruct((b, s, Q_WIDTH), BF16),
                 jax.ShapeDtypeStruct((b, s, Q_WIDTH), BF16)]
    return pl.pallas_call(
        functools.partial(_in_kernel, even=even, rope=rope),
        grid=(b, s // tm),
        in_specs=in_specs, out_specs=out_specs, out_shape=out_shape,
        compiler_params=_compiler_params(2),
        name="in_proj_" + ("even" if even else "odd") + ("_x" if rope else "_ctx"),
    )(*args)


def _softmax_pv(q, k_list, v_list):
    s = [_dot_nt(q, k) for k in k_list]
    m = functools.reduce(jnp.maximum, [jnp.max(x, axis=-1, keepdims=True) for x in s])
    e = [jnp.exp(x - m) for x in s]
    l = functools.reduce(jnp.add, [jnp.sum(x, axis=-1, keepdims=True) for x in e])
    o = functools.reduce(jnp.add, [_dot(x.astype(BF16), v) for x, v in zip(e, v_list)])
    return o * (1.0 / l)


def _attn_even_kernel(*refs, with_x, lam0, tq):
    if with_x:
        q1_ref, q2_ref, kvc_ref, k2c_ref, v2c_ref, kvx_ref, k2x_ref, v2x_ref, lam_ref, sub_ref, o_ref = refs
    else:
        q1_ref, q2_ref, kvc_ref, k2c_ref, v2c_ref, lam_ref, sub_ref, o_ref = refs
        kvx_ref = k2x_ref = v2x_ref = None
    low = _lane_is_low((tq, PAIR))

    def chunks(c_ref, x_ref, lo, hi):
        out = [c_ref[0, :, lo:hi]]
        if with_x:
            out.append(x_ref[0, :, lo:hi])
        return out

    k_list = chunks(kvc_ref, kvx_ref, 0, PAIR)
    v_list = chunks(kvc_ref, kvx_ref, PAIR, 2 * PAIR)
    group = 4
    for g in range(N_SLOTS // group):
        q = q1_ref[0, g * group:(g + 1) * group].reshape(group * tq, PAIR)
        o = _softmax_pv(q, k_list, v_list)
        for pp in range(group // 2):
            lo = o[(2 * pp) * tq:(2 * pp + 1) * tq]
            hi = o[(2 * pp + 1) * tq:(2 * pp + 2) * tq]
            col = (g * (group // 2) + pp) * PAIR
            o_ref[0, :, col:col + PAIR] = jnp.where(low, lo, hi).astype(BF16)

    lp = lam_ref[...]
    lam = (jnp.exp(jnp.sum(lp[0:1] * lp[1:2], axis=-1, keepdims=True))
           - jnp.exp(jnp.sum(lp[2:3] * lp[3:4], axis=-1, keepdims=True)) + lam0)
    for h in range(N_SLOTS // 2):
        q = q2_ref[0, 2 * h:2 * h + 2].reshape(2 * tq, PAIR)
        o = _softmax_pv(q, chunks(k2c_ref, k2x_ref, h * PAIR, (h + 1) * PAIR),
                        chunks(v2c_ref, v2x_ref, h * PAIR, (h + 1) * PAIR))
        ob = o[0:tq] - lam * o[tq:2 * tq]
        ob = _rms(ob, sub_ref[...]) * (1.0 - lam0)
        col = Q_WIDTH + h * PAIR
        o_ref[0, :, col:col + PAIR] = ob.astype(BF16)


def _attn_even(q1, q2, kvc, k2c, v2c, kvx, k2x, v2x, lam_p, subln, *, lam0, tq):
    b, _, s, _ = q1.shape
    with_x = kvx is not None
    slot_spec = pl.BlockSpec((1, N_SLOTS, tq, PAIR), lambda i, j: (i, 0, j, 0))

    def full(a):
        return pl.BlockSpec((1,) + a.shape[1:], lambda i, j: (i, 0, 0))

    args = [q1, q2, kvc, k2c, v2c]
    in_specs = [slot_spec, slot_spec, full(kvc), full(k2c), full(v2c)]
    if with_x:
        args += [kvx, k2x, v2x]
        in_specs += [full(kvx), full(k2x), full(v2x)]
    args += [lam_p, subln]
    in_specs += [pl.BlockSpec(lam_p.shape, lambda i, j: (0, 0)), pl.BlockSpec(subln.shape, lambda i, j: (0, 0))]
    return pl.pallas_call(
        functools.partial(_attn_even_kernel, with_x=with_x, lam0=lam0, tq=tq),
        grid=(b, s // tq),
        in_specs=in_specs,
        out_specs=pl.BlockSpec((1, tq, 2 * Q_WIDTH), lambda i, j: (i, j, 0)),
        out_shape=jax.ShapeDtypeStruct((b, s, 2 * Q_WIDTH), BF16),
        compiler_params=_compiler_params(2),
        name="attn_even" + ("_x" if with_x else "_ctx"),
    )(*args)


def _bias_kernel(rpb_ref, o_ref):
    pair = pl.program_id(0)
    n_dr = 2 * NA_ROWS - 1
    n_dc = 2 * NA_COLS - 1
    shape = (PAIR, PAIR)
    row = lax.broadcasted_iota(jnp.int32, shape, 0)
    lane = lax.broadcasted_iota(jnp.int32, shape, 1)
    row_hi = row >= HEAD_DIM
    lane_hi = lane >= HEAD_DIM
    dcl = jnp.clip((lane & (HEAD_DIM - 1)) - (row & (HEAD_DIM - 1)), -(NA_COLS - 1), NA_COLS - 1) + (NA_COLS - 1)
    for d in range(n_dr - 1):
        acc = jnp.zeros(shape, F32)
        for j in range(n_dc):
            def entry(hh, hl):
                return rpb_ref[((2 * pair + hh) * n_dr + d + hl) * n_dc + j]
            val = jnp.where(row_hi, jnp.where(lane_hi, entry(1, 1), entry(1, 0)),
                            jnp.where(lane_hi, entry(0, 1), entry(0, 0)))
            acc = jnp.where(dcl == j, val, acc)
        o_ref[0, d] = acc


def _bias_table(rpb):
    heads, n_dr, n_dc = rpb.shape
    return pl.pallas_call(
        _bias_kernel,
        grid=(heads // 2,),
        in_specs=[pl.BlockSpec(memory_space=pltpu.SMEM)],
        out_specs=pl.BlockSpec((1, n_dr - 1, PAIR, PAIR), lambda p: (p, 0, 0, 0)),
        out_shape=jax.ShapeDtypeStruct((heads // 2, n_dr - 1, PAIR, PAIR), F32),
        compiler_params=_compiler_params(1),
        name="rel_bias_table",
    )(rpb.reshape(-1))


def _attn_odd_kernel(q1_ref, q2_ref, kvx_ref, kvc_ref, k2x_ref, v2x_ref, k2c_ref, v2c_ref, sink_ref, tp_ref,
                     o_ref, *, tq):
    i = pl.program_id(1)
    t = kvx_ref.shape[1]
    rows = t // GRID_W
    kh = min(NA_ROWS, rows)

    n_blk = tq // C_WINDOW
    span = 3 * C_WINDOW
    m_rows = N_SLOTS * C_WINDOW
    low_c = _lane_is_low((C_WINDOW, PAIR))
    slot = lax.broadcasted_iota(jnp.int32, (m_rows, 1), 0) // C_WINDOW
    sink = jnp.zeros((m_rows, 1), F32)
    for j in range(N_SLOTS):
        sink = jnp.where(slot == j, sink_ref[GQA_HEAD_ORDER[j]], sink)
    k_ctx = kvc_ref[0, :, 0:PAIR]
    v_ctx = kvc_ref[0, :, PAIR:2 * PAIR]
    for jb in range(n_blk):
        q0 = (i * n_blk + jb) * C_WINDOW
        start = pl.multiple_of(jnp.clip(q0 - C_WINDOW, 0, t - span), C_WINDOW)
        q = q1_ref[0, :, jb * C_WINDOW:(jb + 1) * C_WINDOW, :].reshape(m_rows, PAIR)
        k_loc = kvx_ref[0, pl.ds(start, span), 0:PAIR]
        v_loc = kvx_ref[0, pl.ds(start, span), PAIR:2 * PAIR]
        qpos = q0 + (lax.broadcasted_iota(jnp.int32, (m_rows, span), 0) & (C_WINDOW - 1))
        kpos = start + lax.broadcasted_iota(jnp.int32, (m_rows, span), 1)
        dist = kpos - qpos
        s_loc = jnp.where((dist >= -C_WINDOW) & (dist <= C_WINDOW), _dot_nt(q, k_loc), NEG)
        s_ctx = _dot_nt(q, k_ctx)
        m = jnp.maximum(jnp.maximum(jnp.max(s_loc, axis=-1, keepdims=True),
                                    jnp.max(s_ctx, axis=-1, keepdims=True)), sink)
        e_loc = jnp.exp(s_loc - m)
        e_ctx = jnp.exp(s_ctx - m)
        l = (jnp.sum(e_loc, axis=-1, keepdims=True) + jnp.sum(e_ctx, axis=-1, keepdims=True)
             + jnp.exp(sink - m))
        o = (_dot(e_loc.astype(BF16), v_loc) + _dot(e_ctx.astype(BF16), v_ctx)) * (1.0 / l)
        for p in range(N_SLOTS // 2):
            lo = o[(2 * p) * C_WINDOW:(2 * p + 1) * C_WINDOW]
            hi = o[(2 * p + 1) * C_WINDOW:(2 * p + 2) * C_WINDOW]
            o_ref[0, jb * C_WINDOW:(jb + 1) * C_WINDOW, p * PAIR:(p + 1) * PAIR] = (
                jnp.where(low_c, lo, hi).astype(BF16))

    n_rows = tq // GRID_W
    n_loc = kh * GRID_W
    shape = (PAIR, n_loc)
    cq = lax.broadcasted_iota(jnp.int32, shape, 0) & (GRID_W - 1)
    ck = lax.broadcasted_iota(jnp.int32, shape, 1) & (GRID_W - 1)
    cs = jnp.clip(cq - NA_COLS // 2, 0, GRID_W - NA_COLS)
    col_valid = (ck >= cs) & (ck < cs + NA_COLS)
    low_d = _lane_is_low((GRID_W, PAIR))
    for jr in range(n_rows):
        r = i * n_rows + jr
        rs = jnp.clip(r - kh // 2, 0, rows - kh)
        bias_row = rs - r + (NA_ROWS - 1)
        kstart = pl.multiple_of(rs * GRID_W, GRID_W)
        for p in range(N_SLOTS // 2):
            cols = slice(p * PAIR, (p + 1) * PAIR)
            q = q2_ref[0, 2 * p:2 * p + 2, jr * GRID_W:(jr + 1) * GRID_W, :].reshape(PAIR, PAIR)
            k_loc = k2x_ref[0, pl.ds(kstart, n_loc), cols]
            v_loc = v2x_ref[0, pl.ds(kstart, n_loc), cols]
            bias = jnp.concatenate([tp_ref[p, bias_row + 2 * kp] for kp in range(kh // 2)], axis=1)
            s_loc = jnp.where(col_valid, _dot_nt(q, k_loc) + bias, NEG)
            s_ctx = _dot_nt(q, k2c_ref[0, :, cols])
            m = jnp.maximum(jnp.max(s_loc, axis=-1, keepdims=True), jnp.max(s_ctx, axis=-1, keepdims=True))
            e_loc = jnp.exp(s_loc - m)
            e_ctx = jnp.exp(s_ctx - m)
            l = jnp.sum(e_loc, axis=-1, keepdims=True) + jnp.sum(e_ctx, axis=-1, keepdims=True)
            o = (_dot(e_loc.astype(BF16), v_loc) + _dot(e_ctx.astype(BF16), v2c_ref[0, :, cols])) * (1.0 / l)
            o_ref[0, jr * GRID_W:(jr + 1) * GRID_W, Q_WIDTH + p * PAIR:Q_WIDTH + (p + 1) * PAIR] = (
                jnp.where(low_d, o[0:GRID_W], o[GRID_W:2 * GRID_W]).astype(BF16))


def _attn_odd(q1, q2, kvx, kvc, k2x, v2x, k2c, v2c, sink, tp, *, tq):
    b, _, s, _ = q1.shape
    slot_spec = pl.BlockSpec((1, N_SLOTS, tq, PAIR), lambda i, j: (i, 0, j, 0))

    def full(a):
        return pl.BlockSpec((1,) + a.shape[1:], lambda i, j: (i, 0, 0))

    return pl.pallas_call(
        functools.partial(_attn_odd_kernel, tq=tq),
        grid=(b, s // tq),
        in_specs=[slot_spec, slot_spec, full(kvx), full(kvc), full(k2x), full(v2x), full(k2c), full(v2c),
                  pl.BlockSpec(memory_space=pltpu.SMEM),
                  pl.BlockSpec(tp.shape, lambda i, j: (0, 0, 0, 0))],
        out_specs=pl.BlockSpec((1, tq, 2 * Q_WIDTH), lambda i, j: (i, j, 0)),
        out_shape=jax.ShapeDtypeStruct((b, s, 2 * Q_WIDTH), BF16),
        compiler_params=_compiler_params(2),
        name="attn_odd",
    )(q1, q2, kvx, kvc, k2x, v2x, k2c, v2c, sink, tp)


def _post_kernel(m_ref, x_ref, mod_ref, g_ref, wo_ref, w1_ref, w2_ref, o_ref, *, ff_chunk):
    d = x_ref.shape[-1]

    def mod(k):
        return mod_ref[0, :, k * d:(k + 1) * d]

    x1 = x_ref[0] + mod(2) * _rms(_dot(m_ref[0], wo_ref[...]), g_ref[1:2])
    h = (_rms(x1, g_ref[2:3]) * (1.0 + mod(4)) + mod(3)).astype(BF16)
    f = None
    for c in range(w1_ref.shape[1] // ff_chunk):
        cols = slice(c * ff_chunk, (c + 1) * ff_chunk)
        a = jnp.maximum(_dot(h, w1_ref[:, cols]), 0.0)
        part = _dot((a * a).astype(BF16), w2_ref[cols, :])
        f = part if f is None else f + part
    o_ref[0] = x1 + mod(5) * _rms(f, g_ref[3:4])


def _post(m, x, mod, gains, wo, w1, w2, *, tm, ff_chunk):
    b, s, d = x.shape
    bm = mod.shape[0]
    mod_map = (lambda i, j: (i, 0, 0)) if bm > 1 else (lambda i, j: (0, 0, 0))

    def const(a):
        return pl.BlockSpec(a.shape, lambda i, j: (0, 0), pipeline_mode=pl.Buffered(1))

    return pl.pallas_call(
        functools.partial(_post_kernel, ff_chunk=ff_chunk),
        grid=(b, s // tm),
        in_specs=[pl.BlockSpec((1, tm, m.shape[-1]), lambda i, j: (i, j, 0)),
                  pl.BlockSpec((1, tm, d), lambda i, j: (i, j, 0)),
                  pl.BlockSpec((1, 1, mod.shape[-1]), mod_map),
                  pl.BlockSpec(gains.shape, lambda i, j: (0, 0)),
                  const(wo), const(w1), const(w2)],
        out_specs=pl.BlockSpec((1, tm, d), lambda i, j: (i, j, 0)),
        out_shape=jax.ShapeDtypeStruct((b, s, d), F32),
        compiler_params=_compiler_params(2),
        name="post",
    )(m, x, mod, gains, wo, w1, w2)


def _rope_tables(t):
    tok = jnp.arange(t, dtype=jnp.int32)
    pos = jnp.stack([tok // GRID_W, tok % GRID_W], axis=-1).astype(F32)
    inv = ROPE_THETA ** (-jnp.arange(ROPE_FREQS, dtype=F32) / ROPE_FREQS)
    ang = pos[..., None] * inv
    cos, sin = jnp.cos(ang), jnp.sin(ang)
    zero = jnp.zeros_like(sin[:, 0])
    c = jnp.concatenate([cos[:, 0], cos[:, 0], cos[:, 1], cos[:, 1]], axis=-1)
    up = jnp.concatenate([-sin[:, 0], zero, -sin[:, 1], zero], axis=-1)
    down = jnp.concatenate([zero, sin[:, 0], zero, sin[:, 1]], axis=-1)
    return jnp.stack([jnp.tile(a, (1, 2)) for a in (c, up, down)])


def _permute_heads(w, axis):
    idx = np.concatenate([np.arange(h * HEAD_DIM, (h + 1) * HEAD_DIM) for h in GQA_HEAD_ORDER]
                         + [np.arange(Q_WIDTH, w.shape[axis])])
    return jnp.take(w, idx, axis=axis)


def _lambda_init(layer):
    return 0.8 - 0.6 * math.exp(-0.3 * layer)


def kernel(x, c, ctx, c_ctx, w_mod, b_mod, norm_g, w_in, w_out, w_mlp_in, w_mlp_out, qk_norm_a, diff_lambda,
           diff_subln, sink_c, rpb_d):
    b, t, d = x.shape
    depth = w_mod.shape[0]
    mod_rows = 16
    cc = jnp.concatenate([c, c_ctx[None], jnp.zeros((mod_rows - b - 1, d), F32)], axis=0)
    mod_all = _modulation(cc, w_mod, b_mod)

    rope_tab = _rope_tables(t)
    seg = jnp.kron(jnp.eye(2, dtype=F32), jnp.ones((HEAD_DIM, HEAD_DIM), F32)).astype(BF16)
    ones_gain = jnp.ones((2, PAIR), F32)

    for l in range(depth):
        last = l == depth - 1
        even = l % 2 == 0
        i = l // 2
        mod_x = mod_all[l, :b][:, None, :]
        mod_c = mod_all[l, b][None, None, :]
        w = _permute_heads(w_in[l], 1).astype(BF16)
        wo = _permute_heads(w_out[l], 0).astype(BF16)
        w1 = w_mlp_in[l].astype(BF16)
        w2 = w_mlp_out[l].astype(BF16)
        g0 = norm_g[l, 0][None, :]
        qk_gain = jnp.tile(qk_norm_a[i], (1, 2)) if even else ones_gain

        px = _in_proj(x, mod_x, g0, w, qk_gain, seg, rope_tab, even=even, tm=512)
        pc = _in_proj(ctx, mod_c, g0, w, qk_gain, seg, None, even=even, tm=ctx.shape[1])
        q1x, kvx, q2x, k2x, v2x = px
        q1c, kvc, q2c, k2c, v2c = pc
        if even:
            lam0 = _lambda_init(l)
            sub = diff_subln[i][None, :]
            m_x = _attn_even(q1x, q2x, kvc, k2c, v2c, kvx, k2x, v2x, diff_lambda[i], sub, lam0=lam0, tq=128)
            if not last:
                m_c = _attn_even(q1c, q2c, kvc, k2c, v2c, None, None, None, diff_lambda[i], sub,
                                 lam0=lam0, tq=128)
        else:
            tp = _bias_table(rpb_d[i])
            m_x = _attn_odd(q1x, q2x, kvx, kvc, k2x, v2x, k2c, v2c, sink_c[i], tp, tq=256)
            if not last:
                raise NotImplementedError("context update after an odd layer is not needed for depth 2")
        x = _post(m_x, x, mod_x, norm_g[l], wo, w1, w2, tm=512, ff_chunk=1024)
        if not last:
            ctx = _post(m_c, ctx, mod_c, norm_g[l], wo, w1, w2, tm=ctx.shape[1], ff_chunk=1024)
    return x
```

```python
import functools
import math

import jax
import jax.numpy as jnp
from jax import lax
from jax.experimental import pallas as pl
from jax.experimental.pallas import tpu as pltpu

F32 = jnp.float32
BF16 = jnp.bfloat16

GRID_W = 64
HEAD_DIM = 64
ROPE_THETA = 10000.0
ROPE_FREQS = HEAD_DIM // 4
N_SLOTS = 8
PAIR = 2 * HEAD_DIM
Q_WIDTH = N_SLOTS * HEAD_DIM
LANES = 128
MXU_QUERY_COLS = 512
C_WINDOW = 128
NA_ROWS = 8
NA_COLS = 16
EPS = 1e-6
NEG = -1e30
Q_SCALE = HEAD_DIM ** -0.5
GQA_SLOT_HEADS = tuple(half * (N_SLOTS // 2) + p for p in range(N_SLOTS // 2) for half in range(2))

VMEM_LIMIT_BYTES = 56 * 1024 * 1024


def _compiler_params(n_grid_dims):
    return pltpu.CompilerParams(dimension_semantics=("arbitrary",) * n_grid_dims,
                                vmem_limit_bytes=VMEM_LIMIT_BYTES)


def _dot(a, b):
    return jnp.dot(a, b, preferred_element_type=F32)


def _dot_nt(a, b):
    return lax.dot_general(a, b, (((1,), (1,)), ((), ())), preferred_element_type=F32)


def _log2(n):
    assert n & (n - 1) == 0
    return n.bit_length() - 1


def _rms(x, gain):
    ms = jnp.mean(x * x, axis=-1, keepdims=True)
    return x * lax.rsqrt(ms + EPS) * gain


def _mod_kernel(c_ref, w_ref, b_ref, o_ref):
    c = c_ref[...]
    s = c * jax.nn.sigmoid(c)
    o_ref[0] = _dot(s.astype(BF16), w_ref[0].astype(BF16)) + b_ref[0]


def _modulation(cc, w_mod, b_mod):
    depth, d, n = w_mod.shape
    rows = cc.shape[0]
    tn = n // 4
    return pl.pallas_call(
        _mod_kernel,
        grid=(depth, n // tn),
        in_specs=[pl.BlockSpec((rows, d), lambda l, j: (0, 0)),
                  pl.BlockSpec((1, d, tn), lambda l, j: (l, 0, j)),
                  pl.BlockSpec((1, 1, tn), lambda l, j: (l, 0, j))],
        out_specs=pl.BlockSpec((1, rows, tn), lambda l, j: (l, 0, j)),
        out_shape=jax.ShapeDtypeStruct((depth, rows, n), F32),
        compiler_params=_compiler_params(2),
        name="modulation",
    )(cc, w_mod, b_mod.reshape(depth, 1, n))


def _in_kernel(*refs, even, rope):
    refs = list(refs)
    tok_ref, mod_ref, g_ref, wt_ref, wk_ref = refs[:5]
    del refs[:5]
    if even:
        gq_ref, gk_ref, seg_ref = refs[:3]
        del refs[:3]
    if rope:
        rope_t_ref, rope_ref = refs[:2]
        del refs[:2]
    q1_ref, k1_ref, v1_ref, q2_ref, k2_ref, v2_ref = refs
    d = tok_ref.shape[-1]
    tm = tok_ref.shape[1]

    h = _rms(tok_ref[0], g_ref[...])
    h = h * (1.0 + mod_ref[0, :, d:2 * d]) + mod_ref[0, :, 0:d]
    hb = h.astype(BF16)

    def rotary_t(c):
        f = ROPE_FREQS
        out = []
        for a in range(2):
            x1, x2 = c[2 * a * f:(2 * a + 1) * f], c[(2 * a + 1) * f:(2 * a + 2) * f]
            cos, sin = rope_t_ref[0, a * f:(a + 1) * f], rope_t_ref[1, a * f:(a + 1) * f]
            out += [x1 * cos - x2 * sin, x2 * cos + x1 * sin]
        return jnp.concatenate(out, axis=0)

    def store_slots(q_ref, qt, use_norm, use_rope):
        zeros = jnp.zeros((HEAD_DIM, tm), F32)
        for j in range(N_SLOTS):
            c = qt[j * HEAD_DIM:(j + 1) * HEAD_DIM]
            if use_norm:
                ms = jnp.mean(c * c, axis=0, keepdims=True)
                c = c * lax.rsqrt(ms + EPS) * gq_ref[...]
            if use_rope:
                c = rotary_t(c)
            c = c * Q_SCALE
            slot = jnp.concatenate([c, zeros] if j % 2 == 0 else [zeros, c], axis=0)
            q_ref[0, j] = slot.astype(BF16)

    def store_vt(v_ref, vt):
        if even:
            v_ref[0] = vt.astype(BF16)
        else:
            for c in range(tm // LANES):
                v_ref[0, c] = vt[:, c * LANES:(c + 1) * LANES].astype(BF16)

    def rotary(c):
        up = pltpu.roll(c, PAIR - ROPE_FREQS, axis=1)
        down = pltpu.roll(c, ROPE_FREQS, axis=1)
        return c * rope_ref[0] + up * rope_ref[1] + down * rope_ref[2]

    o = 0
    store_slots(q1_ref, _dot_nt(wt_ref[o:o + Q_WIDTH], hb), even, rope)
    o += Q_WIDTH
    store_vt(v1_ref, _dot_nt(wt_ref[o:o + PAIR], hb))
    o += PAIR
    store_slots(q2_ref, _dot_nt(wt_ref[o:o + Q_WIDTH], hb), False, rope and even)
    o += Q_WIDTH
    store_vt(v2_ref, _dot_nt(wt_ref[o:o + Q_WIDTH], hb))

    k = _dot(hb, wk_ref[...])
    k1 = k[:, 0:PAIR]
    if even:
        sq = k1 * k1
        hi = sq.astype(BF16)
        lo = (sq - hi.astype(F32)).astype(BF16)
        ms = (_dot(hi, seg_ref[...]) + _dot(lo, seg_ref[...])) * (1.0 / HEAD_DIM)
        k1 = k1 * lax.rsqrt(ms + EPS) * gk_ref[...]
    if rope:
        k1 = rotary(k1)
    k1_ref[0] = k1.astype(BF16)
    for j in range(Q_WIDTH // PAIR):
        c = k[:, PAIR + j * PAIR:PAIR + (j + 1) * PAIR]
        if rope and even:
            c = rotary(c)
        k2_ref[0, :, j * PAIR:(j + 1) * PAIR] = c.astype(BF16)


def _in_proj(tok, mod, gain, wt, wk, norm_args, rope_args, *, even, tm):
    b, s, d = tok.shape
    bm = mod.shape[0]
    rope = rope_args is not None
    mod_map = (lambda i, j: (i, 0, 0)) if bm > 1 else (lambda i, j: (0, 0, 0))

    def const(a):
        return pl.BlockSpec(a.shape, lambda i, j: (0,) * a.ndim)

    in_specs = [pl.BlockSpec((1, tm, d), lambda i, j: (i, j, 0)),
                pl.BlockSpec((1, 1, mod.shape[-1]), mod_map),
                const(gain), const(wt), const(wk)]
    args = [tok, mod, gain, wt, wk]
    if even:
        gq, gk, seg = norm_args
        in_specs += [const(gq), const(gk), const(seg)]
        args += [gq, gk, seg]
    if rope:
        rope_t, rope_n = rope_args
        in_specs += [pl.BlockSpec((2, 2 * ROPE_FREQS, tm), lambda i, j: (0, 0, j)),
                     pl.BlockSpec((3, tm, PAIR), lambda i, j: (0, j, 0))]
        args += [rope_t, rope_n]

    def vt_out(rows):
        if even:
            return (pl.BlockSpec((1, rows, tm), lambda i, j: (i, 0, j)),
                    jax.ShapeDtypeStruct((b, rows, s), BF16))
        return (pl.BlockSpec((1, tm // LANES, rows, LANES), lambda i, j: (i, j, 0, 0)),
                jax.ShapeDtypeStruct((b, s // LANES, rows, LANES), BF16))

    slot_out = (pl.BlockSpec((1, N_SLOTS, PAIR, tm), lambda i, j: (i, 0, 0, j)),
                jax.ShapeDtypeStruct((b, N_SLOTS, PAIR, s), BF16))

    def nat_out(cols):
        return (pl.BlockSpec((1, tm, cols), lambda i, j: (i, j, 0)), jax.ShapeDtypeStruct((b, s, cols), BF16))

    outs = [slot_out, nat_out(PAIR), vt_out(PAIR), slot_out, nat_out(Q_WIDTH), vt_out(Q_WIDTH)]
    return pl.pallas_call(
        functools.partial(_in_kernel, even=even, rope=rope),
        grid=(b, s // tm),
        in_specs=in_specs,
        out_specs=[o[0] for o in outs],
        out_shape=[o[1] for o in outs],
        compiler_params=_compiler_params(2),
        name="in_proj_" + ("even" if even else "odd") + ("_x" if rope else "_ctx"),
    )(*args)


def _scores_t(qt, k_list):
    return [_dot(k, qt) for k in k_list]


def _softmax_t(s, sink=None):
    m = functools.reduce(jnp.maximum, [jnp.max(x, axis=0, keepdims=True) for x in s])
    if sink is not None:
        m = jnp.maximum(m, sink)
    e = [jnp.exp(x - m) for x in s]
    l = functools.reduce(jnp.add, [jnp.sum(x, axis=0, keepdims=True) for x in e])
    if sink is not None:
        l = l + jnp.exp(sink - m)
    return [x.astype(BF16) for x in e], 1.0 / l


def _softmax_pv_t(s, vt_list):
    e, r = _softmax_t(s)
    return functools.reduce(jnp.add, [_dot(vt, x) for vt, x in zip(vt_list, e)]) * r


def _pipelined(jobs):
    s = jobs[0][0]()
    for n, (_, finish) in enumerate(jobs):
        s_next = jobs[n + 1][0]() if n + 1 < len(jobs) else None
        finish(s)
        s = s_next


def _attn_even_kernel(*refs, with_x, lam0, tq):
    if with_x:
        q1_ref, q2_ref, k1c_ref, v1c_ref, k2c_ref, v2c_ref, k1x_ref, v1x_ref, k2x_ref, v2x_ref = refs[:10]
    else:
        q1_ref, q2_ref, k1c_ref, v1c_ref, k2c_ref, v2c_ref = refs[:6]
    lam_ref, sub_ref, o_ref = refs[-3:]

    group = max(2, MXU_QUERY_COLS // tq)
    jobs = []

    def a_scores(g):
        qt = jnp.concatenate([q1_ref[0, g * group + j] for j in range(group)], axis=1)
        return _scores_t(qt, [k1c_ref[0]] + ([k1x_ref[0]] if with_x else []))

    def a_finish(g, s):
        o = _softmax_pv_t(s, [v1c_ref[0]] + ([v1x_ref[0]] if with_x else []))
        for pp in range(group // 2):
            pair_t = jnp.concatenate([o[0:HEAD_DIM, (2 * pp) * tq:(2 * pp + 1) * tq],
                                      o[HEAD_DIM:PAIR, (2 * pp + 1) * tq:(2 * pp + 2) * tq]], axis=0)
            col = (g * (group // 2) + pp) * PAIR
            o_ref[0, :, col:col + PAIR] = pair_t.T.astype(BF16)

    for g in range(N_SLOTS // group):
        jobs.append((functools.partial(a_scores, g), functools.partial(a_finish, g)))

    def b_scores(h):
        rows = slice(h * PAIR, (h + 1) * PAIR)
        qt = jnp.concatenate([q2_ref[0, 2 * h], q2_ref[0, 2 * h + 1]], axis=1)
        return _scores_t(qt, [k2c_ref[0, :, rows]] + ([k2x_ref[0, :, rows]] if with_x else []))

    def b_finish(h, s):
        rows = slice(h * PAIR, (h + 1) * PAIR)
        o = _softmax_pv_t(s, [v2c_ref[0, rows]] + ([v2x_ref[0, rows]] if with_x else []))
        lp = lam_ref[...]
        lam = (jnp.exp(jnp.sum(lp[0:1] * lp[1:2], axis=-1, keepdims=True))
               - jnp.exp(jnp.sum(lp[2:3] * lp[3:4], axis=-1, keepdims=True)) + lam0)
        ob = (o[:, 0:tq] - lam * o[:, tq:2 * tq]).T
        ob = _rms(ob, sub_ref[...]) * (1.0 - lam0)
        col = Q_WIDTH + h * PAIR
        o_ref[0, :, col:col + PAIR] = ob.astype(BF16)

    for h in range(N_SLOTS // 2):
        jobs.append((functools.partial(b_scores, h), functools.partial(b_finish, h)))
    _pipelined(jobs)


def _attn_even(qx, ctx_kv, x_kv, lam_p, subln, *, lam0, tq):
    q1, q2 = qx
    b, _, _, s = q1.shape
    with_x = x_kv is not None
    slot_spec = pl.BlockSpec((1, N_SLOTS, PAIR, tq), lambda i, j: (i, 0, 0, j))

    def full(a):
        return pl.BlockSpec((1,) + a.shape[1:], lambda i, j: (i, 0, 0))

    kv = list(ctx_kv) + (list(x_kv) if with_x else [])
    return pl.pallas_call(
        functools.partial(_attn_even_kernel, with_x=with_x, lam0=lam0, tq=tq),
        grid=(b, s // tq),
        in_specs=[slot_spec, slot_spec] + [full(a) for a in kv]
                 + [pl.BlockSpec(lam_p.shape, lambda i, j: (0, 0)), pl.BlockSpec(subln.shape, lambda i, j: (0, 0))],
        out_specs=pl.BlockSpec((1, tq, 2 * Q_WIDTH), lambda i, j: (i, j, 0)),
        out_shape=jax.ShapeDtypeStruct((b, s, 2 * Q_WIDTH), BF16),
        compiler_params=_compiler_params(2),
        name="attn_even" + ("_x" if with_x else "_ctx"),
    )(q1, q2, *kv, lam_p, subln)


N_BIAS_ROWS = 2 * NA_ROWS


def _bias_kernel(rpb_ref, o_ref):
    head = pl.program_id(0)
    n_dr = 2 * NA_ROWS - 1
    n_dc = 2 * NA_COLS - 1
    shape = (GRID_W, LANES)
    ck = lax.broadcasted_iota(jnp.int32, shape, 0)
    lane = lax.broadcasted_iota(jnp.int32, shape, 1)
    second = lane >= GRID_W
    dcl = jnp.clip(ck - (lane & (GRID_W - 1)), -(NA_COLS - 1), NA_COLS - 1) + (NA_COLS - 1)
    for t in range(N_BIAS_ROWS):
        acc = jnp.zeros(shape, F32)
        for j in range(n_dc):
            first_v = rpb_ref[(head * n_dr + min(t, n_dr - 1)) * n_dc + j]
            second_v = rpb_ref[(head * n_dr + max(t - 1, 0)) * n_dc + j]
            acc = jnp.where(dcl == j, jnp.where(second, second_v, first_v), acc)
        o_ref[0, t] = acc


def _bias_table(rpb):
    heads = rpb.shape[0]
    return pl.pallas_call(
        _bias_kernel,
        grid=(heads,),
        in_specs=[pl.BlockSpec(memory_space=pltpu.SMEM)],
        out_specs=pl.BlockSpec((1, N_BIAS_ROWS, GRID_W, LANES), lambda h: (h, 0, 0, 0)),
        out_shape=jax.ShapeDtypeStruct((heads, N_BIAS_ROWS, GRID_W, LANES), F32),
        compiler_params=_compiler_params(1),
        name="rel_bias_table",
    )(rpb.reshape(-1))


def _attn_odd_kernel(q1_ref, q2_ref, k1x_ref, v1x_ref, k2x_ref, v2x_ref, k1c_ref, v1c_ref, k2c_ref, v2c_ref,
                     sink_ref, tp_ref, o_ref, *, tq, sink_order):
    i = pl.program_id(1)
    t = k1x_ref.shape[1]
    n_ctx_blk = v1c_ref.shape[1]
    rows = t // GRID_W
    kh = min(NA_ROWS, rows)
    n_blk = tq // LANES

    jobs = []

    n_q = N_SLOTS * C_WINDOW
    n_seq_blk = t // C_WINDOW
    lane_q = lax.broadcasted_iota(jnp.int32, (1, n_q), 1)
    sink = jnp.zeros((1, n_q), F32)
    for j in range(N_SLOTS):
        sink = jnp.where(lane_q >> _log2(C_WINDOW) == j, sink_ref[sink_order[j]], sink)
    tri = (lax.broadcasted_iota(jnp.int32, (C_WINDOW, n_q), 0)
           - (lax.broadcasted_iota(jnp.int32, (C_WINDOW, n_q), 1) & (C_WINDOW - 1)))

    def c_blocks(jb):
        blk = i * n_blk + jb
        return blk, jnp.maximum(blk - 1, 0), jnp.minimum(blk + 1, n_seq_blk - 1)

    def c_scores(jb):
        blk, prev, nxt = c_blocks(jb)
        qt = jnp.concatenate([q1_ref[0, j, :, jb * LANES:(jb + 1) * LANES] for j in range(N_SLOTS)], axis=1)
        k_loc = jnp.concatenate([k1x_ref[0, pl.ds(pl.multiple_of(n * C_WINDOW, C_WINDOW), C_WINDOW), :]
                                 for n in (prev, blk, nxt)], axis=0)
        return _scores_t(qt, [k_loc, k1c_ref[0]])

    def c_finish(jb, s):
        blk, prev, nxt = c_blocks(jb)
        s_loc, s_ctx = s
        lo_bound = jnp.where(blk > 0, 0, C_WINDOW)
        hi_bound = jnp.where(blk < n_seq_blk - 1, 0, -C_WINDOW)
        chunks = [jnp.where(tri >= lo_bound, s_loc[0:C_WINDOW], NEG),
                  s_loc[C_WINDOW:2 * C_WINDOW],
                  jnp.where(tri <= hi_bound, s_loc[2 * C_WINDOW:3 * C_WINDOW], NEG),
                  s_ctx]
        e, rl = _softmax_t(chunks, sink)
        vt_loc = jnp.concatenate([v1x_ref[0, n] for n in (prev, blk, nxt)], axis=1)
        vt_ctx = jnp.concatenate([v1c_ref[0, c] for c in range(n_ctx_blk)], axis=1)
        o = (_dot(vt_loc, jnp.concatenate(e[0:3], axis=0)) + _dot(vt_ctx, e[3])) * rl
        for p in range(N_SLOTS // 2):
            pair_t = jnp.concatenate([o[0:HEAD_DIM, (2 * p) * LANES:(2 * p + 1) * LANES],
                                      o[HEAD_DIM:PAIR, (2 * p + 1) * LANES:(2 * p + 2) * LANES]], axis=0)
            o_ref[0, jb * LANES:(jb + 1) * LANES, p * PAIR:(p + 1) * PAIR] = pair_t.T.astype(BF16)

    for jb in range(n_blk):
        jobs.append((functools.partial(c_scores, jb), functools.partial(c_finish, jb)))

    n_rows = tq // GRID_W
    span_rows = -(-(n_rows - 1 + kh) // 2) * 2
    n_loc = span_rows * GRID_W
    shape = (n_loc, 2 * tq)
    key_i = lax.broadcasted_iota(jnp.int32, shape, 0)
    lane_i = lax.broadcasted_iota(jnp.int32, shape, 1)
    kk = key_i >> _log2(GRID_W)
    ck = key_i & (GRID_W - 1)
    cq = lane_i & (GRID_W - 1)
    cs = jnp.clip(cq - NA_COLS // 2, 0, GRID_W - NA_COLS)
    r0 = i * n_rows
    u = jnp.minimum(jnp.clip(r0 - kh // 2, 0, rows - kh), rows - span_rows)
    q_row = (lax.broadcasted_iota(jnp.int32, (1, 2 * tq), 1) >> _log2(GRID_W)) & (n_rows - 1)
    first = jnp.zeros((1, 2 * tq), jnp.int32)
    for j in range(n_rows):
        first = jnp.where(q_row == j, jnp.clip(r0 + j - kh // 2, 0, rows - kh) - u, first)
    valid = (ck >= cs) & (ck < cs + NA_COLS) & (kk >= first) & (kk < first + kh)

    def d_scores(p):
        cols = slice(p * PAIR, (p + 1) * PAIR)
        qt = jnp.concatenate([q2_ref[0, 2 * p], q2_ref[0, 2 * p + 1]], axis=1)
        k_loc = k2x_ref[0, pl.ds(pl.multiple_of(u * GRID_W, LANES), n_loc), cols]
        return _scores_t(qt, [k_loc, k2c_ref[0, :, cols]])

    def d_finish(p, s):
        cols = slice(p * PAIR, (p + 1) * PAIR)
        s_loc, s_ctx = s
        bias = jnp.concatenate(
            [jnp.concatenate(
                [tp_ref[2 * p + lb // (n_rows // 2),
                        jnp.clip(u + a - (r0 + 2 * (lb % (n_rows // 2))) + NA_ROWS - 1, 0, N_BIAS_ROWS - 1)]
                 for lb in range(n_rows)], axis=1)
             for a in range(span_rows)], axis=0)
        e, rl = _softmax_t([jnp.where(valid, s_loc + bias, NEG), s_ctx])
        blk0 = u // (LANES // GRID_W)
        vt_loc = jnp.concatenate([v2x_ref[0, blk0 + c, cols, :] for c in range(n_loc // LANES)], axis=1)
        vt_ctx = jnp.concatenate([v2c_ref[0, c, cols, :] for c in range(n_ctx_blk)], axis=1)
        o = (_dot(vt_loc, e[0]) + _dot(vt_ctx, e[1])) * rl
        pair_t = jnp.concatenate([o[0:HEAD_DIM, 0:tq], o[HEAD_DIM:PAIR, tq:2 * tq]], axis=0)
        o_ref[0, :, Q_WIDTH + p * PAIR:Q_WIDTH + (p + 1) * PAIR] = pair_t.T.astype(BF16)

    for p in range(N_SLOTS // 2):
        jobs.append((functools.partial(d_scores, p), functools.partial(d_finish, p)))
    _pipelined(jobs)


def _attn_odd(qx, x_kv, ctx_kv, sink, tp, *, tq, sink_order):
    q1, q2 = qx
    b, _, _, s = q1.shape
    slot_spec = pl.BlockSpec((1, N_SLOTS, PAIR, tq), lambda i, j: (i, 0, 0, j))

    def full(a):
        return pl.BlockSpec((1,) + a.shape[1:], lambda i, j: (i,) + (0,) * (a.ndim - 1))

    kv = list(x_kv) + list(ctx_kv)
    return pl.pallas_call(
        functools.partial(_attn_odd_kernel, tq=tq, sink_order=sink_order),
        grid=(b, s // tq),
        in_specs=[slot_spec, slot_spec] + [full(a) for a in kv]
                 + [pl.BlockSpec(memory_space=pltpu.SMEM), pl.BlockSpec(tp.shape, lambda i, j: (0, 0, 0, 0))],
        out_specs=pl.BlockSpec((1, tq, 2 * Q_WIDTH), lambda i, j: (i, j, 0)),
        out_shape=jax.ShapeDtypeStruct((b, s, 2 * Q_WIDTH), BF16),
        compiler_params=_compiler_params(2),
        name="attn_odd",
    )(q1, q2, *kv, sink, tp)


def _post_kernel(m_ref, x_ref, mod_ref, g_ref, wo_ref, w1_ref, w2_ref, o_ref, *, ff_chunk):
    d = x_ref.shape[-1]

    def mod(k):
        return mod_ref[0, :, k * d:(k + 1) * d]

    x1 = x_ref[0] + mod(2) * _rms(_dot(m_ref[0], wo_ref[...]), g_ref[1:2])
    h = (_rms(x1, g_ref[2:3]) * (1.0 + mod(4)) + mod(3)).astype(BF16)
    f = None
    for c in range(w1_ref.shape[1] // ff_chunk):
        cols = slice(c * ff_chunk, (c + 1) * ff_chunk)
        a = jnp.maximum(_dot(h, w1_ref[:, cols]), 0.0)
        part = _dot((a * a).astype(BF16), w2_ref[cols, :])
        f = part if f is None else f + part
    o_ref[0] = x1 + mod(5) * _rms(f, g_ref[3:4])


def _post(m, x, mod, gains, wo, w1, w2, *, tm, ff_chunk):
    b, s, d = x.shape
    bm = mod.shape[0]
    mod_map = (lambda i, j: (i, 0, 0)) if bm > 1 else (lambda i, j: (0, 0, 0))

    def const(a):
        return pl.BlockSpec(a.shape, lambda i, j: (0, 0), pipeline_mode=pl.Buffered(1))

    return pl.pallas_call(
        functools.partial(_post_kernel, ff_chunk=ff_chunk),
        grid=(b, s // tm),
        in_specs=[pl.BlockSpec((1, tm, m.shape[-1]), lambda i, j: (i, j, 0)),
                  pl.BlockSpec((1, tm, d), lambda i, j: (i, j, 0)),
                  pl.BlockSpec((1, 1, mod.shape[-1]), mod_map),
                  pl.BlockSpec(gains.shape, lambda i, j: (0, 0)),
                  const(wo), const(w1), const(w2)],
        out_specs=pl.BlockSpec((1, tm, d), lambda i, j: (i, j, 0)),
        out_shape=jax.ShapeDtypeStruct((b, s, d), F32),
        compiler_params=_compiler_params(2),
        name="post",
    )(m, x, mod, gains, wo, w1, w2)


def _rope_tables(t):
    tok = jnp.arange(t, dtype=jnp.int32)
    pos = jnp.stack([tok // GRID_W, tok % GRID_W], axis=-1).astype(F32)
    inv = ROPE_THETA ** (-jnp.arange(ROPE_FREQS, dtype=F32) / ROPE_FREQS)
    ang = pos[..., None] * inv
    cos, sin = jnp.cos(ang), jnp.sin(ang)
    rope_t = jnp.stack([cos.reshape(t, -1).T, sin.reshape(t, -1).T])
    zero = jnp.zeros_like(sin[:, 0])
    c = jnp.concatenate([cos[:, 0], cos[:, 0], cos[:, 1], cos[:, 1]], axis=-1)
    up = jnp.concatenate([-sin[:, 0], zero, -sin[:, 1], zero], axis=-1)
    down = jnp.concatenate([zero, sin[:, 0], zero, sin[:, 1]], axis=-1)
    rope_n = jnp.stack([jnp.tile(a, (1, PAIR // HEAD_DIM)) for a in (c, up, down)])
    return rope_t, rope_n


def _pair_gqa_heads(w, axis):
    shape = w.shape
    w = w.reshape(shape[:axis] + (2, N_SLOTS // 2, HEAD_DIM) + shape[axis + 1:])
    return jnp.swapaxes(w, axis, axis + 1).reshape(shape)


def _lambda_init(layer):
    return 0.8 - 0.6 * math.exp(-0.3 * layer)


def kernel(x, c, ctx, c_ctx, w_mod, b_mod, norm_g, w_in, w_out, w_mlp_in, w_mlp_out, qk_norm_a, diff_lambda,
           diff_subln, sink_c, rpb_d):
    b, t, d = x.shape
    n_ctx = ctx.shape[1]
    depth = w_mod.shape[0]
    mod_rows = 16
    cc = jnp.concatenate([c, c_ctx[None], jnp.zeros((mod_rows - b - 1, d), F32)], axis=0)
    mod_all = _modulation(cc, w_mod, b_mod)

    rope_args = _rope_tables(t)
    seg = jnp.kron(jnp.eye(2, dtype=F32), jnp.ones((HEAD_DIM, HEAD_DIM), F32)).astype(BF16)
    tm_x = 512

    for l in range(depth):
        last = l == depth - 1
        even = l % 2 == 0
        i = l // 2
        mod_x = mod_all[l, :b][:, None, :]
        mod_c = mod_all[l, b][None, None, :]
        w = w_in[l]
        o_k1, o_v1, o_q2, o_k2, o_v2 = Q_WIDTH, Q_WIDTH + PAIR, Q_WIDTH + 2 * PAIR, 2 * Q_WIDTH + 2 * PAIR, \
            3 * Q_WIDTH + 2 * PAIR
        wt = jnp.concatenate([_pair_gqa_heads(w[:, 0:o_k1], 1), w[:, o_v1:o_q2], w[:, o_q2:o_k2], w[:, o_v2:]],
                             axis=1).T.astype(BF16)
        wk = jnp.concatenate([w[:, o_k1:o_v1], w[:, o_k2:o_v2]], axis=1).astype(BF16)
        wo = jnp.concatenate([_pair_gqa_heads(w_out[l, 0:Q_WIDTH], 0), w_out[l, Q_WIDTH:]], axis=0).astype(BF16)
        w1 = w_mlp_in[l].astype(BF16)
        w2 = w_mlp_out[l].astype(BF16)
        g0 = norm_g[l, 0][None, :]

        def norm_args(tm):
            if not even:
                return None
            return (jnp.broadcast_to(qk_norm_a[i, 0][:, None], (HEAD_DIM, tm)),
                    jnp.tile(qk_norm_a[i, 1], PAIR // HEAD_DIM)[None, :], seg)

        q1x, k1x, v1x, q2x, k2x, v2x = _in_proj(x, mod_x, g0, wt, wk, norm_args(tm_x), rope_args,
                                                even=even, tm=tm_x)
        q1c, k1c, v1c, q2c, k2c, v2c = _in_proj(ctx, mod_c, g0, wt, wk, norm_args(n_ctx), None,
                                                even=even, tm=n_ctx)
        x_kv = (k1x, v1x, k2x, v2x)
        ctx_kv = (k1c, v1c, k2c, v2c)
        if even:
            lam0 = _lambda_init(l)
            sub = diff_subln[i][None, :]
            m_x = _attn_even((q1x, q2x), ctx_kv, x_kv, diff_lambda[i], sub, lam0=lam0, tq=256)
            if not last:
                m_c = _attn_even((q1c, q2c), ctx_kv, None, diff_lambda[i], sub, lam0=lam0, tq=256)
        else:
            if not last:
                raise NotImplementedError("context update after an odd layer is not needed for depth 2")
            tp = _bias_table(rpb_d[i])
            m_x = _attn_odd((q1x, q2x), x_kv, ctx_kv, sink_c[i], tp, tq=256, sink_order=GQA_SLOT_HEADS)
        x = _post(m_x, x, mod_x, norm_g[l], wo, w1, w2, tm=512, ff_chunk=1024)
        if not last:
            ctx = _post(m_c, ctx, mod_c, norm_g[l], wo, w1, w2, tm=n_ctx, ff_chunk=1024)
    return x
```

```python
import functools
import math
from typing import Any, Callable, NamedTuple, Sequence

import jax
import jax.numpy as jnp
from jax import lax
from jax.experimental import pallas as pl
from jax.experimental.pallas import tpu as pltpu

F32 = jnp.float32
BF16 = jnp.bfloat16

GRID_W = 64
HEAD_DIM = 64
ROPE_THETA = 10000.0
ROPE_FREQS = HEAD_DIM // 4
N_SLOTS = 8
PAIR = 2 * HEAD_DIM
Q_WIDTH = N_SLOTS * HEAD_DIM
LANES = 128
MXU_QUERY_COLS = 512
EVEN_KEY_CHUNK = 512
NA_CHUNK_ROWS = 4
C_WINDOW = 128
NA_ROWS = 8
NA_COLS = 16
EPS = 1e-6
NEG = -1e30
LOG2E = math.log2(math.e)
Q_SCALE = HEAD_DIM ** -0.5 * LOG2E
GQA_SLOT_HEADS = tuple(half * (N_SLOTS // 2) + p for p in range(N_SLOTS // 2) for half in range(2))

VMEM_LIMIT_BYTES = 56 * 1024 * 1024


def _compiler_params(n_grid_dims):
    return pltpu.CompilerParams(dimension_semantics=("arbitrary",) * n_grid_dims,
                                vmem_limit_bytes=VMEM_LIMIT_BYTES)


def _dot(a, b):
    return jnp.dot(a, b, preferred_element_type=F32)


def _dot_nt(a, b):
    return lax.dot_general(a, b, (((1,), (1,)), ((), ())), preferred_element_type=F32)


def _log2(n):
    assert n & (n - 1) == 0
    return n.bit_length() - 1


def _rms(x, gain):
    ms = jnp.mean(x * x, axis=-1, keepdims=True)
    return x * lax.rsqrt(ms + EPS) * gain


def _mod_kernel(c_ref, w_ref, b_ref, o_ref):
    c = c_ref[...]
    s = c * jax.nn.sigmoid(c)
    o_ref[0] = _dot(s.astype(BF16), w_ref[0].astype(BF16)) + b_ref[0]


def _modulation(cc, w_mod, b_mod):
    depth, d, n = w_mod.shape
    rows = cc.shape[0]
    tn = n // 4
    return pl.pallas_call(
        _mod_kernel,
        grid=(depth, n // tn),
        in_specs=[pl.BlockSpec((rows, d), lambda l, j: (0, 0)),
                  pl.BlockSpec((1, d, tn), lambda l, j: (l, 0, j)),
                  pl.BlockSpec((1, 1, tn), lambda l, j: (l, 0, j))],
        out_specs=pl.BlockSpec((1, rows, tn), lambda l, j: (l, 0, j)),
        out_shape=jax.ShapeDtypeStruct((depth, rows, n), F32),
        compiler_params=_compiler_params(2),
        name="modulation",
    )(cc, w_mod, b_mod.reshape(depth, 1, n))


def _in_kernel(*refs, even, rope):
    refs = list(refs)
    tok_ref, mod_ref, g_ref, wt_ref, wk_ref = refs[:5]
    del refs[:5]
    if even:
        gq_ref, gk_ref, seg_ref = refs[:3]
        del refs[:3]
    if rope:
        rope_t_ref, rope_ref = refs[:2]
        del refs[:2]
    q1_ref, k1_ref, v1_ref, q2_ref, k2_ref, v2_ref = refs
    d = tok_ref.shape[-1]
    tm = tok_ref.shape[1]

    h = _rms(tok_ref[0], g_ref[...])
    h = h * (1.0 + mod_ref[0, :, d:2 * d]) + mod_ref[0, :, 0:d]
    hb = h.astype(BF16)

    def rotary_t(c):
        f = ROPE_FREQS
        out = []
        for a in range(2):
            x1, x2 = c[2 * a * f:(2 * a + 1) * f], c[(2 * a + 1) * f:(2 * a + 2) * f]
            cos, sin = rope_t_ref[0, a * f:(a + 1) * f], rope_t_ref[1, a * f:(a + 1) * f]
            out += [x1 * cos - x2 * sin, x2 * cos + x1 * sin]
        return jnp.concatenate(out, axis=0)

    def store_slots(q_ref, qt, use_norm, use_rope):
        zeros = jnp.zeros((HEAD_DIM, tm), F32)
        for j in range(N_SLOTS):
            c = qt[j * HEAD_DIM:(j + 1) * HEAD_DIM]
            if use_norm:
                ms = jnp.mean(c * c, axis=0, keepdims=True)
                c = c * lax.rsqrt(ms + EPS) * gq_ref[...]
            if use_rope:
                c = rotary_t(c)
            c = c * Q_SCALE
            slot = jnp.concatenate([c, zeros] if j % 2 == 0 else [zeros, c], axis=0)
            q_ref[0, j] = slot.astype(BF16)

    def store_vt(v_ref, vt):
        if even:
            v_ref[0] = vt.astype(BF16)
        else:
            for c in range(tm // LANES):
                v_ref[0, c] = vt[:, c * LANES:(c + 1) * LANES].astype(BF16)

    def rotary(c):
        up = pltpu.roll(c, PAIR - ROPE_FREQS, axis=1)
        down = pltpu.roll(c, ROPE_FREQS, axis=1)
        return c * rope_ref[0] + up * rope_ref[1] + down * rope_ref[2]

    o = 0
    store_slots(q1_ref, _dot_nt(wt_ref[o:o + Q_WIDTH], hb), even, rope)
    o += Q_WIDTH
    store_vt(v1_ref, _dot_nt(wt_ref[o:o + PAIR], hb))
    o += PAIR
    store_slots(q2_ref, _dot_nt(wt_ref[o:o + Q_WIDTH], hb), False, rope and even)
    o += Q_WIDTH
    store_vt(v2_ref, _dot_nt(wt_ref[o:o + Q_WIDTH], hb))

    k = _dot(hb, wk_ref[...])
    k1 = k[:, 0:PAIR]
    if even:
        sq = k1 * k1
        hi = sq.astype(BF16)
        lo = (sq - hi.astype(F32)).astype(BF16)
        ms = (_dot(hi, seg_ref[...]) + _dot(lo, seg_ref[...])) * (1.0 / HEAD_DIM)
        k1 = k1 * lax.rsqrt(ms + EPS) * gk_ref[...]
    if rope:
        k1 = rotary(k1)
    k1_ref[0] = k1.astype(BF16)
    for j in range(Q_WIDTH // PAIR):
        c = k[:, PAIR + j * PAIR:PAIR + (j + 1) * PAIR]
        if rope and even:
            c = rotary(c)
        k2_ref[0, :, j * PAIR:(j + 1) * PAIR] = c.astype(BF16)


def _in_proj(tok, mod, gain, wt, wk, norm_args, rope_args, *, even, tm):
    b, s, d = tok.shape
    bm = mod.shape[0]
    rope = rope_args is not None
    mod_map = (lambda i, j: (i, 0, 0)) if bm > 1 else (lambda i, j: (0, 0, 0))

    def const(a):
        return pl.BlockSpec(a.shape, lambda i, j: (0,) * a.ndim)

    in_specs = [pl.BlockSpec((1, tm, d), lambda i, j: (i, j, 0)),
                pl.BlockSpec((1, 1, mod.shape[-1]), mod_map),
                const(gain), const(wt), const(wk)]
    args = [tok, mod, gain, wt, wk]
    if even:
        gq, gk, seg = norm_args
        in_specs += [const(gq), const(gk), const(seg)]
        args += [gq, gk, seg]
    if rope:
        rope_t, rope_n = rope_args
        in_specs += [pl.BlockSpec((2, 2 * ROPE_FREQS, tm), lambda i, j: (0, 0, j)),
                     pl.BlockSpec((3, tm, PAIR), lambda i, j: (0, j, 0))]
        args += [rope_t, rope_n]

    def vt_out(rows):
        if even:
            return (pl.BlockSpec((1, rows, tm), lambda i, j: (i, 0, j)),
                    jax.ShapeDtypeStruct((b, rows, s), BF16))
        return (pl.BlockSpec((1, tm // LANES, rows, LANES), lambda i, j: (i, j, 0, 0)),
                jax.ShapeDtypeStruct((b, s // LANES, rows, LANES), BF16))

    slot_out = (pl.BlockSpec((1, N_SLOTS, PAIR, tm), lambda i, j: (i, 0, 0, j)),
                jax.ShapeDtypeStruct((b, N_SLOTS, PAIR, s), BF16))

    def nat_out(cols):
        return (pl.BlockSpec((1, tm, cols), lambda i, j: (i, j, 0)), jax.ShapeDtypeStruct((b, s, cols), BF16))

    outs = [slot_out, nat_out(PAIR), vt_out(PAIR), slot_out, nat_out(Q_WIDTH), vt_out(Q_WIDTH)]
    return pl.pallas_call(
        functools.partial(_in_kernel, even=even, rope=rope),
        grid=(b, s // tm),
        in_specs=in_specs,
        out_specs=[o[0] for o in outs],
        out_shape=[o[1] for o in outs],
        compiler_params=_compiler_params(2),
        name="in_proj_" + ("even" if even else "odd") + ("_x" if rope else "_ctx"),
    )(*args)


class AttnJob(NamedTuple):
    query: Callable[[], Any]
    scores: Sequence[Callable[[Any], Any]]
    values: Sequence[Callable[[], Any]]
    finish: Callable[[Any], None]
    sink: Any = None


def _attend_jobs(jobs):
    items = [(j, c) for j, job in enumerate(jobs) for c in range(len(job.scores))]
    lead = max(len(job.scores) for job in jobs) + 1
    qt, s, m, l, o = {}, {}, {}, {}, {}
    for t in range(len(items) + lead):
        if t < len(items):
            j, c = items[t]
            job = jobs[j]
            if c == 0:
                qt[j] = job.query()
            x = job.scores[c](qt[j])
            s[j, c] = x
            part_m = jnp.max(x, axis=0, keepdims=True)
            if c == 0:
                m[j] = part_m if job.sink is None else jnp.maximum(part_m, job.sink)
            else:
                m[j] = jnp.maximum(m[j], part_m)
        if t >= lead:
            j, c = items[t - lead]
            job = jobs[j]
            e = jnp.exp2(s.pop((j, c)) - m[j])
            part_l = jnp.sum(e, axis=0, keepdims=True)
            part_o = _dot(job.values[c](), e.astype(BF16))
            l[j] = part_l if c == 0 else l[j] + part_l
            o[j] = part_o if c == 0 else o[j] + part_o
            if c == len(job.scores) - 1:
                denom = l.pop(j) if job.sink is None else l.pop(j) + jnp.exp2(job.sink - m[j])
                job.finish(o.pop(j) * (1.0 / denom))


def _attn_even_kernel(*refs, with_x, lam0, tq, key_chunk):
    if with_x:
        q1_ref, q2_ref, k1c_ref, v1c_ref, k2c_ref, v2c_ref, k1x_ref, v1x_ref, k2x_ref, v2x_ref = refs[:10]
    else:
        q1_ref, q2_ref, k1c_ref, v1c_ref, k2c_ref, v2c_ref = refs[:6]
    lam_ref, sub_ref, o_ref = refs[-3:]

    def chunk_fns(c_ref, x_ref, cols, transposed):
        def make(ref, lo, hi):
            if transposed:
                return lambda: ref[0, cols, lo:hi]
            return lambda qt: _dot(ref[0, lo:hi, cols], qt)
        n_ctx = c_ref.shape[2] if transposed else c_ref.shape[1]
        fns = [make(c_ref, 0, n_ctx)]
        if with_x:
            n_x = x_ref.shape[2] if transposed else x_ref.shape[1]
            fns += [make(x_ref, lo, lo + key_chunk) for lo in range(0, n_x, key_chunk)]
        return fns

    group = max(2, MXU_QUERY_COLS // tq)
    jobs = []
    pair_cols = slice(0, PAIR)

    def a_query(g):
        return jnp.concatenate([q1_ref[0, g * group + j] for j in range(group)], axis=1)

    def a_finish(g, o):
        for pp in range(group // 2):
            pair_t = jnp.concatenate([o[0:HEAD_DIM, (2 * pp) * tq:(2 * pp + 1) * tq],
                                      o[HEAD_DIM:PAIR, (2 * pp + 1) * tq:(2 * pp + 2) * tq]], axis=0)
            col = (g * (group // 2) + pp) * PAIR
            o_ref[0, :, col:col + PAIR] = pair_t.T.astype(BF16)

    for g in range(N_SLOTS // group):
        jobs.append(AttnJob(functools.partial(a_query, g),
                            chunk_fns(k1c_ref, k1x_ref if with_x else None, pair_cols, False),
                            chunk_fns(v1c_ref, v1x_ref if with_x else None, pair_cols, True),
                            functools.partial(a_finish, g)))

    def b_query(h):
        return jnp.concatenate([q2_ref[0, 2 * h], q2_ref[0, 2 * h + 1]], axis=1)

    def b_finish(h, o):
        lp = lam_ref[...]
        lam = (jnp.exp(jnp.sum(lp[0:1] * lp[1:2], axis=-1, keepdims=True))
               - jnp.exp(jnp.sum(lp[2:3] * lp[3:4], axis=-1, keepdims=True)) + lam0)
        ob = (o[:, 0:tq] - lam * o[:, tq:2 * tq]).T
        ob = _rms(ob, sub_ref[...]) * (1.0 - lam0)
        col = Q_WIDTH + h * PAIR
        o_ref[0, :, col:col + PAIR] = ob.astype(BF16)

    for h in range(N_SLOTS // 2):
        cols = slice(h * PAIR, (h + 1) * PAIR)
        jobs.append(AttnJob(functools.partial(b_query, h),
                            chunk_fns(k2c_ref, k2x_ref if with_x else None, cols, False),
                            chunk_fns(v2c_ref, v2x_ref if with_x else None, cols, True),
                            functools.partial(b_finish, h)))
    _attend_jobs(jobs)


def _attn_even(qx, ctx_kv, x_kv, lam_p, subln, *, lam0, tq):
    q1, q2 = qx
    b, _, _, s = q1.shape
    with_x = x_kv is not None
    slot_spec = pl.BlockSpec((1, N_SLOTS, PAIR, tq), lambda i, j: (i, 0, 0, j))

    def full(a):
        return pl.BlockSpec((1,) + a.shape[1:], lambda i, j: (i, 0, 0))

    kv = list(ctx_kv) + (list(x_kv) if with_x else [])
    return pl.pallas_call(
        functools.partial(_attn_even_kernel, with_x=with_x, lam0=lam0, tq=tq, key_chunk=EVEN_KEY_CHUNK),
        grid=(b, s // tq),
        in_specs=[slot_spec, slot_spec] + [full(a) for a in kv]
                 + [pl.BlockSpec(lam_p.shape, lambda i, j: (0, 0)), pl.BlockSpec(subln.shape, lambda i, j: (0, 0))],
        out_specs=pl.BlockSpec((1, tq, 2 * Q_WIDTH), lambda i, j: (i, j, 0)),
        out_shape=jax.ShapeDtypeStruct((b, s, 2 * Q_WIDTH), BF16),
        compiler_params=_compiler_params(2),
        name="attn_even" + ("_x" if with_x else "_ctx"),
    )(q1, q2, *kv, lam_p, subln)


N_BIAS_ROWS = 2 * NA_ROWS


def _bias_kernel(rpb_ref, o_ref):
    head = pl.program_id(0)
    n_dr = 2 * NA_ROWS - 1
    n_dc = 2 * NA_COLS - 1
    shape = (GRID_W, LANES)
    ck = lax.broadcasted_iota(jnp.int32, shape, 0)
    lane = lax.broadcasted_iota(jnp.int32, shape, 1)
    second = lane >= GRID_W
    dcl = jnp.clip(ck - (lane & (GRID_W - 1)), -(NA_COLS - 1), NA_COLS - 1) + (NA_COLS - 1)
    for t in range(N_BIAS_ROWS):
        acc = jnp.zeros(shape, F32)
        for j in range(n_dc):
            first_v = rpb_ref[(head * n_dr + min(t, n_dr - 1)) * n_dc + j]
            second_v = rpb_ref[(head * n_dr + max(t - 1, 0)) * n_dc + j]
            acc = jnp.where(dcl == j, jnp.where(second, second_v, first_v), acc)
        o_ref[0, t] = acc * LOG2E


def _bias_table(rpb):
    heads = rpb.shape[0]
    return pl.pallas_call(
        _bias_kernel,
        grid=(heads,),
        in_specs=[pl.BlockSpec(memory_space=pltpu.SMEM)],
        out_specs=pl.BlockSpec((1, N_BIAS_ROWS, GRID_W, LANES), lambda h: (h, 0, 0, 0)),
        out_shape=jax.ShapeDtypeStruct((heads, N_BIAS_ROWS, GRID_W, LANES), F32),
        compiler_params=_compiler_params(1),
        name="rel_bias_table",
    )(rpb.reshape(-1))


def _attn_odd_kernel(q1_ref, q2_ref, k1x_ref, v1x_ref, k2x_ref, v2x_ref, k1c_ref, v1c_ref, k2c_ref, v2c_ref,
                     sink_ref, tp_ref, o_ref, *, tq, sink_order):
    i = pl.program_id(1)
    t = k1x_ref.shape[1]
    n_ctx_blk = v1c_ref.shape[1]
    rows = t // GRID_W
    kh = min(NA_ROWS, rows)
    n_blk = tq // LANES

    jobs = []

    n_q = N_SLOTS * C_WINDOW
    n_seq_blk = t // C_WINDOW
    lane_q = lax.broadcasted_iota(jnp.int32, (1, n_q), 1)
    sink = jnp.zeros((1, n_q), F32)
    for j in range(N_SLOTS):
        sink = jnp.where(lane_q >> _log2(C_WINDOW) == j, sink_ref[sink_order[j]], sink)
    sink = sink * LOG2E
    tri = (lax.broadcasted_iota(jnp.int32, (C_WINDOW, n_q), 0)
           - (lax.broadcasted_iota(jnp.int32, (C_WINDOW, n_q), 1) & (C_WINDOW - 1)))

    def c_blocks(jb):
        blk = i * n_blk + jb
        return blk, jnp.maximum(blk - 1, 0), jnp.minimum(blk + 1, n_seq_blk - 1)

    def c_query(jb):
        return jnp.concatenate([q1_ref[0, j, :, jb * LANES:(jb + 1) * LANES] for j in range(N_SLOTS)], axis=1)

    def c_local_scores(jb, qt):
        blk, prev, nxt = c_blocks(jb)
        k_loc = jnp.concatenate([k1x_ref[0, pl.ds(pl.multiple_of(n * C_WINDOW, C_WINDOW), C_WINDOW), :]
                                 for n in (prev, blk, nxt)], axis=0)
        s = _dot(k_loc, qt)
        lo_bound = jnp.where(blk > 0, 0, C_WINDOW)
        hi_bound = jnp.where(blk < n_seq_blk - 1, 0, -C_WINDOW)
        return jnp.concatenate([jnp.where(tri >= lo_bound, s[0:C_WINDOW], NEG),
                                s[C_WINDOW:2 * C_WINDOW],
                                jnp.where(tri <= hi_bound, s[2 * C_WINDOW:3 * C_WINDOW], NEG)], axis=0)

    def c_local_values(jb):
        blk, prev, nxt = c_blocks(jb)
        return jnp.concatenate([v1x_ref[0, n] for n in (prev, blk, nxt)], axis=1)

    def c_finish(jb, o):
        for p in range(N_SLOTS // 2):
            pair_t = jnp.concatenate([o[0:HEAD_DIM, (2 * p) * LANES:(2 * p + 1) * LANES],
                                      o[HEAD_DIM:PAIR, (2 * p + 1) * LANES:(2 * p + 2) * LANES]], axis=0)
            o_ref[0, jb * LANES:(jb + 1) * LANES, p * PAIR:(p + 1) * PAIR] = pair_t.T.astype(BF16)

    for jb in range(n_blk):
        jobs.append(AttnJob(
            functools.partial(c_query, jb),
            [functools.partial(c_local_scores, jb), lambda qt: _dot(k1c_ref[0], qt)],
            [functools.partial(c_local_values, jb),
             lambda: jnp.concatenate([v1c_ref[0, c] for c in range(n_ctx_blk)], axis=1)],
            functools.partial(c_finish, jb), sink))

    n_rows = tq // GRID_W
    span_rows = -(-(n_rows - 1 + kh) // 2) * 2
    n_loc = span_rows * GRID_W
    shape = (n_loc, 2 * tq)
    key_i = lax.broadcasted_iota(jnp.int32, shape, 0)
    lane_i = lax.broadcasted_iota(jnp.int32, shape, 1)
    kk = key_i >> _log2(GRID_W)
    ck = key_i & (GRID_W - 1)
    cq = lane_i & (GRID_W - 1)
    cs = jnp.clip(cq - NA_COLS // 2, 0, GRID_W - NA_COLS)
    r0 = i * n_rows
    u = jnp.minimum(jnp.clip(r0 - kh // 2, 0, rows - kh), rows - span_rows)
    q_row = (lax.broadcasted_iota(jnp.int32, (1, 2 * tq), 1) >> _log2(GRID_W)) & (n_rows - 1)
    first = jnp.zeros((1, 2 * tq), jnp.int32)
    for j in range(n_rows):
        first = jnp.where(q_row == j, jnp.clip(r0 + j - kh // 2, 0, rows - kh) - u, first)
    valid = (ck >= cs) & (ck < cs + NA_COLS) & (kk >= first) & (kk < first + kh)

    chunk_rows = NA_CHUNK_ROWS
    chunk_keys = chunk_rows * GRID_W
    blk_rows = LANES // GRID_W

    def d_query(p):
        return jnp.concatenate([q2_ref[0, 2 * p], q2_ref[0, 2 * p + 1]], axis=1)

    def d_local_scores(p, c, qt):
        k_loc = k2x_ref[0, pl.ds(pl.multiple_of((u + c * chunk_rows) * GRID_W, LANES), chunk_keys),
                        p * PAIR:(p + 1) * PAIR]
        bias = jnp.concatenate(
            [jnp.concatenate(
                [tp_ref[2 * p + lb // (n_rows // 2),
                        jnp.clip(u + a - (r0 + 2 * (lb % (n_rows // 2))) + NA_ROWS - 1, 0, N_BIAS_ROWS - 1)]
                 for lb in range(n_rows)], axis=1)
             for a in range(c * chunk_rows, (c + 1) * chunk_rows)], axis=0)
        return jnp.where(valid[c * chunk_keys:(c + 1) * chunk_keys], _dot(k_loc, qt) + bias, NEG)

    def d_local_values(p, c):
        blk0 = (u + c * chunk_rows) // blk_rows
        return jnp.concatenate([v2x_ref[0, blk0 + b, p * PAIR:(p + 1) * PAIR, :]
                                for b in range(chunk_rows // blk_rows)], axis=1)

    def d_finish(p, o):
        pair_t = jnp.concatenate([o[0:HEAD_DIM, 0:tq], o[HEAD_DIM:PAIR, tq:2 * tq]], axis=0)
        o_ref[0, :, Q_WIDTH + p * PAIR:Q_WIDTH + (p + 1) * PAIR] = pair_t.T.astype(BF16)

    for p in range(N_SLOTS // 2):
        cols = slice(p * PAIR, (p + 1) * PAIR)
        n_chunks = span_rows // chunk_rows
        jobs.append(AttnJob(
            functools.partial(d_query, p),
            [functools.partial(d_local_scores, p, c) for c in range(n_chunks)]
            + [lambda qt, cols=cols: _dot(k2c_ref[0, :, cols], qt)],
            [functools.partial(d_local_values, p, c) for c in range(n_chunks)]
            + [lambda cols=cols: jnp.concatenate([v2c_ref[0, b, cols, :] for b in range(n_ctx_blk)], axis=1)],
            functools.partial(d_finish, p)))
    _attend_jobs(jobs)


def _attn_odd(qx, x_kv, ctx_kv, sink, tp, *, tq, sink_order):
    q1, q2 = qx
    b, _, _, s = q1.shape
    slot_spec = pl.BlockSpec((1, N_SLOTS, PAIR, tq), lambda i, j: (i, 0, 0, j))

    def full(a):
        return pl.BlockSpec((1,) + a.shape[1:], lambda i, j: (i,) + (0,) * (a.ndim - 1))

    kv = list(x_kv) + list(ctx_kv)
    return pl.pallas_call(
        functools.partial(_attn_odd_kernel, tq=tq, sink_order=sink_order),
        grid=(b, s // tq),
        in_specs=[slot_spec, slot_spec] + [full(a) for a in kv]
                 + [pl.BlockSpec(memory_space=pltpu.SMEM), pl.BlockSpec(tp.shape, lambda i, j: (0, 0, 0, 0))],
        out_specs=pl.BlockSpec((1, tq, 2 * Q_WIDTH), lambda i, j: (i, j, 0)),
        out_shape=jax.ShapeDtypeStruct((b, s, 2 * Q_WIDTH), BF16),
        compiler_params=_compiler_params(2),
        name="attn_odd",
    )(q1, q2, *kv, sink, tp)


def _post_kernel(m_ref, x_ref, mod_ref, g_ref, wo_ref, w1_ref, w2_ref, o_ref, *, ff_chunk):
    d = x_ref.shape[-1]

    def mod(k):
        return mod_ref[0, :, k * d:(k + 1) * d]

    x1 = x_ref[0] + mod(2) * _rms(_dot(m_ref[0], wo_ref[...]), g_ref[1:2])
    h = (_rms(x1, g_ref[2:3]) * (1.0 + mod(4)) + mod(3)).astype(BF16)
    f = None
    for c in range(w1_ref.shape[1] // ff_chunk):
        cols = slice(c * ff_chunk, (c + 1) * ff_chunk)
        a = jnp.maximum(_dot(h, w1_ref[:, cols]), 0.0)
        part = _dot((a * a).astype(BF16), w2_ref[cols, :])
        f = part if f is None else f + part
    o_ref[0] = x1 + mod(5) * _rms(f, g_ref[3:4])


def _post(m, x, mod, gains, wo, w1, w2, *, tm, ff_chunk):
    b, s, d = x.shape
    bm = mod.shape[0]
    mod_map = (lambda i, j: (i, 0, 0)) if bm > 1 else (lambda i, j: (0, 0, 0))

    def const(a):
        return pl.BlockSpec(a.shape, lambda i, j: (0, 0), pipeline_mode=pl.Buffered(1))

    return pl.pallas_call(
        functools.partial(_post_kernel, ff_chunk=ff_chunk),
        grid=(b, s // tm),
        in_specs=[pl.BlockSpec((1, tm, m.shape[-1]), lambda i, j: (i, j, 0)),
                  pl.BlockSpec((1, tm, d), lambda i, j: (i, j, 0)),
                  pl.BlockSpec((1, 1, mod.shape[-1]), mod_map),
                  pl.BlockSpec(gains.shape, lambda i, j: (0, 0)),
                  const(wo), const(w1), const(w2)],
        out_specs=pl.BlockSpec((1, tm, d), lambda i, j: (i, j, 0)),
        out_shape=jax.ShapeDtypeStruct((b, s, d), F32),
        compiler_params=_compiler_params(2),
        name="post",
    )(m, x, mod, gains, wo, w1, w2)


def _rope_tables(t):
    tok = jnp.arange(t, dtype=jnp.int32)
    pos = jnp.stack([tok // GRID_W, tok % GRID_W], axis=-1).astype(F32)
    inv = ROPE_THETA ** (-jnp.arange(ROPE_FREQS, dtype=F32) / ROPE_FREQS)
    ang = pos[..., None] * inv
    cos, sin = jnp.cos(ang), jnp.sin(ang)
    rope_t = jnp.stack([cos.reshape(t, -1).T, sin.reshape(t, -1).T])
    zero = jnp.zeros_like(sin[:, 0])
    c = jnp.concatenate([cos[:, 0], cos[:, 0], cos[:, 1], cos[:, 1]], axis=-1)
    up = jnp.concatenate([-sin[:, 0], zero, -sin[:, 1], zero], axis=-1)
    down = jnp.concatenate([zero, sin[:, 0], zero, sin[:, 1]], axis=-1)
    rope_n = jnp.stack([jnp.tile(a, (1, PAIR // HEAD_DIM)) for a in (c, up, down)])
    return rope_t, rope_n


def _pair_gqa_heads(w, axis):
    shape = w.shape
    w = w.reshape(shape[:axis] + (2, N_SLOTS // 2, HEAD_DIM) + shape[axis + 1:])
    return jnp.swapaxes(w, axis, axis + 1).reshape(shape)


def _lambda_init(layer):
    return 0.8 - 0.6 * math.exp(-0.3 * layer)


def kernel(x, c, ctx, c_ctx, w_mod, b_mod, norm_g, w_in, w_out, w_mlp_in, w_mlp_out, qk_norm_a, diff_lambda,
           diff_subln, sink_c, rpb_d):
    b, t, d = x.shape
    n_ctx = ctx.shape[1]
    depth = w_mod.shape[0]
    mod_rows = 16
    cc = jnp.concatenate([c, c_ctx[None], jnp.zeros((mod_rows - b - 1, d), F32)], axis=0)
    mod_all = _modulation(cc, w_mod, b_mod)

    rope_args = _rope_tables(t)
    seg = jnp.kron(jnp.eye(2, dtype=F32), jnp.ones((HEAD_DIM, HEAD_DIM), F32)).astype(BF16)
    tm_x = 512

    for l in range(depth):
        last = l == depth - 1
        even = l % 2 == 0
        i = l // 2
        mod_x = mod_all[l, :b][:, None, :]
        mod_c = mod_all[l, b][None, None, :]
        w = w_in[l]
        o_k1, o_v1, o_q2, o_k2, o_v2 = Q_WIDTH, Q_WIDTH + PAIR, Q_WIDTH + 2 * PAIR, 2 * Q_WIDTH + 2 * PAIR, \
            3 * Q_WIDTH + 2 * PAIR
        wt = jnp.concatenate([_pair_gqa_heads(w[:, 0:o_k1], 1), w[:, o_v1:o_q2], w[:, o_q2:o_k2], w[:, o_v2:]],
                             axis=1).T.astype(BF16)
        wk = jnp.concatenate([w[:, o_k1:o_v1], w[:, o_k2:o_v2]], axis=1).astype(BF16)
        wo = jnp.concatenate([_pair_gqa_heads(w_out[l, 0:Q_WIDTH], 0), w_out[l, Q_WIDTH:]], axis=0).astype(BF16)
        w1 = w_mlp_in[l].astype(BF16)
        w2 = w_mlp_out[l].astype(BF16)
        g0 = norm_g[l, 0][None, :]

        def norm_args(tm):
            if not even:
                return None
            return (jnp.broadcast_to(qk_norm_a[i, 0][:, None], (HEAD_DIM, tm)),
                    jnp.tile(qk_norm_a[i, 1], PAIR // HEAD_DIM)[None, :], seg)

        q1x, k1x, v1x, q2x, k2x, v2x = _in_proj(x, mod_x, g0, wt, wk, norm_args(tm_x), rope_args,
                                                even=even, tm=tm_x)
        q1c, k1c, v1c, q2c, k2c, v2c = _in_proj(ctx, mod_c, g0, wt, wk, norm_args(n_ctx), None,
                                                even=even, tm=n_ctx)
        x_kv = (k1x, v1x, k2x, v2x)
        ctx_kv = (k1c, v1c, k2c, v2c)
        if even:
            lam0 = _lambda_init(l)
            sub = diff_subln[i][None, :]
            m_x = _attn_even((q1x, q2x), ctx_kv, x_kv, diff_lambda[i], sub, lam0=lam0, tq=256)
            if not last:
                m_c = _attn_even((q1c, q2c), ctx_kv, None, diff_lambda[i], sub, lam0=lam0, tq=256)
        else:
            if not last:
                raise NotImplementedError("context update after an odd layer is not needed for depth 2")
            tp = _bias_table(rpb_d[i])
            m_x = _attn_odd((q1x, q2x), x_kv, ctx_kv, sink_c[i], tp, tq=256, sink_order=GQA_SLOT_HEADS)
        x = _post(m_x, x, mod_x, norm_g[l], wo, w1, w2, tm=512, ff_chunk=1024)
        if not last:
            ctx = _post(m_c, ctx, mod_c, norm_g[l], wo, w1, w2, tm=n_ctx, ff_chunk=1024)
    return x
```

```python
import functools
import math
from typing import Any, Callable, NamedTuple, Sequence

import jax
import jax.numpy as jnp
from jax import lax
from jax.experimental import pallas as pl
from jax.experimental.pallas import tpu as pltpu

F32 = jnp.float32
BF16 = jnp.bfloat16

GRID_W = 64
HEAD_DIM = 64
ROPE_THETA = 10000.0
ROPE_FREQS = HEAD_DIM // 4
N_SLOTS = 8
PAIR = 2 * HEAD_DIM
Q_WIDTH = N_SLOTS * HEAD_DIM
LANES = 128
MXU_QUERY_COLS = 512
EVEN_KEY_CHUNK = 1024
NA_CHUNK_ROWS = 4
C_WINDOW = 128
NA_ROWS = 8
NA_COLS = 16
EPS = 1e-6
NEG = -1e30
LOG2E = math.log2(math.e)
Q_SCALE = HEAD_DIM ** -0.5 * LOG2E
GQA_SLOT_HEADS = tuple(half * (N_SLOTS // 2) + p for p in range(N_SLOTS // 2) for half in range(2))

VMEM_LIMIT_BYTES = 56 * 1024 * 1024
ATTN_SCHEDULER_FLAGS = None


def _compiler_params(n_grid_dims, flags=None):
    return pltpu.CompilerParams(dimension_semantics=("arbitrary",) * n_grid_dims,
                                vmem_limit_bytes=VMEM_LIMIT_BYTES, flags=flags)


def _dot(a, b):
    return jnp.dot(a, b, preferred_element_type=F32)


def _dot_nt(a, b):
    return lax.dot_general(a, b, (((1,), (1,)), ((), ())), preferred_element_type=F32)


def _log2(n):
    assert n & (n - 1) == 0
    return n.bit_length() - 1


def _rms(x, gain):
    ms = jnp.mean(x * x, axis=-1, keepdims=True)
    return x * lax.rsqrt(ms + EPS) * gain


def _mod_kernel(c_ref, w_ref, b_ref, o_ref):
    c = c_ref[...]
    s = c * jax.nn.sigmoid(c)
    o_ref[0] = _dot(s.astype(BF16), w_ref[0].astype(BF16)) + b_ref[0]


def _modulation(cc, w_mod, b_mod):
    depth, d, n = w_mod.shape
    rows = cc.shape[0]
    tn = n // 4
    return pl.pallas_call(
        _mod_kernel,
        grid=(depth, n // tn),
        in_specs=[pl.BlockSpec((rows, d), lambda l, j: (0, 0)),
                  pl.BlockSpec((1, d, tn), lambda l, j: (l, 0, j)),
                  pl.BlockSpec((1, 1, tn), lambda l, j: (l, 0, j))],
        out_specs=pl.BlockSpec((1, rows, tn), lambda l, j: (l, 0, j)),
        out_shape=jax.ShapeDtypeStruct((depth, rows, n), F32),
        compiler_params=_compiler_params(2),
        name="modulation",
    )(cc, w_mod, b_mod.reshape(depth, 1, n))


def _in_kernel(*refs, even, rope):
    refs = list(refs)
    tok_ref, mod_ref, g_ref, wt_ref, wk_ref = refs[:5]
    del refs[:5]
    if even:
        gq_ref, gk_ref, seg_ref = refs[:3]
        del refs[:3]
    if rope:
        rope_t_ref, rope_ref = refs[:2]
        del refs[:2]
    q1_ref, k1_ref, v1_ref, q2_ref, k2_ref, v2_ref = refs
    d = tok_ref.shape[-1]
    tm = tok_ref.shape[1]

    h = _rms(tok_ref[0], g_ref[...])
    h = h * (1.0 + mod_ref[0, :, d:2 * d]) + mod_ref[0, :, 0:d]
    hb = h.astype(BF16)

    def rotary_t(c):
        f = ROPE_FREQS
        out = []
        for a in range(2):
            x1, x2 = c[2 * a * f:(2 * a + 1) * f], c[(2 * a + 1) * f:(2 * a + 2) * f]
            cos, sin = rope_t_ref[0, a * f:(a + 1) * f], rope_t_ref[1, a * f:(a + 1) * f]
            out += [x1 * cos - x2 * sin, x2 * cos + x1 * sin]
        return jnp.concatenate(out, axis=0)

    def store_slots(q_ref, qt, use_norm, use_rope):
        zeros = jnp.zeros((HEAD_DIM, tm), F32)
        for j in range(N_SLOTS):
            c = qt[j * HEAD_DIM:(j + 1) * HEAD_DIM]
            if use_norm:
                ms = jnp.mean(c * c, axis=0, keepdims=True)
                c = c * lax.rsqrt(ms + EPS) * gq_ref[...]
            if use_rope:
                c = rotary_t(c)
            c = c * Q_SCALE
            slot = jnp.concatenate([c, zeros] if j % 2 == 0 else [zeros, c], axis=0)
            q_ref[0, j] = slot.astype(BF16)

    def store_vt(v_ref, vt):
        if even:
            v_ref[0] = vt.astype(BF16)
        else:
            for c in range(tm // LANES):
                v_ref[0, c] = vt[:, c * LANES:(c + 1) * LANES].astype(BF16)

    def rotary(c):
        up = pltpu.roll(c, PAIR - ROPE_FREQS, axis=1)
        down = pltpu.roll(c, ROPE_FREQS, axis=1)
        return c * rope_ref[0] + up * rope_ref[1] + down * rope_ref[2]

    o = 0
    store_slots(q1_ref, _dot_nt(wt_ref[o:o + Q_WIDTH], hb), even, rope)
    o += Q_WIDTH
    store_vt(v1_ref, _dot_nt(wt_ref[o:o + PAIR], hb))
    o += PAIR
    store_slots(q2_ref, _dot_nt(wt_ref[o:o + Q_WIDTH], hb), False, rope and even)
    o += Q_WIDTH
    store_vt(v2_ref, _dot_nt(wt_ref[o:o + Q_WIDTH], hb))

    k = _dot(hb, wk_ref[...])
    k1 = k[:, 0:PAIR]
    if even:
        sq = k1 * k1
        hi = sq.astype(BF16)
        lo = (sq - hi.astype(F32)).astype(BF16)
        ms = (_dot(hi, seg_ref[...]) + _dot(lo, seg_ref[...])) * (1.0 / HEAD_DIM)
        k1 = k1 * lax.rsqrt(ms + EPS) * gk_ref[...]
    if rope:
        k1 = rotary(k1)
    k1_ref[0] = k1.astype(BF16)
    for j in range(Q_WIDTH // PAIR):
        c = k[:, PAIR + j * PAIR:PAIR + (j + 1) * PAIR]
        if rope and even:
            c = rotary(c)
        k2_ref[0, :, j * PAIR:(j + 1) * PAIR] = c.astype(BF16)


def _in_proj(tok, mod, gain, wt, wk, norm_args, rope_args, *, even, tm):
    b, s, d = tok.shape
    bm = mod.shape[0]
    rope = rope_args is not None
    mod_map = (lambda i, j: (i, 0, 0)) if bm > 1 else (lambda i, j: (0, 0, 0))

    def const(a):
        return pl.BlockSpec(a.shape, lambda i, j: (0,) * a.ndim)

    in_specs = [pl.BlockSpec((1, tm, d), lambda i, j: (i, j, 0)),
                pl.BlockSpec((1, 1, mod.shape[-1]), mod_map),
                const(gain), const(wt), const(wk)]
    args = [tok, mod, gain, wt, wk]
    if even:
        gq, gk, seg = norm_args
        in_specs += [const(gq), const(gk), const(seg)]
        args += [gq, gk, seg]
    if rope:
        rope_t, rope_n = rope_args
        in_specs += [pl.BlockSpec((2, 2 * ROPE_FREQS, tm), lambda i, j: (0, 0, j)),
                     pl.BlockSpec((3, tm, PAIR), lambda i, j: (0, j, 0))]
        args += [rope_t, rope_n]

    def vt_out(rows):
        if even:
            return (pl.BlockSpec((1, rows, tm), lambda i, j: (i, 0, j)),
                    jax.ShapeDtypeStruct((b, rows, s), BF16))
        return (pl.BlockSpec((1, tm // LANES, rows, LANES), lambda i, j: (i, j, 0, 0)),
                jax.ShapeDtypeStruct((b, s // LANES, rows, LANES), BF16))

    slot_out = (pl.BlockSpec((1, N_SLOTS, PAIR, tm), lambda i, j: (i, 0, 0, j)),
                jax.ShapeDtypeStruct((b, N_SLOTS, PAIR, s), BF16))

    def nat_out(cols):
        return (pl.BlockSpec((1, tm, cols), lambda i, j: (i, j, 0)), jax.ShapeDtypeStruct((b, s, cols), BF16))

    outs = [slot_out, nat_out(PAIR), vt_out(PAIR), slot_out, nat_out(Q_WIDTH), vt_out(Q_WIDTH)]
    return pl.pallas_call(
        functools.partial(_in_kernel, even=even, rope=rope),
        grid=(b, s // tm),
        in_specs=in_specs,
        out_specs=[o[0] for o in outs],
        out_shape=[o[1] for o in outs],
        compiler_params=_compiler_params(2),
        name="in_proj_" + ("even" if even else "odd") + ("_x" if rope else "_ctx"),
    )(*args)


class AttnJob(NamedTuple):
    query: Callable[[], Any]
    scores: Sequence[Callable[[Any], Any]]
    values: Sequence[Callable[[], Any]]
    finish: Callable[[Any], None]
    sink: Any = None


def _attend_lead(jobs):
    return max(len(job.scores) for job in jobs) + 1


def _attend_jobs(jobs, s_ref):
    items = [(j, c) for j, job in enumerate(jobs) for c in range(len(job.scores))]
    lead = _attend_lead(jobs)
    n_slots = s_ref.shape[0]
    assert n_slots > lead
    qt, shape, m, l, o = {}, {}, {}, {}, {}
    for t in range(len(items) + lead):
        if t < len(items):
            j, c = items[t]
            job = jobs[j]
            if c == 0:
                qt[j] = job.query()
            x = job.scores[c](qt[j])
            shape[t] = x.shape
            s_ref[t % n_slots, 0:x.shape[0], 0:x.shape[1]] = x
            part_m = jnp.max(x, axis=0, keepdims=True)
            if c == 0:
                m[j] = part_m if job.sink is None else jnp.maximum(part_m, job.sink)
            else:
                m[j] = jnp.maximum(m[j], part_m)
        if t >= lead:
            j, c = items[t - lead]
            job = jobs[j]
            rows, cols = shape.pop(t - lead)
            e = jnp.exp2(s_ref[(t - lead) % n_slots, 0:rows, 0:cols] - m[j])
            part_l = jnp.sum(e, axis=0, keepdims=True)
            part_o = _dot(job.values[c](), e.astype(BF16))
            l[j] = part_l if c == 0 else l[j] + part_l
            o[j] = part_o if c == 0 else o[j] + part_o
            if c == len(job.scores) - 1:
                denom = l.pop(j) if job.sink is None else l.pop(j) + jnp.exp2(job.sink - m[j])
                job.finish(o.pop(j) * (1.0 / denom))


def _attn_even_kernel(*refs, with_x, lam0, tq, key_chunk):
    if with_x:
        q1_ref, q2_ref, k1c_ref, v1c_ref, k2c_ref, v2c_ref, k1x_ref, v1x_ref, k2x_ref, v2x_ref = refs[:10]
    else:
        q1_ref, q2_ref, k1c_ref, v1c_ref, k2c_ref, v2c_ref = refs[:6]
    lam_ref, sub_ref, o_ref, s_ref = refs[-4:]

    def chunk_fns(c_ref, x_ref, cols, transposed):
        def make(ref, lo, hi):
            if transposed:
                return lambda: ref[0, cols, lo:hi]
            return lambda qt: _dot(ref[0, lo:hi, cols], qt)
        n_ctx = c_ref.shape[2] if transposed else c_ref.shape[1]
        fns = [make(c_ref, 0, n_ctx)]
        if with_x:
            n_x = x_ref.shape[2] if transposed else x_ref.shape[1]
            fns += [make(x_ref, lo, lo + key_chunk) for lo in range(0, n_x, key_chunk)]
        return fns

    group = max(2, MXU_QUERY_COLS // tq)
    jobs = []
    pair_cols = slice(0, PAIR)

    def a_query(g):
        return jnp.concatenate([q1_ref[0, g * group + j] for j in range(group)], axis=1)

    def a_finish(g, o):
        for pp in range(group // 2):
            pair_t = jnp.concatenate([o[0:HEAD_DIM, (2 * pp) * tq:(2 * pp + 1) * tq],
                                      o[HEAD_DIM:PAIR, (2 * pp + 1) * tq:(2 * pp + 2) * tq]], axis=0)
            col = (g * (group // 2) + pp) * PAIR
            o_ref[0, :, col:col + PAIR] = pair_t.T.astype(BF16)

    for g in range(N_SLOTS // group):
        jobs.append(AttnJob(functools.partial(a_query, g),
                            chunk_fns(k1c_ref, k1x_ref if with_x else None, pair_cols, False),
                            chunk_fns(v1c_ref, v1x_ref if with_x else None, pair_cols, True),
                            functools.partial(a_finish, g)))

    def b_query(h):
        return jnp.concatenate([q2_ref[0, 2 * h], q2_ref[0, 2 * h + 1]], axis=1)

    def b_finish(h, o):
        lp = lam_ref[...]
        lam = (jnp.exp(jnp.sum(lp[0:1] * lp[1:2], axis=-1, keepdims=True))
               - jnp.exp(jnp.sum(lp[2:3] * lp[3:4], axis=-1, keepdims=True)) + lam0)
        ob = (o[:, 0:tq] - lam * o[:, tq:2 * tq]).T
        ob = _rms(ob, sub_ref[...]) * (1.0 - lam0)
        col = Q_WIDTH + h * PAIR
        o_ref[0, :, col:col + PAIR] = ob.astype(BF16)

    for h in range(N_SLOTS // 2):
        cols = slice(h * PAIR, (h + 1) * PAIR)
        jobs.append(AttnJob(functools.partial(b_query, h),
                            chunk_fns(k2c_ref, k2x_ref if with_x else None, cols, False),
                            chunk_fns(v2c_ref, v2x_ref if with_x else None, cols, True),
                            functools.partial(b_finish, h)))
    _attend_jobs(jobs, s_ref)


def _attn_even(qx, ctx_kv, x_kv, lam_p, subln, *, lam0, tq):
    q1, q2 = qx
    b, _, _, s = q1.shape
    with_x = x_kv is not None
    slot_spec = pl.BlockSpec((1, N_SLOTS, PAIR, tq), lambda i, j: (i, 0, 0, j))

    def full(a):
        return pl.BlockSpec((1,) + a.shape[1:], lambda i, j: (i, 0, 0))

    kv = list(ctx_kv) + (list(x_kv) if with_x else [])
    n_ctx = ctx_kv[0].shape[1]
    key_chunk = min(EVEN_KEY_CHUNK, x_kv[0].shape[1]) if with_x else n_ctx
    chunks_per_job = 1 + (x_kv[0].shape[1] // key_chunk if with_x else 0)
    score_ring = pltpu.VMEM((chunks_per_job + 2, max(n_ctx, key_chunk), max(2 * tq, MXU_QUERY_COLS)), F32)
    return pl.pallas_call(
        functools.partial(_attn_even_kernel, with_x=with_x, lam0=lam0, tq=tq, key_chunk=key_chunk),
        grid=(b, s // tq),
        scratch_shapes=[score_ring],
        in_specs=[slot_spec, slot_spec] + [full(a) for a in kv]
                 + [pl.BlockSpec(lam_p.shape, lambda i, j: (0, 0)), pl.BlockSpec(subln.shape, lambda i, j: (0, 0))],
        out_specs=pl.BlockSpec((1, tq, 2 * Q_WIDTH), lambda i, j: (i, j, 0)),
        out_shape=jax.ShapeDtypeStruct((b, s, 2 * Q_WIDTH), BF16),
        compiler_params=_compiler_params(2, ATTN_SCHEDULER_FLAGS),
        name="attn_even" + ("_x" if with_x else "_ctx"),
    )(q1, q2, *kv, lam_p, subln)


N_BIAS_ROWS = 2 * NA_ROWS


def _bias_kernel(rpb_ref, o_ref):
    head = pl.program_id(0)
    n_dr = 2 * NA_ROWS - 1
    n_dc = 2 * NA_COLS - 1
    shape = (GRID_W, LANES)
    ck = lax.broadcasted_iota(jnp.int32, shape, 0)
    lane = lax.broadcasted_iota(jnp.int32, shape, 1)
    second = lane >= GRID_W
    dcl = jnp.clip(ck - (lane & (GRID_W - 1)), -(NA_COLS - 1), NA_COLS - 1) + (NA_COLS - 1)
    for t in range(N_BIAS_ROWS):
        acc = jnp.zeros(shape, F32)
        for j in range(n_dc):
            first_v = rpb_ref[(head * n_dr + min(t, n_dr - 1)) * n_dc + j]
            second_v = rpb_ref[(head * n_dr + max(t - 1, 0)) * n_dc + j]
            acc = jnp.where(dcl == j, jnp.where(second, second_v, first_v), acc)
        o_ref[0, t] = acc * LOG2E


def _bias_table(rpb):
    heads = rpb.shape[0]
    return pl.pallas_call(
        _bias_kernel,
        grid=(heads,),
        in_specs=[pl.BlockSpec(memory_space=pltpu.SMEM)],
        out_specs=pl.BlockSpec((1, N_BIAS_ROWS, GRID_W, LANES), lambda h: (h, 0, 0, 0)),
        out_shape=jax.ShapeDtypeStruct((heads, N_BIAS_ROWS, GRID_W, LANES), F32),
        compiler_params=_compiler_params(1),
        name="rel_bias_table",
    )(rpb.reshape(-1))


def _attn_odd_kernel(q1_ref, q2_ref, k1x_ref, v1x_ref, k2x_ref, v2x_ref, k1c_ref, v1c_ref, k2c_ref, v2c_ref,
                     sink_ref, tp_ref, o_ref, s_ref, *, tq, sink_order):
    i = pl.program_id(1)
    t = k1x_ref.shape[1]
    n_ctx_blk = v1c_ref.shape[1]
    rows = t // GRID_W
    kh = min(NA_ROWS, rows)
    n_blk = tq // LANES

    jobs = []

    n_q = N_SLOTS * C_WINDOW
    n_seq_blk = t // C_WINDOW
    lane_q = lax.broadcasted_iota(jnp.int32, (1, n_q), 1)
    sink = jnp.zeros((1, n_q), F32)
    for j in range(N_SLOTS):
        sink = jnp.where(lane_q >> _log2(C_WINDOW) == j, sink_ref[sink_order[j]], sink)
    sink = sink * LOG2E
    tri = (lax.broadcasted_iota(jnp.int32, (C_WINDOW, n_q), 0)
           - (lax.broadcasted_iota(jnp.int32, (C_WINDOW, n_q), 1) & (C_WINDOW - 1)))

    def c_blocks(jb):
        blk = i * n_blk + jb
        return blk, jnp.maximum(blk - 1, 0), jnp.minimum(blk + 1, n_seq_blk - 1)

    def c_query(jb):
        return jnp.concatenate([q1_ref[0, j, :, jb * LANES:(jb + 1) * LANES] for j in range(N_SLOTS)], axis=1)

    def c_local_scores(jb, qt):
        blk, prev, nxt = c_blocks(jb)
        k_loc = jnp.concatenate([k1x_ref[0, pl.ds(pl.multiple_of(n * C_WINDOW, C_WINDOW), C_WINDOW), :]
                                 for n in (prev, blk, nxt)], axis=0)
        s = _dot(k_loc, qt)
        lo_bound = jnp.where(blk > 0, 0, C_WINDOW)
        hi_bound = jnp.where(blk < n_seq_blk - 1, 0, -C_WINDOW)
        return jnp.concatenate([jnp.where(tri >= lo_bound, s[0:C_WINDOW], NEG),
                                s[C_WINDOW:2 * C_WINDOW],
                                jnp.where(tri <= hi_bound, s[2 * C_WINDOW:3 * C_WINDOW], NEG)], axis=0)

    def c_local_values(jb):
        blk, prev, nxt = c_blocks(jb)
        return jnp.concatenate([v1x_ref[0, n] for n in (prev, blk, nxt)], axis=1)

    def c_finish(jb, o):
        for p in range(N_SLOTS // 2):
            pair_t = jnp.concatenate([o[0:HEAD_DIM, (2 * p) * LANES:(2 * p + 1) * LANES],
                                      o[HEAD_DIM:PAIR, (2 * p + 1) * LANES:(2 * p + 2) * LANES]], axis=0)
            o_ref[0, jb * LANES:(jb + 1) * LANES, p * PAIR:(p + 1) * PAIR] = pair_t.T.astype(BF16)

    for jb in range(n_blk):
        jobs.append(AttnJob(
            functools.partial(c_query, jb),
            [functools.partial(c_local_scores, jb), lambda qt: _dot(k1c_ref[0], qt)],
            [functools.partial(c_local_values, jb),
             lambda: jnp.concatenate([v1c_ref[0, c] for c in range(n_ctx_blk)], axis=1)],
            functools.partial(c_finish, jb), sink))

    n_rows = tq // GRID_W
    span_rows = -(-(n_rows - 1 + kh) // 2) * 2
    n_loc = span_rows * GRID_W
    shape = (n_loc, 2 * tq)
    key_i = lax.broadcasted_iota(jnp.int32, shape, 0)
    lane_i = lax.broadcasted_iota(jnp.int32, shape, 1)
    kk = key_i >> _log2(GRID_W)
    ck = key_i & (GRID_W - 1)
    cq = lane_i & (GRID_W - 1)
    cs = jnp.clip(cq - NA_COLS // 2, 0, GRID_W - NA_COLS)
    r0 = i * n_rows
    u = jnp.minimum(jnp.clip(r0 - kh // 2, 0, rows - kh), rows - span_rows)
    q_row = (lax.broadcasted_iota(jnp.int32, (1, 2 * tq), 1) >> _log2(GRID_W)) & (n_rows - 1)
    first = jnp.zeros((1, 2 * tq), jnp.int32)
    for j in range(n_rows):
        first = jnp.where(q_row == j, jnp.clip(r0 + j - kh // 2, 0, rows - kh) - u, first)
    valid = (ck >= cs) & (ck < cs + NA_COLS) & (kk >= first) & (kk < first + kh)

    chunk_rows = NA_CHUNK_ROWS
    chunk_keys = chunk_rows * GRID_W
    blk_rows = LANES // GRID_W

    def d_query(p):
        return jnp.concatenate([q2_ref[0, 2 * p], q2_ref[0, 2 * p + 1]], axis=1)

    def d_local_scores(p, c, qt):
        k_loc = k2x_ref[0, pl.ds(pl.multiple_of((u + c * chunk_rows) * GRID_W, LANES), chunk_keys),
                        p * PAIR:(p + 1) * PAIR]
        bias = jnp.concatenate(
            [jnp.concatenate(
                [tp_ref[2 * p + lb // (n_rows // 2),
                        jnp.clip(u + a - (r0 + 2 * (lb % (n_rows // 2))) + NA_ROWS - 1, 0, N_BIAS_ROWS - 1)]
                 for lb in range(n_rows)], axis=1)
             for a in range(c * chunk_rows, (c + 1) * chunk_rows)], axis=0)
        return jnp.where(valid[c * chunk_keys:(c + 1) * chunk_keys], _dot(k_loc, qt) + bias, NEG)

    def d_local_values(p, c):
        blk0 = (u + c * chunk_rows) // blk_rows
        return jnp.concatenate([v2x_ref[0, blk0 + b, p * PAIR:(p + 1) * PAIR, :]
                                for b in range(chunk_rows // blk_rows)], axis=1)

    def d_finish(p, o):
        pair_t = jnp.concatenate([o[0:HEAD_DIM, 0:tq], o[HEAD_DIM:PAIR, tq:2 * tq]], axis=0)
        o_ref[0, :, Q_WIDTH + p * PAIR:Q_WIDTH + (p + 1) * PAIR] = pair_t.T.astype(BF16)

    for p in range(N_SLOTS // 2):
        cols = slice(p * PAIR, (p + 1) * PAIR)
        n_chunks = span_rows // chunk_rows
        jobs.append(AttnJob(
            functools.partial(d_query, p),
            [functools.partial(d_local_scores, p, c) for c in range(n_chunks)]
            + [lambda qt, cols=cols: _dot(k2c_ref[0, :, cols], qt)],
            [functools.partial(d_local_values, p, c) for c in range(n_chunks)]
            + [lambda cols=cols: jnp.concatenate([v2c_ref[0, b, cols, :] for b in range(n_ctx_blk)], axis=1)],
            functools.partial(d_finish, p)))
    _attend_jobs(jobs, s_ref)


def _attn_odd(qx, x_kv, ctx_kv, sink, tp, *, tq, sink_order):
    q1, q2 = qx
    b, _, _, s = q1.shape
    slot_spec = pl.BlockSpec((1, N_SLOTS, PAIR, tq), lambda i, j: (i, 0, 0, j))

    def full(a):
        return pl.BlockSpec((1,) + a.shape[1:], lambda i, j: (i,) + (0,) * (a.ndim - 1))

    kv = list(x_kv) + list(ctx_kv)
    n_ctx = ctx_kv[0].shape[1]
    rows = s // GRID_W
    span_rows = -(-(tq // GRID_W - 1 + min(NA_ROWS, rows)) // 2) * 2
    chunks_per_job = max(2, span_rows // NA_CHUNK_ROWS + 1)
    score_ring = pltpu.VMEM((chunks_per_job + 2, max(3 * C_WINDOW, NA_CHUNK_ROWS * GRID_W, n_ctx),
                             max(N_SLOTS * C_WINDOW, 2 * tq)), F32)
    return pl.pallas_call(
        functools.partial(_attn_odd_kernel, tq=tq, sink_order=sink_order),
        grid=(b, s // tq),
        scratch_shapes=[score_ring],
        in_specs=[slot_spec, slot_spec] + [full(a) for a in kv]
                 + [pl.BlockSpec(memory_space=pltpu.SMEM), pl.BlockSpec(tp.shape, lambda i, j: (0, 0, 0, 0))],
        out_specs=pl.BlockSpec((1, tq, 2 * Q_WIDTH), lambda i, j: (i, j, 0)),
        out_shape=jax.ShapeDtypeStruct((b, s, 2 * Q_WIDTH), BF16),
        compiler_params=_compiler_params(2),
        name="attn_odd",
    )(q1, q2, *kv, sink, tp)


def _post_kernel(m_ref, x_ref, mod_ref, g_ref, wo_ref, w1_ref, w2_ref, o_ref, *, ff_chunk):
    d = x_ref.shape[-1]

    def mod(k):
        return mod_ref[0, :, k * d:(k + 1) * d]

    x1 = x_ref[0] + mod(2) * _rms(_dot(m_ref[0], wo_ref[...]), g_ref[1:2])
    h = (_rms(x1, g_ref[2:3]) * (1.0 + mod(4)) + mod(3)).astype(BF16)
    f = None
    for c in range(w1_ref.shape[1] // ff_chunk):
        cols = slice(c * ff_chunk, (c + 1) * ff_chunk)
        a = jnp.maximum(_dot(h, w1_ref[:, cols]), 0.0)
        part = _dot((a * a).astype(BF16), w2_ref[cols, :])
        f = part if f is None else f + part
    o_ref[0] = x1 + mod(5) * _rms(f, g_ref[3:4])


def _post(m, x, mod, gains, wo, w1, w2, *, tm, ff_chunk):
    b, s, d = x.shape
    bm = mod.shape[0]
    mod_map = (lambda i, j: (i, 0, 0)) if bm > 1 else (lambda i, j: (0, 0, 0))

    def const(a):
        return pl.BlockSpec(a.shape, lambda i, j: (0, 0), pipeline_mode=pl.Buffered(1))

    return pl.pallas_call(
        functools.partial(_post_kernel, ff_chunk=ff_chunk),
        grid=(b, s // tm),
        in_specs=[pl.BlockSpec((1, tm, m.shape[-1]), lambda i, j: (i, j, 0)),
                  pl.BlockSpec((1, tm, d), lambda i, j: (i, j, 0)),
                  pl.BlockSpec((1, 1, mod.shape[-1]), mod_map),
                  pl.BlockSpec(gains.shape, lambda i, j: (0, 0)),
                  const(wo), const(w1), const(w2)],
        out_specs=pl.BlockSpec((1, tm, d), lambda i, j: (i, j, 0)),
        out_shape=jax.ShapeDtypeStruct((b, s, d), F32),
        compiler_params=_compiler_params(2),
        name="post",
    )(m, x, mod, gains, wo, w1, w2)


def _rope_tables(t):
    tok = jnp.arange(t, dtype=jnp.int32)
    pos = jnp.stack([tok // GRID_W, tok % GRID_W], axis=-1).astype(F32)
    inv = ROPE_THETA ** (-jnp.arange(ROPE_FREQS, dtype=F32) / ROPE_FREQS)
    ang = pos[..., None] * inv
    cos, sin = jnp.cos(ang), jnp.sin(ang)
    rope_t = jnp.stack([cos.reshape(t, -1).T, sin.reshape(t, -1).T])
    zero = jnp.zeros_like(sin[:, 0])
    c = jnp.concatenate([cos[:, 0], cos[:, 0], cos[:, 1], cos[:, 1]], axis=-1)
    up = jnp.concatenate([-sin[:, 0], zero, -sin[:, 1], zero], axis=-1)
    down = jnp.concatenate([zero, sin[:, 0], zero, sin[:, 1]], axis=-1)
    rope_n = jnp.stack([jnp.tile(a, (1, PAIR // HEAD_DIM)) for a in (c, up, down)])
    return rope_t, rope_n


def _pair_gqa_heads(w, axis):
    shape = w.shape
    w = w.reshape(shape[:axis] + (2, N_SLOTS // 2, HEAD_DIM) + shape[axis + 1:])
    return jnp.swapaxes(w, axis, axis + 1).reshape(shape)


def _lambda_init(layer):
    return 0.8 - 0.6 * math.exp(-0.3 * layer)


def kernel(x, c, ctx, c_ctx, w_mod, b_mod, norm_g, w_in, w_out, w_mlp_in, w_mlp_out, qk_norm_a, diff_lambda,
           diff_subln, sink_c, rpb_d):
    b, t, d = x.shape
    n_ctx = ctx.shape[1]
    depth = w_mod.shape[0]
    mod_rows = 16
    cc = jnp.concatenate([c, c_ctx[None], jnp.zeros((mod_rows - b - 1, d), F32)], axis=0)
    mod_all = _modulation(cc, w_mod, b_mod)

    rope_args = _rope_tables(t)
    seg = jnp.kron(jnp.eye(2, dtype=F32), jnp.ones((HEAD_DIM, HEAD_DIM), F32)).astype(BF16)
    tm_x = 512

    for l in range(depth):
        last = l == depth - 1
        even = l % 2 == 0
        i = l // 2
        mod_x = mod_all[l, :b][:, None, :]
        mod_c = mod_all[l, b][None, None, :]
        w = w_in[l]
        o_k1, o_v1, o_q2, o_k2, o_v2 = Q_WIDTH, Q_WIDTH + PAIR, Q_WIDTH + 2 * PAIR, 2 * Q_WIDTH + 2 * PAIR, \
            3 * Q_WIDTH + 2 * PAIR
        wt = jnp.concatenate([_pair_gqa_heads(w[:, 0:o_k1], 1), w[:, o_v1:o_q2], w[:, o_q2:o_k2], w[:, o_v2:]],
                             axis=1).T.astype(BF16)
        wk = jnp.concatenate([w[:, o_k1:o_v1], w[:, o_k2:o_v2]], axis=1).astype(BF16)
        wo = jnp.concatenate([_pair_gqa_heads(w_out[l, 0:Q_WIDTH], 0), w_out[l, Q_WIDTH:]], axis=0).astype(BF16)
        w1 = w_mlp_in[l].astype(BF16)
        w2 = w_mlp_out[l].astype(BF16)
        g0 = norm_g[l, 0][None, :]

        def norm_args(tm):
            if not even:
                return None
            return (jnp.broadcast_to(qk_norm_a[i, 0][:, None], (HEAD_DIM, tm)),
                    jnp.tile(qk_norm_a[i, 1], PAIR // HEAD_DIM)[None, :], seg)

        q1x, k1x, v1x, q2x, k2x, v2x = _in_proj(x, mod_x, g0, wt, wk, norm_args(tm_x), rope_args,
                                                even=even, tm=tm_x)
        q1c, k1c, v1c, q2c, k2c, v2c = _in_proj(ctx, mod_c, g0, wt, wk, norm_args(n_ctx), None,
                                                even=even, tm=n_ctx)
        x_kv = (k1x, v1x, k2x, v2x)
        ctx_kv = (k1c, v1c, k2c, v2c)
        if even:
            lam0 = _lambda_init(l)
            sub = diff_subln[i][None, :]
            m_x = _attn_even((q1x, q2x), ctx_kv, x_kv, diff_lambda[i], sub, lam0=lam0, tq=256)
            if not last:
                m_c = _attn_even((q1c, q2c), ctx_kv, None, diff_lambda[i], sub, lam0=lam0, tq=256)
        else:
            if not last:
                raise NotImplementedError("context update after an odd layer is not needed for depth 2")
            tp = _bias_table(rpb_d[i])
            m_x = _attn_odd((q1x, q2x), x_kv, ctx_kv, sink_c[i], tp, tq=256, sink_order=GQA_SLOT_HEADS)
        x = _post(m_x, x, mod_x, norm_g[l], wo, w1, w2, tm=512, ff_chunk=1024)
        if not last:
            ctx = _post(m_c, ctx, mod_c, norm_g[l], wo, w1, w2, tm=n_ctx, ff_chunk=1024)
    return x
```

```python
import functools
import math
from typing import Any, Callable, NamedTuple, Sequence

import jax
import jax.numpy as jnp
from jax import lax
from jax.experimental import pallas as pl
from jax.experimental.pallas import tpu as pltpu

F32 = jnp.float32
BF16 = jnp.bfloat16

GRID_W = 64
HEAD_DIM = 64
ROPE_THETA = 10000.0
ROPE_FREQS = HEAD_DIM // 4
N_SLOTS = 8
PAIR = 2 * HEAD_DIM
Q_WIDTH = N_SLOTS * HEAD_DIM
LANES = 128
MXU_QUERY_COLS = 512
EVEN_KEY_CHUNK = 1024
IN_SUB_ROWS = 512
POST_SUB_ROWS = 256
NA_CHUNK_ROWS = 4
C_WINDOW = 128
NA_ROWS = 8
NA_COLS = 16
EPS = 1e-6
NEG = -1e30
LOG2E = math.log2(math.e)
Q_SCALE = HEAD_DIM ** -0.5 * LOG2E
GQA_SLOT_HEADS = tuple(half * (N_SLOTS // 2) + p for p in range(N_SLOTS // 2) for half in range(2))

VMEM_LIMIT_BYTES = 56 * 1024 * 1024
ATTN_SCHEDULER_FLAGS = None


def _compiler_params(n_grid_dims, flags=None):
    return pltpu.CompilerParams(dimension_semantics=("arbitrary",) * n_grid_dims,
                                vmem_limit_bytes=VMEM_LIMIT_BYTES, flags=flags)


def _dot(a, b):
    return jnp.dot(a, b, preferred_element_type=F32)


def _dot_nt(a, b):
    return lax.dot_general(a, b, (((1,), (1,)), ((), ())), preferred_element_type=F32)


def _log2(n):
    assert n & (n - 1) == 0
    return n.bit_length() - 1


def _rms(x, gain):
    ms = jnp.mean(x * x, axis=-1, keepdims=True)
    return x * lax.rsqrt(ms + EPS) * gain


def _mod_kernel(c_ref, w_ref, b_ref, o_ref):
    c = c_ref[...]
    s = c * jax.nn.sigmoid(c)
    o_ref[0] = _dot(s.astype(BF16), w_ref[0].astype(BF16)) + b_ref[0]


def _modulation(cc, w_mod, b_mod):
    depth, d, n = w_mod.shape
    rows = cc.shape[0]
    tn = n // 4
    return pl.pallas_call(
        _mod_kernel,
        grid=(depth, n // tn),
        in_specs=[pl.BlockSpec((rows, d), lambda l, j: (0, 0)),
                  pl.BlockSpec((1, d, tn), lambda l, j: (l, 0, j)),
                  pl.BlockSpec((1, 1, tn), lambda l, j: (l, 0, j))],
        out_specs=pl.BlockSpec((1, rows, tn), lambda l, j: (l, 0, j)),
        out_shape=jax.ShapeDtypeStruct((depth, rows, n), F32),
        compiler_params=_compiler_params(2),
        name="modulation",
    )(cc, w_mod, b_mod.reshape(depth, 1, n))


def _in_kernel(*refs, even, rope, sub_rows):
    refs = list(refs)
    tok_ref, mod_ref, g_ref, wt_ref, wk_ref = refs[:5]
    del refs[:5]
    if even:
        gq_ref, gk_ref, seg_ref = refs[:3]
        del refs[:3]
    if rope:
        rope_t_ref, rope_ref = refs[:2]
        del refs[:2]
    q1_ref, k1_ref, v1_ref, q2_ref, k2_ref, v2_ref = refs
    d = tok_ref.shape[-1]
    tm = tok_ref.shape[1]
    sub = min(sub_rows, tm)

    def normed(rows):
        h = _rms(tok_ref[0, rows], g_ref[...])
        h = h * (1.0 + mod_ref[0, :, d:2 * d]) + mod_ref[0, :, 0:d]
        return h.astype(BF16)

    def project(rows, hb):
        def rotary_t(c):
            f = ROPE_FREQS
            out = []
            for a in range(2):
                x1, x2 = c[2 * a * f:(2 * a + 1) * f], c[(2 * a + 1) * f:(2 * a + 2) * f]
                cos, sin = rope_t_ref[0, a * f:(a + 1) * f, rows], rope_t_ref[1, a * f:(a + 1) * f, rows]
                out += [x1 * cos - x2 * sin, x2 * cos + x1 * sin]
            return jnp.concatenate(out, axis=0)

        def store_slots(q_ref, qt, use_norm, use_rope):
            zeros = jnp.zeros((HEAD_DIM, sub), F32)
            for j in range(N_SLOTS):
                c = qt[j * HEAD_DIM:(j + 1) * HEAD_DIM]
                if use_norm:
                    ms = jnp.mean(c * c, axis=0, keepdims=True)
                    c = c * lax.rsqrt(ms + EPS) * gq_ref[:, 0:sub]
                if use_rope:
                    c = rotary_t(c)
                c = c * Q_SCALE
                slot = jnp.concatenate([c, zeros] if j % 2 == 0 else [zeros, c], axis=0)
                q_ref[0, j, :, rows] = slot.astype(BF16)

        def store_vt(v_ref, vt):
            if even:
                v_ref[0, :, rows] = vt.astype(BF16)
            else:
                for c in range(sub // LANES):
                    v_ref[0, rows.start // LANES + c] = vt[:, c * LANES:(c + 1) * LANES].astype(BF16)

        def rotary(c):
            up = pltpu.roll(c, PAIR - ROPE_FREQS, axis=1)
            down = pltpu.roll(c, ROPE_FREQS, axis=1)
            return c * rope_ref[0, rows] + up * rope_ref[1, rows] + down * rope_ref[2, rows]

        o = 0
        store_slots(q1_ref, _dot_nt(wt_ref[o:o + Q_WIDTH], hb), even, rope)
        o += Q_WIDTH
        store_vt(v1_ref, _dot_nt(wt_ref[o:o + PAIR], hb))
        o += PAIR
        store_slots(q2_ref, _dot_nt(wt_ref[o:o + Q_WIDTH], hb), False, rope and even)
        o += Q_WIDTH
        store_vt(v2_ref, _dot_nt(wt_ref[o:o + Q_WIDTH], hb))

        k = _dot(hb, wk_ref[...])
        k1 = k[:, 0:PAIR]
        if even:
            sq = k1 * k1
            hi = sq.astype(BF16)
            lo = (sq - hi.astype(F32)).astype(BF16)
            ms = (_dot(hi, seg_ref[...]) + _dot(lo, seg_ref[...])) * (1.0 / HEAD_DIM)
            k1 = k1 * lax.rsqrt(ms + EPS) * gk_ref[...]
        if rope:
            k1 = rotary(k1)
        k1_ref[0, rows] = k1.astype(BF16)
        for j in range(Q_WIDTH // PAIR):
            c = k[:, PAIR + j * PAIR:PAIR + (j + 1) * PAIR]
            if rope and even:
                c = rotary(c)
            k2_ref[0, rows, j * PAIR:(j + 1) * PAIR] = c.astype(BF16)

    blocks = [slice(r, r + sub) for r in range(0, tm, sub)]
    hbs = [normed(rows) for rows in blocks]
    for rows, hb in zip(blocks, hbs):
        project(rows, hb)


def _in_proj(tok, mod, gain, wt, wk, norm_args, rope_args, *, even, tm):
    b, s, d = tok.shape
    bm = mod.shape[0]
    rope = rope_args is not None
    mod_map = (lambda i, j: (i, 0, 0)) if bm > 1 else (lambda i, j: (0, 0, 0))

    def const(a):
        return pl.BlockSpec(a.shape, lambda i, j: (0,) * a.ndim)

    in_specs = [pl.BlockSpec((1, tm, d), lambda i, j: (i, j, 0)),
                pl.BlockSpec((1, 1, mod.shape[-1]), mod_map),
                const(gain), const(wt), const(wk)]
    args = [tok, mod, gain, wt, wk]
    if even:
        gq, gk, seg = norm_args
        in_specs += [const(gq), const(gk), const(seg)]
        args += [gq, gk, seg]
    if rope:
        rope_t, rope_n = rope_args
        in_specs += [pl.BlockSpec((2, 2 * ROPE_FREQS, tm), lambda i, j: (0, 0, j)),
                     pl.BlockSpec((3, tm, PAIR), lambda i, j: (0, j, 0))]
        args += [rope_t, rope_n]

    def vt_out(rows):
        if even:
            return (pl.BlockSpec((1, rows, tm), lambda i, j: (i, 0, j)),
                    jax.ShapeDtypeStruct((b, rows, s), BF16))
        return (pl.BlockSpec((1, tm // LANES, rows, LANES), lambda i, j: (i, j, 0, 0)),
                jax.ShapeDtypeStruct((b, s // LANES, rows, LANES), BF16))

    slot_out = (pl.BlockSpec((1, N_SLOTS, PAIR, tm), lambda i, j: (i, 0, 0, j)),
                jax.ShapeDtypeStruct((b, N_SLOTS, PAIR, s), BF16))

    def nat_out(cols):
        return (pl.BlockSpec((1, tm, cols), lambda i, j: (i, j, 0)), jax.ShapeDtypeStruct((b, s, cols), BF16))

    outs = [slot_out, nat_out(PAIR), vt_out(PAIR), slot_out, nat_out(Q_WIDTH), vt_out(Q_WIDTH)]
    return pl.pallas_call(
        functools.partial(_in_kernel, even=even, rope=rope, sub_rows=IN_SUB_ROWS),
        grid=(b, s // tm),
        in_specs=in_specs,
        out_specs=[o[0] for o in outs],
        out_shape=[o[1] for o in outs],
        compiler_params=_compiler_params(2),
        name="in_proj_" + ("even" if even else "odd") + ("_x" if rope else "_ctx"),
    )(*args)


class AttnJob(NamedTuple):
    query: Callable[[], Any]
    scores: Sequence[Callable[[Any], Any]]
    values: Sequence[Callable[[], Any]]
    finish: Callable[[Any], None]
    sink: Any = None


def _attend_lead(jobs):
    return max(len(job.scores) for job in jobs) + 1


def _attend_jobs(jobs, s_ref):
    items = [(j, c) for j, job in enumerate(jobs) for c in range(len(job.scores))]
    lead = _attend_lead(jobs)
    n_slots = s_ref.shape[0]
    assert n_slots > lead
    qt, shape, m, l, o = {}, {}, {}, {}, {}
    for t in range(len(items) + lead):
        if t < len(items):
            j, c = items[t]
            job = jobs[j]
            if c == 0:
                qt[j] = job.query()
            x = job.scores[c](qt[j])
            shape[t] = x.shape
            s_ref[t % n_slots, 0:x.shape[0], 0:x.shape[1]] = x
            part_m = jnp.max(x, axis=0, keepdims=True)
            if c == 0:
                m[j] = part_m if job.sink is None else jnp.maximum(part_m, job.sink)
            else:
                m[j] = jnp.maximum(m[j], part_m)
        if t >= lead:
            j, c = items[t - lead]
            job = jobs[j]
            rows, cols = shape.pop(t - lead)
            e = jnp.exp2(s_ref[(t - lead) % n_slots, 0:rows, 0:cols] - m[j])
            part_l = jnp.sum(e, axis=0, keepdims=True)
            part_o = _dot(job.values[c](), e.astype(BF16))
            l[j] = part_l if c == 0 else l[j] + part_l
            o[j] = part_o if c == 0 else o[j] + part_o
            if c == len(job.scores) - 1:
                denom = l.pop(j) if job.sink is None else l.pop(j) + jnp.exp2(job.sink - m[j])
                job.finish(o.pop(j) * (1.0 / denom))


def _attn_even_kernel(*refs, with_x, lam0, tq, key_chunk):
    if with_x:
        q1_ref, q2_ref, k1c_ref, v1c_ref, k2c_ref, v2c_ref, k1x_ref, v1x_ref, k2x_ref, v2x_ref = refs[:10]
    else:
        q1_ref, q2_ref, k1c_ref, v1c_ref, k2c_ref, v2c_ref = refs[:6]
    lam_ref, sub_ref, o_ref, s_ref = refs[-4:]

    def chunk_fns(c_ref, x_ref, cols, transposed):
        def make(ref, lo, hi):
            if transposed:
                return lambda: ref[0, cols, lo:hi]
            return lambda qt: _dot(ref[0, lo:hi, cols], qt)
        n_ctx = c_ref.shape[2] if transposed else c_ref.shape[1]
        fns = [make(c_ref, 0, n_ctx)]
        if with_x:
            n_x = x_ref.shape[2] if transposed else x_ref.shape[1]
            fns += [make(x_ref, lo, lo + key_chunk) for lo in range(0, n_x, key_chunk)]
        return fns

    group = max(2, MXU_QUERY_COLS // tq)
    jobs = []
    pair_cols = slice(0, PAIR)

    def a_query(g):
        return jnp.concatenate([q1_ref[0, g * group + j] for j in range(group)], axis=1)

    def a_finish(g, o):
        for pp in range(group // 2):
            pair_t = jnp.concatenate([o[0:HEAD_DIM, (2 * pp) * tq:(2 * pp + 1) * tq],
                                      o[HEAD_DIM:PAIR, (2 * pp + 1) * tq:(2 * pp + 2) * tq]], axis=0)
            col = (g * (group // 2) + pp) * PAIR
            o_ref[0, :, col:col + PAIR] = pair_t.T.astype(BF16)

    for g in range(N_SLOTS // group):
        jobs.append(AttnJob(functools.partial(a_query, g),
                            chunk_fns(k1c_ref, k1x_ref if with_x else None, pair_cols, False),
                            chunk_fns(v1c_ref, v1x_ref if with_x else None, pair_cols, True),
                            functools.partial(a_finish, g)))

    def b_query(h):
        return jnp.concatenate([q2_ref[0, 2 * h], q2_ref[0, 2 * h + 1]], axis=1)

    def b_finish(h, o):
        lp = lam_ref[...]
        lam = (jnp.exp(jnp.sum(lp[0:1] * lp[1:2], axis=-1, keepdims=True))
               - jnp.exp(jnp.sum(lp[2:3] * lp[3:4], axis=-1, keepdims=True)) + lam0)
        ob = (o[:, 0:tq] - lam * o[:, tq:2 * tq]).T
        ob = _rms(ob, sub_ref[...]) * (1.0 - lam0)
        col = Q_WIDTH + h * PAIR
        o_ref[0, :, col:col + PAIR] = ob.astype(BF16)

    for h in range(N_SLOTS // 2):
        cols = slice(h * PAIR, (h + 1) * PAIR)
        jobs.append(AttnJob(functools.partial(b_query, h),
                            chunk_fns(k2c_ref, k2x_ref if with_x else None, cols, False),
                            chunk_fns(v2c_ref, v2x_ref if with_x else None, cols, True),
                            functools.partial(b_finish, h)))
    _attend_jobs(jobs, s_ref)


def _attn_even(qx, ctx_kv, x_kv, lam_p, subln, *, lam0, tq):
    q1, q2 = qx
    b, _, _, s = q1.shape
    with_x = x_kv is not None
    slot_spec = pl.BlockSpec((1, N_SLOTS, PAIR, tq), lambda i, j: (i, 0, 0, j))

    def full(a):
        return pl.BlockSpec((1,) + a.shape[1:], lambda i, j: (i, 0, 0))

    kv = list(ctx_kv) + (list(x_kv) if with_x else [])
    n_ctx = ctx_kv[0].shape[1]
    key_chunk = min(EVEN_KEY_CHUNK, x_kv[0].shape[1]) if with_x else n_ctx
    chunks_per_job = 1 + (x_kv[0].shape[1] // key_chunk if with_x else 0)
    score_ring = pltpu.VMEM((chunks_per_job + 2, max(n_ctx, key_chunk), max(2 * tq, MXU_QUERY_COLS)), F32)
    return pl.pallas_call(
        functools.partial(_attn_even_kernel, with_x=with_x, lam0=lam0, tq=tq, key_chunk=key_chunk),
        grid=(b, s // tq),
        scratch_shapes=[score_ring],
        in_specs=[slot_spec, slot_spec] + [full(a) for a in kv]
                 + [pl.BlockSpec(lam_p.shape, lambda i, j: (0, 0)), pl.BlockSpec(subln.shape, lambda i, j: (0, 0))],
        out_specs=pl.BlockSpec((1, tq, 2 * Q_WIDTH), lambda i, j: (i, j, 0)),
        out_shape=jax.ShapeDtypeStruct((b, s, 2 * Q_WIDTH), BF16),
        compiler_params=_compiler_params(2, ATTN_SCHEDULER_FLAGS),
        name="attn_even" + ("_x" if with_x else "_ctx"),
    )(q1, q2, *kv, lam_p, subln)


N_BIAS_ROWS = 2 * NA_ROWS


def _bias_kernel(rpb_ref, o_ref):
    head = pl.program_id(0)
    n_dr = 2 * NA_ROWS - 1
    n_dc = 2 * NA_COLS - 1
    shape = (GRID_W, LANES)
    ck = lax.broadcasted_iota(jnp.int32, shape, 0)
    lane = lax.broadcasted_iota(jnp.int32, shape, 1)
    second = lane >= GRID_W
    dcl = jnp.clip(ck - (lane & (GRID_W - 1)), -(NA_COLS - 1), NA_COLS - 1) + (NA_COLS - 1)
    for t in range(N_BIAS_ROWS):
        acc = jnp.zeros(shape, F32)
        for j in range(n_dc):
            first_v = rpb_ref[(head * n_dr + min(t, n_dr - 1)) * n_dc + j]
            second_v = rpb_ref[(head * n_dr + max(t - 1, 0)) * n_dc + j]
            acc = jnp.where(dcl == j, jnp.where(second, second_v, first_v), acc)
        o_ref[0, t] = acc * LOG2E


def _bias_table(rpb):
    heads = rpb.shape[0]
    return pl.pallas_call(
        _bias_kernel,
        grid=(heads,),
        in_specs=[pl.BlockSpec(memory_space=pltpu.SMEM)],
        out_specs=pl.BlockSpec((1, N_BIAS_ROWS, GRID_W, LANES), lambda h: (h, 0, 0, 0)),
        out_shape=jax.ShapeDtypeStruct((heads, N_BIAS_ROWS, GRID_W, LANES), F32),
        compiler_params=_compiler_params(1),
        name="rel_bias_table",
    )(rpb.reshape(-1))


def _attn_odd_kernel(q1_ref, q2_ref, k1x_ref, v1x_ref, k2x_ref, v2x_ref, k1c_ref, v1c_ref, k2c_ref, v2c_ref,
                     sink_ref, tp_ref, o_ref, s_ref, *, tq, sink_order):
    i = pl.program_id(1)
    t = k1x_ref.shape[1]
    n_ctx_blk = v1c_ref.shape[1]
    rows = t // GRID_W
    kh = min(NA_ROWS, rows)
    n_blk = tq // LANES

    jobs = []

    n_q = N_SLOTS * C_WINDOW
    n_seq_blk = t // C_WINDOW
    lane_q = lax.broadcasted_iota(jnp.int32, (1, n_q), 1)
    sink = jnp.zeros((1, n_q), F32)
    for j in range(N_SLOTS):
        sink = jnp.where(lane_q >> _log2(C_WINDOW) == j, sink_ref[sink_order[j]], sink)
    sink = sink * LOG2E
    tri = (lax.broadcasted_iota(jnp.int32, (C_WINDOW, n_q), 0)
           - (lax.broadcasted_iota(jnp.int32, (C_WINDOW, n_q), 1) & (C_WINDOW - 1)))

    def c_blocks(jb):
        blk = i * n_blk + jb
        return blk, jnp.maximum(blk - 1, 0), jnp.minimum(blk + 1, n_seq_blk - 1)

    def c_query(jb):
        return jnp.concatenate([q1_ref[0, j, :, jb * LANES:(jb + 1) * LANES] for j in range(N_SLOTS)], axis=1)

    def c_local_scores(jb, qt):
        blk, prev, nxt = c_blocks(jb)
        k_loc = jnp.concatenate([k1x_ref[0, pl.ds(pl.multiple_of(n * C_WINDOW, C_WINDOW), C_WINDOW), :]
                                 for n in (prev, blk, nxt)], axis=0)
        s = _dot(k_loc, qt)
        lo_bound = jnp.where(blk > 0, 0, C_WINDOW)
        hi_bound = jnp.where(blk < n_seq_blk - 1, 0, -C_WINDOW)
        return jnp.concatenate([jnp.where(tri >= lo_bound, s[0:C_WINDOW], NEG),
                                s[C_WINDOW:2 * C_WINDOW],
                                jnp.where(tri <= hi_bound, s[2 * C_WINDOW:3 * C_WINDOW], NEG)], axis=0)

    def c_local_values(jb):
        blk, prev, nxt = c_blocks(jb)
        return jnp.concatenate([v1x_ref[0, n] for n in (prev, blk, nxt)], axis=1)

    def c_finish(jb, o):
        for p in range(N_SLOTS // 2):
            pair_t = jnp.concatenate([o[0:HEAD_DIM, (2 * p) * LANES:(2 * p + 1) * LANES],
                                      o[HEAD_DIM:PAIR, (2 * p + 1) * LANES:(2 * p + 2) * LANES]], axis=0)
            o_ref[0, jb * LANES:(jb + 1) * LANES, p * PAIR:(p + 1) * PAIR] = pair_t.T.astype(BF16)

    for jb in range(n_blk):
        jobs.append(AttnJob(
            functools.partial(c_query, jb),
            [functools.partial(c_local_scores, jb), lambda qt: _dot(k1c_ref[0], qt)],
            [functools.partial(c_local_values, jb),
             lambda: jnp.concatenate([v1c_ref[0, c] for c in range(n_ctx_blk)], axis=1)],
            functools.partial(c_finish, jb), sink))

    n_rows = tq // GRID_W
    span_rows = -(-(n_rows - 1 + kh) // 2) * 2
    n_loc = span_rows * GRID_W
    shape = (n_loc, 2 * tq)
    key_i = lax.broadcasted_iota(jnp.int32, shape, 0)
    lane_i = lax.broadcasted_iota(jnp.int32, shape, 1)
    kk = key_i >> _log2(GRID_W)
    ck = key_i & (GRID_W - 1)
    cq = lane_i & (GRID_W - 1)
    cs = jnp.clip(cq - NA_COLS // 2, 0, GRID_W - NA_COLS)
    r0 = i * n_rows
    u = jnp.minimum(jnp.clip(r0 - kh // 2, 0, rows - kh), rows - span_rows)
    q_row = (lax.broadcasted_iota(jnp.int32, (1, 2 * tq), 1) >> _log2(GRID_W)) & (n_rows - 1)
    first = jnp.zeros((1, 2 * tq), jnp.int32)
    for j in range(n_rows):
        first = jnp.where(q_row == j, jnp.clip(r0 + j - kh // 2, 0, rows - kh) - u, first)
    valid = (ck >= cs) & (ck < cs + NA_COLS) & (kk >= first) & (kk < first + kh)

    chunk_rows = NA_CHUNK_ROWS
    chunk_keys = chunk_rows * GRID_W
    blk_rows = LANES // GRID_W

    def d_query(p):
        return jnp.concatenate([q2_ref[0, 2 * p], q2_ref[0, 2 * p + 1]], axis=1)

    def d_local_scores(p, c, qt):
        k_loc = k2x_ref[0, pl.ds(pl.multiple_of((u + c * chunk_rows) * GRID_W, LANES), chunk_keys),
                        p * PAIR:(p + 1) * PAIR]
        bias = jnp.concatenate(
            [jnp.concatenate(
                [tp_ref[2 * p + lb // (n_rows // 2),
                        jnp.clip(u + a - (r0 + 2 * (lb % (n_rows // 2))) + NA_ROWS - 1, 0, N_BIAS_ROWS - 1)]
                 for lb in range(n_rows)], axis=1)
             for a in range(c * chunk_rows, (c + 1) * chunk_rows)], axis=0)
        return jnp.where(valid[c * chunk_keys:(c + 1) * chunk_keys], _dot(k_loc, qt) + bias, NEG)

    def d_local_values(p, c):
        blk0 = (u + c * chunk_rows) // blk_rows
        return jnp.concatenate([v2x_ref[0, blk0 + b, p * PAIR:(p + 1) * PAIR, :]
                                for b in range(chunk_rows // blk_rows)], axis=1)

    def d_finish(p, o):
        pair_t = jnp.concatenate([o[0:HEAD_DIM, 0:tq], o[HEAD_DIM:PAIR, tq:2 * tq]], axis=0)
        o_ref[0, :, Q_WIDTH + p * PAIR:Q_WIDTH + (p + 1) * PAIR] = pair_t.T.astype(BF16)

    for p in range(N_SLOTS // 2):
        cols = slice(p * PAIR, (p + 1) * PAIR)
        n_chunks = span_rows // chunk_rows
        jobs.append(AttnJob(
            functools.partial(d_query, p),
            [functools.partial(d_local_scores, p, c) for c in range(n_chunks)]
            + [lambda qt, cols=cols: _dot(k2c_ref[0, :, cols], qt)],
            [functools.partial(d_local_values, p, c) for c in range(n_chunks)]
            + [lambda cols=cols: jnp.concatenate([v2c_ref[0, b, cols, :] for b in range(n_ctx_blk)], axis=1)],
            functools.partial(d_finish, p)))
    _attend_jobs(jobs, s_ref)


def _attn_odd(qx, x_kv, ctx_kv, sink, tp, *, tq, sink_order):
    q1, q2 = qx
    b, _, _, s = q1.shape
    slot_spec = pl.BlockSpec((1, N_SLOTS, PAIR, tq), lambda i, j: (i, 0, 0, j))

    def full(a):
        return pl.BlockSpec((1,) + a.shape[1:], lambda i, j: (i,) + (0,) * (a.ndim - 1))

    kv = list(x_kv) + list(ctx_kv)
    n_ctx = ctx_kv[0].shape[1]
    rows = s // GRID_W
    span_rows = -(-(tq // GRID_W - 1 + min(NA_ROWS, rows)) // 2) * 2
    chunks_per_job = max(2, span_rows // NA_CHUNK_ROWS + 1)
    score_ring = pltpu.VMEM((chunks_per_job + 2, max(3 * C_WINDOW, NA_CHUNK_ROWS * GRID_W, n_ctx),
                             max(N_SLOTS * C_WINDOW, 2 * tq)), F32)
    return pl.pallas_call(
        functools.partial(_attn_odd_kernel, tq=tq, sink_order=sink_order),
        grid=(b, s // tq),
        scratch_shapes=[score_ring],
        in_specs=[slot_spec, slot_spec] + [full(a) for a in kv]
                 + [pl.BlockSpec(memory_space=pltpu.SMEM), pl.BlockSpec(tp.shape, lambda i, j: (0, 0, 0, 0))],
        out_specs=pl.BlockSpec((1, tq, 2 * Q_WIDTH), lambda i, j: (i, j, 0)),
        out_shape=jax.ShapeDtypeStruct((b, s, 2 * Q_WIDTH), BF16),
        compiler_params=_compiler_params(2),
        name="attn_odd",
    )(q1, q2, *kv, sink, tp)


def _post_kernel(m_ref, x_ref, mod_ref, g_ref, wo_ref, w1_ref, w2_ref, o_ref, *, ff_chunk, sub_rows):
    d = x_ref.shape[-1]

    def mod(k):
        return mod_ref[0, :, k * d:(k + 1) * d]

    def mixer_residual(rows):
        x1 = x_ref[0, rows] + mod(2) * _rms(_dot(m_ref[0, rows], wo_ref[...]), g_ref[1:2])
        return x1, (_rms(x1, g_ref[2:3]) * (1.0 + mod(4)) + mod(3)).astype(BF16)

    def mlp(h):
        f = None
        for c in range(w1_ref.shape[1] // ff_chunk):
            cols = slice(c * ff_chunk, (c + 1) * ff_chunk)
            a = jnp.maximum(_dot(h, w1_ref[:, cols]), 0.0)
            part = _dot((a * a).astype(BF16), w2_ref[cols, :])
            f = part if f is None else f + part
        return f

    tm = x_ref.shape[1]
    sub = min(sub_rows, tm)
    blocks = [slice(r, r + sub) for r in range(0, tm, sub)]
    stage1 = [mixer_residual(rows) for rows in blocks]
    stage2 = [mlp(h) for _, h in stage1]
    for rows, (x1, _), f in zip(blocks, stage1, stage2):
        o_ref[0, rows] = x1 + mod(5) * _rms(f, g_ref[3:4])


def _post(m, x, mod, gains, wo, w1, w2, *, tm, ff_chunk):
    b, s, d = x.shape
    bm = mod.shape[0]
    mod_map = (lambda i, j: (i, 0, 0)) if bm > 1 else (lambda i, j: (0, 0, 0))

    def const(a):
        return pl.BlockSpec(a.shape, lambda i, j: (0, 0), pipeline_mode=pl.Buffered(1))

    return pl.pallas_call(
        functools.partial(_post_kernel, ff_chunk=ff_chunk, sub_rows=POST_SUB_ROWS),
        grid=(b, s // tm),
        in_specs=[pl.BlockSpec((1, tm, m.shape[-1]), lambda i, j: (i, j, 0)),
                  pl.BlockSpec((1, tm, d), lambda i, j: (i, j, 0)),
                  pl.BlockSpec((1, 1, mod.shape[-1]), mod_map),
                  pl.BlockSpec(gains.shape, lambda i, j: (0, 0)),
                  const(wo), const(w1), const(w2)],
        out_specs=pl.BlockSpec((1, tm, d), lambda i, j: (i, j, 0)),
        out_shape=jax.ShapeDtypeStruct((b, s, d), F32),
        compiler_params=_compiler_params(2),
        name="post",
    )(m, x, mod, gains, wo, w1, w2)


def _rope_tables(t):
    tok = jnp.arange(t, dtype=jnp.int32)
    pos = jnp.stack([tok // GRID_W, tok % GRID_W], axis=-1).astype(F32)
    inv = ROPE_THETA ** (-jnp.arange(ROPE_FREQS, dtype=F32) / ROPE_FREQS)
    ang = pos[..., None] * inv
    cos, sin = jnp.cos(ang), jnp.sin(ang)
    rope_t = jnp.stack([cos.reshape(t, -1).T, sin.reshape(t, -1).T])
    zero = jnp.zeros_like(sin[:, 0])
    c = jnp.concatenate([cos[:, 0], cos[:, 0], cos[:, 1], cos[:, 1]], axis=-1)
    up = jnp.concatenate([-sin[:, 0], zero, -sin[:, 1], zero], axis=-1)
    down = jnp.concatenate([zero, sin[:, 0], zero, sin[:, 1]], axis=-1)
    rope_n = jnp.stack([jnp.tile(a, (1, PAIR // HEAD_DIM)) for a in (c, up, down)])
    return rope_t, rope_n


def _pair_gqa_heads(w, axis):
    shape = w.shape
    w = w.reshape(shape[:axis] + (2, N_SLOTS // 2, HEAD_DIM) + shape[axis + 1:])
    return jnp.swapaxes(w, axis, axis + 1).reshape(shape)


def _lambda_init(layer):
    return 0.8 - 0.6 * math.exp(-0.3 * layer)


def kernel(x, c, ctx, c_ctx, w_mod, b_mod, norm_g, w_in, w_out, w_mlp_in, w_mlp_out, qk_norm_a, diff_lambda,
           diff_subln, sink_c, rpb_d):
    b, t, d = x.shape
    n_ctx = ctx.shape[1]
    depth = w_mod.shape[0]
    mod_rows = 16
    cc = jnp.concatenate([c, c_ctx[None], jnp.zeros((mod_rows - b - 1, d), F32)], axis=0)
    mod_all = _modulation(cc, w_mod, b_mod)

    rope_args = _rope_tables(t)
    seg = jnp.kron(jnp.eye(2, dtype=F32), jnp.ones((HEAD_DIM, HEAD_DIM), F32)).astype(BF16)
    tm_x = 2 * IN_SUB_ROWS

    for l in range(depth):
        last = l == depth - 1
        even = l % 2 == 0
        i = l // 2
        mod_x = mod_all[l, :b][:, None, :]
        mod_c = mod_all[l, b][None, None, :]
        w = w_in[l]
        o_k1, o_v1, o_q2, o_k2, o_v2 = Q_WIDTH, Q_WIDTH + PAIR, Q_WIDTH + 2 * PAIR, 2 * Q_WIDTH + 2 * PAIR, \
            3 * Q_WIDTH + 2 * PAIR
        wt = jnp.concatenate([_pair_gqa_heads(w[:, 0:o_k1], 1), w[:, o_v1:o_q2], w[:, o_q2:o_k2], w[:, o_v2:]],
                             axis=1).T.astype(BF16)
        wk = jnp.concatenate([w[:, o_k1:o_v1], w[:, o_k2:o_v2]], axis=1).astype(BF16)
        wo = jnp.concatenate([_pair_gqa_heads(w_out[l, 0:Q_WIDTH], 0), w_out[l, Q_WIDTH:]], axis=0).astype(BF16)
        w1 = w_mlp_in[l].astype(BF16)
        w2 = w_mlp_out[l].astype(BF16)
        g0 = norm_g[l, 0][None, :]

        def norm_args(tm):
            if not even:
                return None
            return (jnp.broadcast_to(qk_norm_a[i, 0][:, None], (HEAD_DIM, tm)),
                    jnp.tile(qk_norm_a[i, 1], PAIR // HEAD_DIM)[None, :], seg)

        q1x, k1x, v1x, q2x, k2x, v2x = _in_proj(x, mod_x, g0, wt, wk, norm_args(tm_x), rope_args,
                                                even=even, tm=tm_x)
        q1c, k1c, v1c, q2c, k2c, v2c = _in_proj(ctx, mod_c, g0, wt, wk, norm_args(n_ctx), None,
                                                even=even, tm=n_ctx)
        x_kv = (k1x, v1x, k2x, v2x)
        ctx_kv = (k1c, v1c, k2c, v2c)
        if even:
            lam0 = _lambda_init(l)
            sub = diff_subln[i][None, :]
            m_x = _attn_even((q1x, q2x), ctx_kv, x_kv, diff_lambda[i], sub, lam0=lam0, tq=256)
            if not last:
                m_c = _attn_even((q1c, q2c), ctx_kv, None, diff_lambda[i], sub, lam0=lam0, tq=256)
        else:
            if not last:
                raise NotImplementedError("context update after an odd layer is not needed for depth 2")
            tp = _bias_table(rpb_d[i])
            m_x = _attn_odd((q1x, q2x), x_kv, ctx_kv, sink_c[i], tp, tq=256, sink_order=GQA_SLOT_HEADS)
        x = _post(m_x, x, mod_x, norm_g[l], wo, w1, w2, tm=512, ff_chunk=1024)
        if not last:
            ctx = _post(m_c, ctx, mod_c, norm_g[l], wo, w1, w2, tm=n_ctx, ff_chunk=1024)
    return x
```

```python
import functools
import math
from typing import Any, Callable, NamedTuple, Sequence

import jax
import jax.numpy as jnp
from jax import lax
from jax.experimental import pallas as pl
from jax.experimental.pallas import tpu as pltpu

F32 = jnp.float32
BF16 = jnp.bfloat16

GRID_W = 64
HEAD_DIM = 64
ROPE_THETA = 10000.0
ROPE_FREQS = HEAD_DIM // 4
N_SLOTS = 8
PAIR = 2 * HEAD_DIM
Q_WIDTH = N_SLOTS * HEAD_DIM
LANES = 128
MXU_QUERY_COLS = 512
EVEN_KEY_CHUNK = 1024
IN_SUB_ROWS = 512
ATTEND_EXTRA_LEAD = 2
POST_SUB_ROWS = 256
NA_CHUNK_ROWS = 4
C_WINDOW = 128
NA_ROWS = 8
NA_COLS = 16
EPS = 1e-6
NEG = -1e30
LOG2E = math.log2(math.e)
Q_SCALE = HEAD_DIM ** -0.5 * LOG2E
GQA_SLOT_HEADS = tuple(half * (N_SLOTS // 2) + p for p in range(N_SLOTS // 2) for half in range(2))

VMEM_LIMIT_BYTES = 56 * 1024 * 1024
ATTN_SCHEDULER_FLAGS = None


def _compiler_params(n_grid_dims, flags=None):
    return pltpu.CompilerParams(dimension_semantics=("arbitrary",) * n_grid_dims,
                                vmem_limit_bytes=VMEM_LIMIT_BYTES, flags=flags)


def _dot(a, b):
    return jnp.dot(a, b, preferred_element_type=F32)


def _dot_nt(a, b):
    return lax.dot_general(a, b, (((1,), (1,)), ((), ())), preferred_element_type=F32)


def _log2(n):
    assert n & (n - 1) == 0
    return n.bit_length() - 1


def _rms(x, gain):
    ms = jnp.mean(x * x, axis=-1, keepdims=True)
    return x * lax.rsqrt(ms + EPS) * gain


def _mod_kernel(c_ref, w_ref, b_ref, o_ref):
    c = c_ref[...]
    s = c * jax.nn.sigmoid(c)
    o_ref[0] = _dot(s.astype(BF16), w_ref[0].astype(BF16)) + b_ref[0]


def _modulation(cc, w_mod, b_mod):
    depth, d, n = w_mod.shape
    rows = cc.shape[0]
    tn = n // 4
    return pl.pallas_call(
        _mod_kernel,
        grid=(depth, n // tn),
        in_specs=[pl.BlockSpec((rows, d), lambda l, j: (0, 0)),
                  pl.BlockSpec((1, d, tn), lambda l, j: (l, 0, j)),
                  pl.BlockSpec((1, 1, tn), lambda l, j: (l, 0, j))],
        out_specs=pl.BlockSpec((1, rows, tn), lambda l, j: (l, 0, j)),
        out_shape=jax.ShapeDtypeStruct((depth, rows, n), F32),
        compiler_params=_compiler_params(2),
        name="modulation",
    )(cc, w_mod, b_mod.reshape(depth, 1, n))


def _in_kernel(*refs, even, rope, sub_rows):
    refs = list(refs)
    tok_ref, mod_ref, g_ref, wt_ref, wk_ref = refs[:5]
    del refs[:5]
    if even:
        gq_ref, gk_ref, seg_ref = refs[:3]
        del refs[:3]
    if rope:
        rope_t_ref, rope_ref = refs[:2]
        del refs[:2]
    q1_ref, k1_ref, v1_ref, q2_ref, k2_ref, v2_ref = refs
    d = tok_ref.shape[-1]
    tm = tok_ref.shape[1]
    sub = min(sub_rows, tm)

    def normed(rows):
        h = _rms(tok_ref[0, rows], g_ref[...])
        h = h * (1.0 + mod_ref[0, :, d:2 * d]) + mod_ref[0, :, 0:d]
        return h.astype(BF16)

    def project(rows, hb):
        def rotary_t(c):
            f = ROPE_FREQS
            out = []
            for a in range(2):
                x1, x2 = c[2 * a * f:(2 * a + 1) * f], c[(2 * a + 1) * f:(2 * a + 2) * f]
                cos, sin = rope_t_ref[0, a * f:(a + 1) * f, rows], rope_t_ref[1, a * f:(a + 1) * f, rows]
                out += [x1 * cos - x2 * sin, x2 * cos + x1 * sin]
            return jnp.concatenate(out, axis=0)

        def store_slots(q_ref, qt, use_norm, use_rope):
            zeros = jnp.zeros((HEAD_DIM, sub), F32)
            for j in range(N_SLOTS):
                c = qt[j * HEAD_DIM:(j + 1) * HEAD_DIM]
                if use_norm:
                    ms = jnp.mean(c * c, axis=0, keepdims=True)
                    c = c * lax.rsqrt(ms + EPS) * gq_ref[:, 0:sub]
                if use_rope:
                    c = rotary_t(c)
                c = c * Q_SCALE
                slot = jnp.concatenate([c, zeros] if j % 2 == 0 else [zeros, c], axis=0)
                q_ref[0, j, :, rows] = slot.astype(BF16)

        def store_vt(v_ref, vt):
            if even:
                v_ref[0, :, rows] = vt.astype(BF16)
            else:
                for c in range(sub // LANES):
                    v_ref[0, rows.start // LANES + c] = vt[:, c * LANES:(c + 1) * LANES].astype(BF16)

        def rotary(c):
            up = pltpu.roll(c, PAIR - ROPE_FREQS, axis=1)
            down = pltpu.roll(c, ROPE_FREQS, axis=1)
            return c * rope_ref[0, rows] + up * rope_ref[1, rows] + down * rope_ref[2, rows]

        o = 0
        store_slots(q1_ref, _dot_nt(wt_ref[o:o + Q_WIDTH], hb), even, rope)
        o += Q_WIDTH
        store_vt(v1_ref, _dot_nt(wt_ref[o:o + PAIR], hb))
        o += PAIR
        store_slots(q2_ref, _dot_nt(wt_ref[o:o + Q_WIDTH], hb), False, rope and even)
        o += Q_WIDTH
        store_vt(v2_ref, _dot_nt(wt_ref[o:o + Q_WIDTH], hb))

        k = _dot(hb, wk_ref[...])
        k1 = k[:, 0:PAIR]
        if even:
            sq = k1 * k1
            hi = sq.astype(BF16)
            lo = (sq - hi.astype(F32)).astype(BF16)
            ms = (_dot(hi, seg_ref[...]) + _dot(lo, seg_ref[...])) * (1.0 / HEAD_DIM)
            k1 = k1 * lax.rsqrt(ms + EPS) * gk_ref[...]
        if rope:
            k1 = rotary(k1)
        k1_ref[0, rows] = k1.astype(BF16)
        for j in range(Q_WIDTH // PAIR):
            c = k[:, PAIR + j * PAIR:PAIR + (j + 1) * PAIR]
            if rope and even:
                c = rotary(c)
            k2_ref[0, rows, j * PAIR:(j + 1) * PAIR] = c.astype(BF16)

    blocks = [slice(r, r + sub) for r in range(0, tm, sub)]
    hbs = [normed(rows) for rows in blocks]
    for rows, hb in zip(blocks, hbs):
        project(rows, hb)


def _in_proj(tok, mod, gain, wt, wk, norm_args, rope_args, *, even, tm):
    b, s, d = tok.shape
    bm = mod.shape[0]
    rope = rope_args is not None
    mod_map = (lambda i, j: (i, 0, 0)) if bm > 1 else (lambda i, j: (0, 0, 0))

    def const(a):
        return pl.BlockSpec(a.shape, lambda i, j: (0,) * a.ndim)

    in_specs = [pl.BlockSpec((1, tm, d), lambda i, j: (i, j, 0)),
                pl.BlockSpec((1, 1, mod.shape[-1]), mod_map),
                const(gain), const(wt), const(wk)]
    args = [tok, mod, gain, wt, wk]
    if even:
        gq, gk, seg = norm_args
        in_specs += [const(gq), const(gk), const(seg)]
        args += [gq, gk, seg]
    if rope:
        rope_t, rope_n = rope_args
        in_specs += [pl.BlockSpec((2, 2 * ROPE_FREQS, tm), lambda i, j: (0, 0, j)),
                     pl.BlockSpec((3, tm, PAIR), lambda i, j: (0, j, 0))]
        args += [rope_t, rope_n]

    def vt_out(rows):
        if even:
            return (pl.BlockSpec((1, rows, tm), lambda i, j: (i, 0, j)),
                    jax.ShapeDtypeStruct((b, rows, s), BF16))
        return (pl.BlockSpec((1, tm // LANES, rows, LANES), lambda i, j: (i, j, 0, 0)),
                jax.ShapeDtypeStruct((b, s // LANES, rows, LANES), BF16))

    slot_out = (pl.BlockSpec((1, N_SLOTS, PAIR, tm), lambda i, j: (i, 0, 0, j)),
                jax.ShapeDtypeStruct((b, N_SLOTS, PAIR, s), BF16))

    def nat_out(cols):
        return (pl.BlockSpec((1, tm, cols), lambda i, j: (i, j, 0)), jax.ShapeDtypeStruct((b, s, cols), BF16))

    outs = [slot_out, nat_out(PAIR), vt_out(PAIR), slot_out, nat_out(Q_WIDTH), vt_out(Q_WIDTH)]
    return pl.pallas_call(
        functools.partial(_in_kernel, even=even, rope=rope, sub_rows=IN_SUB_ROWS),
        grid=(b, s // tm),
        in_specs=in_specs,
        out_specs=[o[0] for o in outs],
        out_shape=[o[1] for o in outs],
        compiler_params=_compiler_params(2),
        name="in_proj_" + ("even" if even else "odd") + ("_x" if rope else "_ctx"),
    )(*args)


class AttnJob(NamedTuple):
    query: Callable[[], Any]
    scores: Sequence[Callable[[Any], Any]]
    values: Sequence[Callable[[], Any]]
    finish: Callable[[Any], None]
    sink: Any = None


def _attend_lead(jobs):
    return max(len(job.scores) for job in jobs) + ATTEND_EXTRA_LEAD


def _attend_jobs(jobs, s_ref):
    items = [(j, c) for j, job in enumerate(jobs) for c in range(len(job.scores))]
    lead = _attend_lead(jobs)
    n_slots = s_ref.shape[0]
    assert n_slots > lead
    qt, shape, m, l, o = {}, {}, {}, {}, {}
    for t in range(len(items) + lead):
        if t < len(items):
            j, c = items[t]
            job = jobs[j]
            if c == 0:
                qt[j] = job.query()
            x = job.scores[c](qt[j])
            shape[t] = x.shape
            s_ref[t % n_slots, 0:x.shape[0], 0:x.shape[1]] = x
            part_m = jnp.max(x, axis=0, keepdims=True)
            if c == 0:
                m[j] = part_m if job.sink is None else jnp.maximum(part_m, job.sink)
            else:
                m[j] = jnp.maximum(m[j], part_m)
        if t >= lead:
            j, c = items[t - lead]
            job = jobs[j]
            rows, cols = shape.pop(t - lead)
            e = jnp.exp2(s_ref[(t - lead) % n_slots, 0:rows, 0:cols] - m[j])
            part_l = jnp.sum(e, axis=0, keepdims=True)
            part_o = _dot(job.values[c](), e.astype(BF16))
            l[j] = part_l if c == 0 else l[j] + part_l
            o[j] = part_o if c == 0 else o[j] + part_o
            if c == len(job.scores) - 1:
                denom = l.pop(j) if job.sink is None else l.pop(j) + jnp.exp2(job.sink - m[j])
                job.finish(o.pop(j) * (1.0 / denom))


def _attn_even_kernel(*refs, with_x, lam0, tq, key_chunk):
    if with_x:
        q1_ref, q2_ref, k1c_ref, v1c_ref, k2c_ref, v2c_ref, k1x_ref, v1x_ref, k2x_ref, v2x_ref = refs[:10]
    else:
        q1_ref, q2_ref, k1c_ref, v1c_ref, k2c_ref, v2c_ref = refs[:6]
    lam_ref, sub_ref, o_ref, s_ref = refs[-4:]

    def chunk_fns(c_ref, x_ref, cols, transposed):
        def make(ref, lo, hi):
            if transposed:
                return lambda: ref[0, cols, lo:hi]
            return lambda qt: _dot(ref[0, lo:hi, cols], qt)
        n_ctx = c_ref.shape[2] if transposed else c_ref.shape[1]
        fns = [make(c_ref, 0, n_ctx)]
        if with_x:
            n_x = x_ref.shape[2] if transposed else x_ref.shape[1]
            fns += [make(x_ref, lo, lo + key_chunk) for lo in range(0, n_x, key_chunk)]
        return fns

    group = max(2, MXU_QUERY_COLS // tq)
    jobs = []
    pair_cols = slice(0, PAIR)

    def a_query(g):
        return jnp.concatenate([q1_ref[0, g * group + j] for j in range(group)], axis=1)

    def a_finish(g, o):
        for pp in range(group // 2):
            pair_t = jnp.concatenate([o[0:HEAD_DIM, (2 * pp) * tq:(2 * pp + 1) * tq],
                                      o[HEAD_DIM:PAIR, (2 * pp + 1) * tq:(2 * pp + 2) * tq]], axis=0)
            col = (g * (group // 2) + pp) * PAIR
            o_ref[0, :, col:col + PAIR] = pair_t.T.astype(BF16)

    for g in range(N_SLOTS // group):
        jobs.append(AttnJob(functools.partial(a_query, g),
                            chunk_fns(k1c_ref, k1x_ref if with_x else None, pair_cols, False),
                            chunk_fns(v1c_ref, v1x_ref if with_x else None, pair_cols, True),
                            functools.partial(a_finish, g)))

    def b_query(h):
        return jnp.concatenate([q2_ref[0, 2 * h], q2_ref[0, 2 * h + 1]], axis=1)

    def b_finish(h, o):
        lp = lam_ref[...]
        lam = (jnp.exp(jnp.sum(lp[0:1] * lp[1:2], axis=-1, keepdims=True))
               - jnp.exp(jnp.sum(lp[2:3] * lp[3:4], axis=-1, keepdims=True)) + lam0)
        ob = (o[:, 0:tq] - lam * o[:, tq:2 * tq]).T
        ob = _rms(ob, sub_ref[...]) * (1.0 - lam0)
        col = Q_WIDTH + h * PAIR
        o_ref[0, :, col:col + PAIR] = ob.astype(BF16)

    for h in range(N_SLOTS // 2):
        cols = slice(h * PAIR, (h + 1) * PAIR)
        jobs.append(AttnJob(functools.partial(b_query, h),
                            chunk_fns(k2c_ref, k2x_ref if with_x else None, cols, False),
                            chunk_fns(v2c_ref, v2x_ref if with_x else None, cols, True),
                            functools.partial(b_finish, h)))
    _attend_jobs(jobs, s_ref)


def _attn_even(qx, ctx_kv, x_kv, lam_p, subln, *, lam0, tq):
    q1, q2 = qx
    b, _, _, s = q1.shape
    with_x = x_kv is not None
    slot_spec = pl.BlockSpec((1, N_SLOTS, PAIR, tq), lambda i, j: (i, 0, 0, j))

    def full(a):
        return pl.BlockSpec((1,) + a.shape[1:], lambda i, j: (i, 0, 0))

    kv = list(ctx_kv) + (list(x_kv) if with_x else [])
    n_ctx = ctx_kv[0].shape[1]
    key_chunk = min(EVEN_KEY_CHUNK, x_kv[0].shape[1]) if with_x else n_ctx
    chunks_per_job = 1 + (x_kv[0].shape[1] // key_chunk if with_x else 0)
    score_ring = pltpu.VMEM((chunks_per_job + ATTEND_EXTRA_LEAD + 1, max(n_ctx, key_chunk),
                             max(2 * tq, MXU_QUERY_COLS)), F32)
    return pl.pallas_call(
        functools.partial(_attn_even_kernel, with_x=with_x, lam0=lam0, tq=tq, key_chunk=key_chunk),
        grid=(b, s // tq),
        scratch_shapes=[score_ring],
        in_specs=[slot_spec, slot_spec] + [full(a) for a in kv]
                 + [pl.BlockSpec(lam_p.shape, lambda i, j: (0, 0)), pl.BlockSpec(subln.shape, lambda i, j: (0, 0))],
        out_specs=pl.BlockSpec((1, tq, 2 * Q_WIDTH), lambda i, j: (i, j, 0)),
        out_shape=jax.ShapeDtypeStruct((b, s, 2 * Q_WIDTH), BF16),
        compiler_params=_compiler_params(2, ATTN_SCHEDULER_FLAGS),
        name="attn_even" + ("_x" if with_x else "_ctx"),
    )(q1, q2, *kv, lam_p, subln)


N_BIAS_ROWS = 2 * NA_ROWS


def _bias_kernel(rpb_ref, o_ref):
    head = pl.program_id(0)
    n_dr = 2 * NA_ROWS - 1
    n_dc = 2 * NA_COLS - 1
    shape = (GRID_W, LANES)
    ck = lax.broadcasted_iota(jnp.int32, shape, 0)
    lane = lax.broadcasted_iota(jnp.int32, shape, 1)
    second = lane >= GRID_W
    dcl = jnp.clip(ck - (lane & (GRID_W - 1)), -(NA_COLS - 1), NA_COLS - 1) + (NA_COLS - 1)
    for t in range(N_BIAS_ROWS):
        acc = jnp.zeros(shape, F32)
        for j in range(n_dc):
            first_v = rpb_ref[(head * n_dr + min(t, n_dr - 1)) * n_dc + j]
            second_v = rpb_ref[(head * n_dr + max(t - 1, 0)) * n_dc + j]
            acc = jnp.where(dcl == j, jnp.where(second, second_v, first_v), acc)
        o_ref[0, t] = acc * LOG2E


def _bias_table(rpb):
    heads = rpb.shape[0]
    return pl.pallas_call(
        _bias_kernel,
        grid=(heads,),
        in_specs=[pl.BlockSpec(memory_space=pltpu.SMEM)],
        out_specs=pl.BlockSpec((1, N_BIAS_ROWS, GRID_W, LANES), lambda h: (h, 0, 0, 0)),
        out_shape=jax.ShapeDtypeStruct((heads, N_BIAS_ROWS, GRID_W, LANES), F32),
        compiler_params=_compiler_params(1),
        name="rel_bias_table",
    )(rpb.reshape(-1))


def _attn_odd_kernel(q1_ref, q2_ref, k1x_ref, v1x_ref, k2x_ref, v2x_ref, k1c_ref, v1c_ref, k2c_ref, v2c_ref,
                     sink_ref, tp_ref, o_ref, s_ref, *, tq, sink_order):
    i = pl.program_id(1)
    t = k1x_ref.shape[1]
    n_ctx_blk = v1c_ref.shape[1]
    rows = t // GRID_W
    kh = min(NA_ROWS, rows)
    n_blk = tq // LANES

    jobs = []

    n_q = N_SLOTS * C_WINDOW
    n_seq_blk = t // C_WINDOW
    lane_q = lax.broadcasted_iota(jnp.int32, (1, n_q), 1)
    sink = jnp.zeros((1, n_q), F32)
    for j in range(N_SLOTS):
        sink = jnp.where(lane_q >> _log2(C_WINDOW) == j, sink_ref[sink_order[j]], sink)
    sink = sink * LOG2E
    tri = (lax.broadcasted_iota(jnp.int32, (C_WINDOW, n_q), 0)
           - (lax.broadcasted_iota(jnp.int32, (C_WINDOW, n_q), 1) & (C_WINDOW - 1)))

    def c_blocks(jb):
        blk = i * n_blk + jb
        return blk, jnp.maximum(blk - 1, 0), jnp.minimum(blk + 1, n_seq_blk - 1)

    def c_query(jb):
        return jnp.concatenate([q1_ref[0, j, :, jb * LANES:(jb + 1) * LANES] for j in range(N_SLOTS)], axis=1)

    def c_local_scores(jb, qt):
        blk, prev, nxt = c_blocks(jb)
        k_loc = jnp.concatenate([k1x_ref[0, pl.ds(pl.multiple_of(n * C_WINDOW, C_WINDOW), C_WINDOW), :]
                                 for n in (prev, blk, nxt)], axis=0)
        s = _dot(k_loc, qt)
        lo_bound = jnp.where(blk > 0, 0, C_WINDOW)
        hi_bound = jnp.where(blk < n_seq_blk - 1, 0, -C_WINDOW)
        return jnp.concatenate([jnp.where(tri >= lo_bound, s[0:C_WINDOW], NEG),
                                s[C_WINDOW:2 * C_WINDOW],
                                jnp.where(tri <= hi_bound, s[2 * C_WINDOW:3 * C_WINDOW], NEG)], axis=0)

    def c_local_values(jb):
        blk, prev, nxt = c_blocks(jb)
        return jnp.concatenate([v1x_ref[0, n] for n in (prev, blk, nxt)], axis=1)

    def c_finish(jb, o):
        for p in range(N_SLOTS // 2):
            pair_t = jnp.concatenate([o[0:HEAD_DIM, (2 * p) * LANES:(2 * p + 1) * LANES],
                                      o[HEAD_DIM:PAIR, (2 * p + 1) * LANES:(2 * p + 2) * LANES]], axis=0)
            o_ref[0, jb * LANES:(jb + 1) * LANES, p * PAIR:(p + 1) * PAIR] = pair_t.T.astype(BF16)

    for jb in range(n_blk):
        jobs.append(AttnJob(
            functools.partial(c_query, jb),
            [functools.partial(c_local_scores, jb), lambda qt: _dot(k1c_ref[0], qt)],
            [functools.partial(c_local_values, jb),
             lambda: jnp.concatenate([v1c_ref[0, c] for c in range(n_ctx_blk)], axis=1)],
            functools.partial(c_finish, jb), sink))

    n_rows = tq // GRID_W
    span_rows = -(-(n_rows - 1 + kh) // 2) * 2
    n_loc = span_rows * GRID_W
    shape = (n_loc, 2 * tq)
    key_i = lax.broadcasted_iota(jnp.int32, shape, 0)
    lane_i = lax.broadcasted_iota(jnp.int32, shape, 1)
    kk = key_i >> _log2(GRID_W)
    ck = key_i & (GRID_W - 1)
    cq = lane_i & (GRID_W - 1)
    cs = jnp.clip(cq - NA_COLS // 2, 0, GRID_W - NA_COLS)
    r0 = i * n_rows
    u = jnp.minimum(jnp.clip(r0 - kh // 2, 0, rows - kh), rows - span_rows)
    q_row = (lax.broadcasted_iota(jnp.int32, (1, 2 * tq), 1) >> _log2(GRID_W)) & (n_rows - 1)
    first = jnp.zeros((1, 2 * tq), jnp.int32)
    for j in range(n_rows):
        first = jnp.where(q_row == j, jnp.clip(r0 + j - kh // 2, 0, rows - kh) - u, first)
    valid = (ck >= cs) & (ck < cs + NA_COLS) & (kk >= first) & (kk < first + kh)

    chunk_rows = NA_CHUNK_ROWS
    chunk_keys = chunk_rows * GRID_W
    blk_rows = LANES // GRID_W

    def d_query(p):
        return jnp.concatenate([q2_ref[0, 2 * p], q2_ref[0, 2 * p + 1]], axis=1)

    def d_local_scores(p, c, qt):
        k_loc = k2x_ref[0, pl.ds(pl.multiple_of((u + c * chunk_rows) * GRID_W, LANES), chunk_keys),
                        p * PAIR:(p + 1) * PAIR]
        bias = jnp.concatenate(
            [jnp.concatenate(
                [tp_ref[2 * p + lb // (n_rows // 2),
                        jnp.clip(u + a - (r0 + 2 * (lb % (n_rows // 2))) + NA_ROWS - 1, 0, N_BIAS_ROWS - 1)]
                 for lb in range(n_rows)], axis=1)
             for a in range(c * chunk_rows, (c + 1) * chunk_rows)], axis=0)
        return jnp.where(valid[c * chunk_keys:(c + 1) * chunk_keys], _dot(k_loc, qt) + bias, NEG)

    def d_local_values(p, c):
        blk0 = (u + c * chunk_rows) // blk_rows
        return jnp.concatenate([v2x_ref[0, blk0 + b, p * PAIR:(p + 1) * PAIR, :]
                                for b in range(chunk_rows // blk_rows)], axis=1)

    def d_finish(p, o):
        pair_t = jnp.concatenate([o[0:HEAD_DIM, 0:tq], o[HEAD_DIM:PAIR, tq:2 * tq]], axis=0)
        o_ref[0, :, Q_WIDTH + p * PAIR:Q_WIDTH + (p + 1) * PAIR] = pair_t.T.astype(BF16)

    for p in range(N_SLOTS // 2):
        cols = slice(p * PAIR, (p + 1) * PAIR)
        n_chunks = span_rows // chunk_rows
        jobs.append(AttnJob(
            functools.partial(d_query, p),
            [functools.partial(d_local_scores, p, c) for c in range(n_chunks)]
            + [lambda qt, cols=cols: _dot(k2c_ref[0, :, cols], qt)],
            [functools.partial(d_local_values, p, c) for c in range(n_chunks)]
            + [lambda cols=cols: jnp.concatenate([v2c_ref[0, b, cols, :] for b in range(n_ctx_blk)], axis=1)],
            functools.partial(d_finish, p)))
    _attend_jobs(jobs, s_ref)


def _attn_odd(qx, x_kv, ctx_kv, sink, tp, *, tq, sink_order):
    q1, q2 = qx
    b, _, _, s = q1.shape
    slot_spec = pl.BlockSpec((1, N_SLOTS, PAIR, tq), lambda i, j: (i, 0, 0, j))

    def full(a):
        return pl.BlockSpec((1,) + a.shape[1:], lambda i, j: (i,) + (0,) * (a.ndim - 1))

    kv = list(x_kv) + list(ctx_kv)
    n_ctx = ctx_kv[0].shape[1]
    rows = s // GRID_W
    span_rows = -(-(tq // GRID_W - 1 + min(NA_ROWS, rows)) // 2) * 2
    chunks_per_job = max(2, span_rows // NA_CHUNK_ROWS + 1)
    score_ring = pltpu.VMEM((chunks_per_job + ATTEND_EXTRA_LEAD + 1, max(3 * C_WINDOW, NA_CHUNK_ROWS * GRID_W, n_ctx),
                             max(N_SLOTS * C_WINDOW, 2 * tq)), F32)
    return pl.pallas_call(
        functools.partial(_attn_odd_kernel, tq=tq, sink_order=sink_order),
        grid=(b, s // tq),
        scratch_shapes=[score_ring],
        in_specs=[slot_spec, slot_spec] + [full(a) for a in kv]
                 + [pl.BlockSpec(memory_space=pltpu.SMEM), pl.BlockSpec(tp.shape, lambda i, j: (0, 0, 0, 0))],
        out_specs=pl.BlockSpec((1, tq, 2 * Q_WIDTH), lambda i, j: (i, j, 0)),
        out_shape=jax.ShapeDtypeStruct((b, s, 2 * Q_WIDTH), BF16),
        compiler_params=_compiler_params(2),
        name="attn_odd",
    )(q1, q2, *kv, sink, tp)


def _post_kernel(m_ref, x_ref, mod_ref, g_ref, wo_ref, w1_ref, w2_ref, o_ref, *, ff_chunk, sub_rows):
    d = x_ref.shape[-1]

    def mod(k):
        return mod_ref[0, :, k * d:(k + 1) * d]

    def mixer_residual(rows):
        x1 = x_ref[0, rows] + mod(2) * _rms(_dot(m_ref[0, rows], wo_ref[...]), g_ref[1:2])
        return x1, (_rms(x1, g_ref[2:3]) * (1.0 + mod(4)) + mod(3)).astype(BF16)

    def mlp(h):
        f = None
        for c in range(w1_ref.shape[1] // ff_chunk):
            cols = slice(c * ff_chunk, (c + 1) * ff_chunk)
            a = jnp.maximum(_dot(h, w1_ref[:, cols]), 0.0)
            part = _dot((a * a).astype(BF16), w2_ref[cols, :])
            f = part if f is None else f + part
        return f

    tm = x_ref.shape[1]
    sub = min(sub_rows, tm)
    blocks = [slice(r, r + sub) for r in range(0, tm, sub)]
    stage1 = [mixer_residual(rows) for rows in blocks]
    stage2 = [mlp(h) for _, h in stage1]
    for rows, (x1, _), f in zip(blocks, stage1, stage2):
        o_ref[0, rows] = x1 + mod(5) * _rms(f, g_ref[3:4])


def _post(m, x, mod, gains, wo, w1, w2, *, tm, ff_chunk):
    b, s, d = x.shape
    bm = mod.shape[0]
    mod_map = (lambda i, j: (i, 0, 0)) if bm > 1 else (lambda i, j: (0, 0, 0))

    def const(a):
        return pl.BlockSpec(a.shape, lambda i, j: (0, 0), pipeline_mode=pl.Buffered(1))

    return pl.pallas_call(
        functools.partial(_post_kernel, ff_chunk=ff_chunk, sub_rows=POST_SUB_ROWS),
        grid=(b, s // tm),
        in_specs=[pl.BlockSpec((1, tm, m.shape[-1]), lambda i, j: (i, j, 0)),
                  pl.BlockSpec((1, tm, d), lambda i, j: (i, j, 0)),
                  pl.BlockSpec((1, 1, mod.shape[-1]), mod_map),
                  pl.BlockSpec(gains.shape, lambda i, j: (0, 0)),
                  const(wo), const(w1), const(w2)],
        out_specs=pl.BlockSpec((1, tm, d), lambda i, j: (i, j, 0)),
        out_shape=jax.ShapeDtypeStruct((b, s, d), F32),
        compiler_params=_compiler_params(2),
        name="post",
    )(m, x, mod, gains, wo, w1, w2)


def _rope_tables(t):
    tok = jnp.arange(t, dtype=jnp.int32)
    pos = jnp.stack([tok // GRID_W, tok % GRID_W], axis=-1).astype(F32)
    inv = ROPE_THETA ** (-jnp.arange(ROPE_FREQS, dtype=F32) / ROPE_FREQS)
    ang = pos[..., None] * inv
    cos, sin = jnp.cos(ang), jnp.sin(ang)
    rope_t = jnp.stack([cos.reshape(t, -1).T, sin.reshape(t, -1).T])
    zero = jnp.zeros_like(sin[:, 0])
    c = jnp.concatenate([cos[:, 0], cos[:, 0], cos[:, 1], cos[:, 1]], axis=-1)
    up = jnp.concatenate([-sin[:, 0], zero, -sin[:, 1], zero], axis=-1)
    down = jnp.concatenate([zero, sin[:, 0], zero, sin[:, 1]], axis=-1)
    rope_n = jnp.stack([jnp.tile(a, (1, PAIR // HEAD_DIM)) for a in (c, up, down)])
    return rope_t, rope_n


def _pair_gqa_heads(w, axis):
    shape = w.shape
    w = w.reshape(shape[:axis] + (2, N_SLOTS // 2, HEAD_DIM) + shape[axis + 1:])
    return jnp.swapaxes(w, axis, axis + 1).reshape(shape)


def _lambda_init(layer):
    return 0.8 - 0.6 * math.exp(-0.3 * layer)


def kernel(x, c, ctx, c_ctx, w_mod, b_mod, norm_g, w_in, w_out, w_mlp_in, w_mlp_out, qk_norm_a, diff_lambda,
           diff_subln, sink_c, rpb_d):
    b, t, d = x.shape
    n_ctx = ctx.shape[1]
    depth = w_mod.shape[0]
    mod_rows = 16
    cc = jnp.concatenate([c, c_ctx[None], jnp.zeros((mod_rows - b - 1, d), F32)], axis=0)
    mod_all = _modulation(cc, w_mod, b_mod)

    rope_args = _rope_tables(t)
    seg = jnp.kron(jnp.eye(2, dtype=F32), jnp.ones((HEAD_DIM, HEAD_DIM), F32)).astype(BF16)
    tm_x = 2 * IN_SUB_ROWS

    for l in range(depth):
        last = l == depth - 1
        even = l % 2 == 0
        i = l // 2
        mod_x = mod_all[l, :b][:, None, :]
        mod_c = mod_all[l, b][None, None, :]
        w = w_in[l]
        o_k1, o_v1, o_q2, o_k2, o_v2 = Q_WIDTH, Q_WIDTH + PAIR, Q_WIDTH + 2 * PAIR, 2 * Q_WIDTH + 2 * PAIR, \
            3 * Q_WIDTH + 2 * PAIR
        wt = jnp.concatenate([_pair_gqa_heads(w[:, 0:o_k1], 1), w[:, o_v1:o_q2], w[:, o_q2:o_k2], w[:, o_v2:]],
                             axis=1).T.astype(BF16)
        wk = jnp.concatenate([w[:, o_k1:o_v1], w[:, o_k2:o_v2]], axis=1).astype(BF16)
        wo = jnp.concatenate([_pair_gqa_heads(w_out[l, 0:Q_WIDTH], 0), w_out[l, Q_WIDTH:]], axis=0).astype(BF16)
        w1 = w_mlp_in[l].astype(BF16)
        w2 = w_mlp_out[l].astype(BF16)
        g0 = norm_g[l, 0][None, :]

        def norm_args(tm):
            if not even:
                return None
            return (jnp.broadcast_to(qk_norm_a[i, 0][:, None], (HEAD_DIM, tm)),
                    jnp.tile(qk_norm_a[i, 1], PAIR // HEAD_DIM)[None, :], seg)

        q1x, k1x, v1x, q2x, k2x, v2x = _in_proj(x, mod_x, g0, wt, wk, norm_args(tm_x), rope_args,
                                                even=even, tm=tm_x)
        q1c, k1c, v1c, q2c, k2c, v2c = _in_proj(ctx, mod_c, g0, wt, wk, norm_args(n_ctx), None,
                                                even=even, tm=n_ctx)
        x_kv = (k1x, v1x, k2x, v2x)
        ctx_kv = (k1c, v1c, k2c, v2c)
        if even:
            lam0 = _lambda_init(l)
            sub = diff_subln[i][None, :]
            m_x = _attn_even((q1x, q2x), ctx_kv, x_kv, diff_lambda[i], sub, lam0=lam0, tq=256)
            if not last:
                m_c = _attn_even((q1c, q2c), ctx_kv, None, diff_lambda[i], sub, lam0=lam0, tq=256)
        else:
            if not last:
                raise NotImplementedError("context update after an odd layer is not needed for depth 2")
            tp = _bias_table(rpb_d[i])
            m_x = _attn_odd((q1x, q2x), x_kv, ctx_kv, sink_c[i], tp, tq=256, sink_order=GQA_SLOT_HEADS)
        x = _post(m_x, x, mod_x, norm_g[l], wo, w1, w2, tm=512, ff_chunk=1024)
        if not last:
            ctx = _post(m_c, ctx, mod_c, norm_g[l], wo, w1, w2, tm=n_ctx, ff_chunk=1024)
    return x
```

```python
import functools
import math
from typing import Any, Callable, NamedTuple, Sequence

import numpy as np
import jax
import jax.numpy as jnp
from jax import lax
from jax.experimental import pallas as pl
from jax.experimental.pallas import tpu as pltpu

F32 = jnp.float32
BF16 = jnp.bfloat16

GRID_W = 64
HEAD_DIM = 64
ROPE_THETA = 10000.0
ROPE_FREQS = HEAD_DIM // 4
N_SLOTS = 8
PAIR = 2 * HEAD_DIM
Q_WIDTH = N_SLOTS * HEAD_DIM
LANES = 128
MXU_QUERY_COLS = 512
EVEN_KEY_CHUNK = 1024
IN_SUB_ROWS = 512
ATTEND_EXTRA_LEAD = 2
POST_SUB_ROWS = 256
NA_INVALID_COLUMN = 1 << 20
NA_CHUNK_ROWS = 4
C_WINDOW = 128
NA_ROWS = 8
NA_COLS = 16
EPS = 1e-6
NEG = -1e30
LOG2E = math.log2(math.e)
Q_SCALE = HEAD_DIM ** -0.5 * LOG2E
GQA_SLOT_HEADS = tuple(half * (N_SLOTS // 2) + p for p in range(N_SLOTS // 2) for half in range(2))

VMEM_LIMIT_BYTES = 56 * 1024 * 1024
ATTN_SCHEDULER_FLAGS = None


def _compiler_params(n_grid_dims, flags=None):
    return pltpu.CompilerParams(dimension_semantics=("arbitrary",) * n_grid_dims,
                                vmem_limit_bytes=VMEM_LIMIT_BYTES, flags=flags)


def _dot(a, b):
    return jnp.dot(a, b, preferred_element_type=F32)


def _dot_nt(a, b):
    return lax.dot_general(a, b, (((1,), (1,)), ((), ())), preferred_element_type=F32)


def _log2(n):
    assert n & (n - 1) == 0
    return n.bit_length() - 1


def _rms(x, gain):
    ms = jnp.mean(x * x, axis=-1, keepdims=True)
    return x * lax.rsqrt(ms + EPS) * gain


def _mod_kernel(c_ref, w_ref, b_ref, o_ref):
    c = c_ref[...]
    s = c * jax.nn.sigmoid(c)
    o_ref[0] = _dot(s.astype(BF16), w_ref[0].astype(BF16)) + b_ref[0]


def _modulation(cc, w_mod, b_mod):
    depth, d, n = w_mod.shape
    rows = cc.shape[0]
    tn = n // 4
    return pl.pallas_call(
        _mod_kernel,
        grid=(depth, n // tn),
        in_specs=[pl.BlockSpec((rows, d), lambda l, j: (0, 0)),
                  pl.BlockSpec((1, d, tn), lambda l, j: (l, 0, j)),
                  pl.BlockSpec((1, 1, tn), lambda l, j: (l, 0, j))],
        out_specs=pl.BlockSpec((1, rows, tn), lambda l, j: (l, 0, j)),
        out_shape=jax.ShapeDtypeStruct((depth, rows, n), F32),
        compiler_params=_compiler_params(2),
        name="modulation",
    )(cc, w_mod, b_mod.reshape(depth, 1, n))


def _in_kernel(*refs, even, rope, sub_rows):
    refs = list(refs)
    tok_ref, mod_ref, g_ref, wt_ref, wk_ref = refs[:5]
    del refs[:5]
    if even:
        gq_ref, gk_ref, seg_ref = refs[:3]
        del refs[:3]
    if rope:
        rope_t_ref, rope_ref = refs[:2]
        del refs[:2]
    q1_ref, k1_ref, v1_ref, q2_ref, k2_ref, v2_ref = refs
    d = tok_ref.shape[-1]
    tm = tok_ref.shape[1]
    sub = min(sub_rows, tm)

    def normed(rows):
        h = _rms(tok_ref[0, rows], g_ref[...])
        h = h * (1.0 + mod_ref[0, :, d:2 * d]) + mod_ref[0, :, 0:d]
        return h.astype(BF16)

    def project(rows, hb):
        def rotary_t(c):
            f = ROPE_FREQS
            out = []
            for a in range(2):
                x1, x2 = c[2 * a * f:(2 * a + 1) * f], c[(2 * a + 1) * f:(2 * a + 2) * f]
                cos, sin = rope_t_ref[0, a * f:(a + 1) * f, rows], rope_t_ref[1, a * f:(a + 1) * f, rows]
                out += [x1 * cos - x2 * sin, x2 * cos + x1 * sin]
            return jnp.concatenate(out, axis=0)

        def store_slots(q_ref, qt, use_norm, use_rope):
            zeros = jnp.zeros((HEAD_DIM, sub), F32)
            for j in range(N_SLOTS):
                c = qt[j * HEAD_DIM:(j + 1) * HEAD_DIM]
                if use_norm:
                    ms = jnp.mean(c * c, axis=0, keepdims=True)
                    c = c * lax.rsqrt(ms + EPS) * gq_ref[:, 0:sub]
                if use_rope:
                    c = rotary_t(c)
                c = c * Q_SCALE
                slot = jnp.concatenate([c, zeros] if j % 2 == 0 else [zeros, c], axis=0)
                q_ref[0, j, :, rows] = slot.astype(BF16)

        def store_vt(v_ref, vt):
            if even:
                v_ref[0, :, rows] = vt.astype(BF16)
            else:
                for c in range(sub // LANES):
                    v_ref[0, rows.start // LANES + c] = vt[:, c * LANES:(c + 1) * LANES].astype(BF16)

        def rotary(c):
            up = pltpu.roll(c, PAIR - ROPE_FREQS, axis=1)
            down = pltpu.roll(c, ROPE_FREQS, axis=1)
            return c * rope_ref[0, rows] + up * rope_ref[1, rows] + down * rope_ref[2, rows]

        o = 0
        store_slots(q1_ref, _dot_nt(wt_ref[o:o + Q_WIDTH], hb), even, rope)
        o += Q_WIDTH
        store_vt(v1_ref, _dot_nt(wt_ref[o:o + PAIR], hb))
        o += PAIR
        store_slots(q2_ref, _dot_nt(wt_ref[o:o + Q_WIDTH], hb), False, rope and even)
        o += Q_WIDTH
        store_vt(v2_ref, _dot_nt(wt_ref[o:o + Q_WIDTH], hb))

        k = _dot(hb, wk_ref[...])
        k1 = k[:, 0:PAIR]
        if even:
            sq = k1 * k1
            hi = sq.astype(BF16)
            lo = (sq - hi.astype(F32)).astype(BF16)
            ms = (_dot(hi, seg_ref[...]) + _dot(lo, seg_ref[...])) * (1.0 / HEAD_DIM)
            k1 = k1 * lax.rsqrt(ms + EPS) * gk_ref[...]
        if rope:
            k1 = rotary(k1)
        k1_ref[0, rows] = k1.astype(BF16)
        for j in range(Q_WIDTH // PAIR):
            c = k[:, PAIR + j * PAIR:PAIR + (j + 1) * PAIR]
            if rope and even:
                c = rotary(c)
            k2_ref[0, rows, j * PAIR:(j + 1) * PAIR] = c.astype(BF16)

    blocks = [slice(r, r + sub) for r in range(0, tm, sub)]
    hbs = [normed(rows) for rows in blocks]
    for rows, hb in zip(blocks, hbs):
        project(rows, hb)


def _in_proj(tok, mod, gain, wt, wk, norm_args, rope_args, *, even, tm):
    b, s, d = tok.shape
    bm = mod.shape[0]
    rope = rope_args is not None
    mod_map = (lambda i, j: (i, 0, 0)) if bm > 1 else (lambda i, j: (0, 0, 0))

    def const(a):
        return pl.BlockSpec(a.shape, lambda i, j: (0,) * a.ndim)

    in_specs = [pl.BlockSpec((1, tm, d), lambda i, j: (i, j, 0)),
                pl.BlockSpec((1, 1, mod.shape[-1]), mod_map),
                const(gain), const(wt), const(wk)]
    args = [tok, mod, gain, wt, wk]
    if even:
        gq, gk, seg = norm_args
        in_specs += [const(gq), const(gk), const(seg)]
        args += [gq, gk, seg]
    if rope:
        rope_t, rope_n = rope_args
        in_specs += [pl.BlockSpec((2, 2 * ROPE_FREQS, tm), lambda i, j: (0, 0, j)),
                     pl.BlockSpec((3, tm, PAIR), lambda i, j: (0, j, 0))]
        args += [rope_t, rope_n]

    def vt_out(rows):
        if even:
            return (pl.BlockSpec((1, rows, tm), lambda i, j: (i, 0, j)),
                    jax.ShapeDtypeStruct((b, rows, s), BF16))
        return (pl.BlockSpec((1, tm // LANES, rows, LANES), lambda i, j: (i, j, 0, 0)),
                jax.ShapeDtypeStruct((b, s // LANES, rows, LANES), BF16))

    slot_out = (pl.BlockSpec((1, N_SLOTS, PAIR, tm), lambda i, j: (i, 0, 0, j)),
                jax.ShapeDtypeStruct((b, N_SLOTS, PAIR, s), BF16))

    def nat_out(cols):
        return (pl.BlockSpec((1, tm, cols), lambda i, j: (i, j, 0)), jax.ShapeDtypeStruct((b, s, cols), BF16))

    outs = [slot_out, nat_out(PAIR), vt_out(PAIR), slot_out, nat_out(Q_WIDTH), vt_out(Q_WIDTH)]
    return pl.pallas_call(
        functools.partial(_in_kernel, even=even, rope=rope, sub_rows=IN_SUB_ROWS),
        grid=(b, s // tm),
        in_specs=in_specs,
        out_specs=[o[0] for o in outs],
        out_shape=[o[1] for o in outs],
        compiler_params=_compiler_params(2),
        name="in_proj_" + ("even" if even else "odd") + ("_x" if rope else "_ctx"),
    )(*args)


class AttnJob(NamedTuple):
    query: Callable[[], Any]
    scores: Sequence[Callable[[Any], Any]]
    values: Sequence[Callable[[], Any]]
    finish: Callable[[Any], None]
    sink: Any = None


def _attend_lead(jobs):
    return max(len(job.scores) for job in jobs) + ATTEND_EXTRA_LEAD


def _attend_jobs(jobs, s_ref):
    items = [(j, c) for j, job in enumerate(jobs) for c in range(len(job.scores))]
    lead = _attend_lead(jobs)
    n_slots = s_ref.shape[0]
    assert n_slots > lead
    qt, shape, m, l, o = {}, {}, {}, {}, {}
    for t in range(len(items) + lead):
        if t < len(items):
            j, c = items[t]
            job = jobs[j]
            if c == 0:
                qt[j] = job.query()
            x = job.scores[c](qt[j])
            shape[t] = x.shape
            s_ref[t % n_slots, 0:x.shape[0], 0:x.shape[1]] = x
            part_m = jnp.max(x, axis=0, keepdims=True)
            if c == 0:
                m[j] = part_m if job.sink is None else jnp.maximum(part_m, job.sink)
            else:
                m[j] = jnp.maximum(m[j], part_m)
        if t >= lead:
            j, c = items[t - lead]
            job = jobs[j]
            rows, cols = shape.pop(t - lead)
            e = jnp.exp2(s_ref[(t - lead) % n_slots, 0:rows, 0:cols] - m[j])
            part_l = jnp.sum(e, axis=0, keepdims=True)
            part_o = _dot(job.values[c](), e.astype(BF16))
            l[j] = part_l if c == 0 else l[j] + part_l
            o[j] = part_o if c == 0 else o[j] + part_o
            if c == len(job.scores) - 1:
                denom = l.pop(j) if job.sink is None else l.pop(j) + jnp.exp2(job.sink - m[j])
                job.finish(o.pop(j) * (1.0 / denom))


def _attn_even_kernel(*refs, with_x, lam0, tq, key_chunk):
    if with_x:
        q1_ref, q2_ref, k1c_ref, v1c_ref, k2c_ref, v2c_ref, k1x_ref, v1x_ref, k2x_ref, v2x_ref = refs[:10]
    else:
        q1_ref, q2_ref, k1c_ref, v1c_ref, k2c_ref, v2c_ref = refs[:6]
    lam_ref, sub_ref, o_ref, s_ref = refs[-4:]

    def chunk_fns(c_ref, x_ref, cols, transposed):
        def make(ref, lo, hi):
            if transposed:
                return lambda: ref[0, cols, lo:hi]
            return lambda qt: _dot(ref[0, lo:hi, cols], qt)
        n_ctx = c_ref.shape[2] if transposed else c_ref.shape[1]
        fns = [make(c_ref, 0, n_ctx)]
        if with_x:
            n_x = x_ref.shape[2] if transposed else x_ref.shape[1]
            fns += [make(x_ref, lo, lo + key_chunk) for lo in range(0, n_x, key_chunk)]
        return fns

    group = max(2, MXU_QUERY_COLS // tq)
    jobs = []
    pair_cols = slice(0, PAIR)

    def a_query(g):
        return jnp.concatenate([q1_ref[0, g * group + j] for j in range(group)], axis=1)

    def a_finish(g, o):
        for pp in range(group // 2):
            pair_t = jnp.concatenate([o[0:HEAD_DIM, (2 * pp) * tq:(2 * pp + 1) * tq],
                                      o[HEAD_DIM:PAIR, (2 * pp + 1) * tq:(2 * pp + 2) * tq]], axis=0)
            col = (g * (group // 2) + pp) * PAIR
            o_ref[0, :, col:col + PAIR] = pair_t.T.astype(BF16)

    for g in range(N_SLOTS // group):
        jobs.append(AttnJob(functools.partial(a_query, g),
                            chunk_fns(k1c_ref, k1x_ref if with_x else None, pair_cols, False),
                            chunk_fns(v1c_ref, v1x_ref if with_x else None, pair_cols, True),
                            functools.partial(a_finish, g)))

    def b_query(h):
        return jnp.concatenate([q2_ref[0, 2 * h], q2_ref[0, 2 * h + 1]], axis=1)

    def b_finish(h, o):
        lp = lam_ref[...]
        lam = (jnp.exp(jnp.sum(lp[0:1] * lp[1:2], axis=-1, keepdims=True))
               - jnp.exp(jnp.sum(lp[2:3] * lp[3:4], axis=-1, keepdims=True)) + lam0)
        ob = (o[:, 0:tq] - lam * o[:, tq:2 * tq]).T
        ob = _rms(ob, sub_ref[...]) * (1.0 - lam0)
        col = Q_WIDTH + h * PAIR
        o_ref[0, :, col:col + PAIR] = ob.astype(BF16)

    for h in range(N_SLOTS // 2):
        cols = slice(h * PAIR, (h + 1) * PAIR)
        jobs.append(AttnJob(functools.partial(b_query, h),
                            chunk_fns(k2c_ref, k2x_ref if with_x else None, cols, False),
                            chunk_fns(v2c_ref, v2x_ref if with_x else None, cols, True),
                            functools.partial(b_finish, h)))
    _attend_jobs(jobs, s_ref)


def _attn_even(qx, ctx_kv, x_kv, lam_p, subln, *, lam0, tq):
    q1, q2 = qx
    b, _, _, s = q1.shape
    with_x = x_kv is not None
    slot_spec = pl.BlockSpec((1, N_SLOTS, PAIR, tq), lambda i, j: (i, 0, 0, j))

    def full(a):
        return pl.BlockSpec((1,) + a.shape[1:], lambda i, j: (i, 0, 0))

    kv = list(ctx_kv) + (list(x_kv) if with_x else [])
    n_ctx = ctx_kv[0].shape[1]
    key_chunk = min(EVEN_KEY_CHUNK, x_kv[0].shape[1]) if with_x else n_ctx
    chunks_per_job = 1 + (x_kv[0].shape[1] // key_chunk if with_x else 0)
    score_ring = pltpu.VMEM((chunks_per_job + ATTEND_EXTRA_LEAD + 1, max(n_ctx, key_chunk),
                             max(2 * tq, MXU_QUERY_COLS)), F32)
    return pl.pallas_call(
        functools.partial(_attn_even_kernel, with_x=with_x, lam0=lam0, tq=tq, key_chunk=key_chunk),
        grid=(b, s // tq),
        scratch_shapes=[score_ring],
        in_specs=[slot_spec, slot_spec] + [full(a) for a in kv]
                 + [pl.BlockSpec(lam_p.shape, lambda i, j: (0, 0)), pl.BlockSpec(subln.shape, lambda i, j: (0, 0))],
        out_specs=pl.BlockSpec((1, tq, 2 * Q_WIDTH), lambda i, j: (i, j, 0)),
        out_shape=jax.ShapeDtypeStruct((b, s, 2 * Q_WIDTH), BF16),
        compiler_params=_compiler_params(2, ATTN_SCHEDULER_FLAGS),
        name="attn_even" + ("_x" if with_x else "_ctx"),
    )(q1, q2, *kv, lam_p, subln)


N_BIAS_ROWS = 2 * NA_ROWS


def _bias_kernel(rpb_ref, o_ref):
    head = pl.program_id(0)
    n_dr = 2 * NA_ROWS - 1
    n_dc = 2 * NA_COLS - 1
    shape = (GRID_W, LANES)
    ck = lax.broadcasted_iota(jnp.int32, shape, 0)
    lane = lax.broadcasted_iota(jnp.int32, shape, 1)
    second = lane >= GRID_W
    dcl = jnp.clip(ck - (lane & (GRID_W - 1)), -(NA_COLS - 1), NA_COLS - 1) + (NA_COLS - 1)
    for t in range(N_BIAS_ROWS):
        acc = jnp.zeros(shape, F32)
        for j in range(n_dc):
            first_v = rpb_ref[(head * n_dr + min(t, n_dr - 1)) * n_dc + j]
            second_v = rpb_ref[(head * n_dr + max(t - 1, 0)) * n_dc + j]
            acc = jnp.where(dcl == j, jnp.where(second, second_v, first_v), acc)
        o_ref[0, t] = acc * LOG2E


def _bias_table(rpb):
    heads = rpb.shape[0]
    return pl.pallas_call(
        _bias_kernel,
        grid=(heads,),
        in_specs=[pl.BlockSpec(memory_space=pltpu.SMEM)],
        out_specs=pl.BlockSpec((1, N_BIAS_ROWS, GRID_W, LANES), lambda h: (h, 0, 0, 0)),
        out_shape=jax.ShapeDtypeStruct((heads, N_BIAS_ROWS, GRID_W, LANES), F32),
        compiler_params=_compiler_params(1),
        name="rel_bias_table",
    )(rpb.reshape(-1))


def _attn_odd_kernel(q1_ref, q2_ref, k1x_ref, v1x_ref, k2x_ref, v2x_ref, k1c_ref, v1c_ref, k2c_ref, v2c_ref,
                     sink_ref, tp_ref, win_ref, o_ref, s_ref, *, tq, sink_order):
    i = pl.program_id(1)
    t = k1x_ref.shape[1]
    n_ctx_blk = v1c_ref.shape[1]
    rows = t // GRID_W
    kh = min(NA_ROWS, rows)
    n_blk = tq // LANES

    jobs = []

    n_q = N_SLOTS * C_WINDOW
    n_seq_blk = t // C_WINDOW
    lane_q = lax.broadcasted_iota(jnp.int32, (1, n_q), 1)
    sink = jnp.zeros((1, n_q), F32)
    for j in range(N_SLOTS):
        sink = jnp.where(lane_q >> _log2(C_WINDOW) == j, sink_ref[sink_order[j]], sink)
    sink = sink * LOG2E
    tri = (lax.broadcasted_iota(jnp.int32, (C_WINDOW, n_q), 0)
           - (lax.broadcasted_iota(jnp.int32, (C_WINDOW, n_q), 1) & (C_WINDOW - 1)))

    def c_blocks(jb):
        blk = i * n_blk + jb
        return blk, jnp.maximum(blk - 1, 0), jnp.minimum(blk + 1, n_seq_blk - 1)

    def c_query(jb):
        return jnp.concatenate([q1_ref[0, j, :, jb * LANES:(jb + 1) * LANES] for j in range(N_SLOTS)], axis=1)

    def c_local_scores(jb, qt):
        blk, prev, nxt = c_blocks(jb)
        k_loc = jnp.concatenate([k1x_ref[0, pl.ds(pl.multiple_of(n * C_WINDOW, C_WINDOW), C_WINDOW), :]
                                 for n in (prev, blk, nxt)], axis=0)
        s = _dot(k_loc, qt)
        lo_bound = jnp.where(blk > 0, 0, C_WINDOW)
        hi_bound = jnp.where(blk < n_seq_blk - 1, 0, -C_WINDOW)
        return jnp.concatenate([jnp.where(tri >= lo_bound, s[0:C_WINDOW], NEG),
                                s[C_WINDOW:2 * C_WINDOW],
                                jnp.where(tri <= hi_bound, s[2 * C_WINDOW:3 * C_WINDOW], NEG)], axis=0)

    def c_local_values(jb):
        blk, prev, nxt = c_blocks(jb)
        return jnp.concatenate([v1x_ref[0, n] for n in (prev, blk, nxt)], axis=1)

    def c_finish(jb, o):
        for p in range(N_SLOTS // 2):
            pair_t = jnp.concatenate([o[0:HEAD_DIM, (2 * p) * LANES:(2 * p + 1) * LANES],
                                      o[HEAD_DIM:PAIR, (2 * p + 1) * LANES:(2 * p + 2) * LANES]], axis=0)
            o_ref[0, jb * LANES:(jb + 1) * LANES, p * PAIR:(p + 1) * PAIR] = pair_t.T.astype(BF16)

    for jb in range(n_blk):
        jobs.append(AttnJob(
            functools.partial(c_query, jb),
            [functools.partial(c_local_scores, jb), lambda qt: _dot(k1c_ref[0], qt)],
            [functools.partial(c_local_values, jb),
             lambda: jnp.concatenate([v1c_ref[0, c] for c in range(n_ctx_blk)], axis=1)],
            functools.partial(c_finish, jb), sink))

    n_rows = tq // GRID_W
    span_rows = -(-(n_rows - 1 + kh) // 2) * 2
    n_loc = span_rows * GRID_W
    r0 = i * n_rows
    u = jnp.minimum(jnp.clip(r0 - kh // 2, 0, rows - kh), rows - span_rows)
    q_row = (lax.broadcasted_iota(jnp.int32, (1, 2 * tq), 1) >> _log2(GRID_W)) & (n_rows - 1)
    first = jnp.zeros((1, 2 * tq), jnp.int32)
    for j in range(n_rows):
        first = jnp.where(q_row == j, jnp.clip(r0 + j - kh // 2, 0, rows - kh) - u, first)
    window_row = win_ref[...] - first

    def valid(c):
        w = window_row[c * NA_CHUNK_ROWS * GRID_W:(c + 1) * NA_CHUNK_ROWS * GRID_W]
        return lax.bitcast_convert_type(w, jnp.uint32) < jnp.uint32(kh)

    chunk_rows = NA_CHUNK_ROWS
    chunk_keys = chunk_rows * GRID_W
    blk_rows = LANES // GRID_W

    def d_query(p):
        return jnp.concatenate([q2_ref[0, 2 * p], q2_ref[0, 2 * p + 1]], axis=1)

    def d_local_scores(p, c, qt):
        k_loc = k2x_ref[0, pl.ds(pl.multiple_of((u + c * chunk_rows) * GRID_W, LANES), chunk_keys),
                        p * PAIR:(p + 1) * PAIR]
        bias = jnp.concatenate(
            [jnp.concatenate(
                [tp_ref[2 * p + lb // (n_rows // 2),
                        jnp.clip(u + a - (r0 + 2 * (lb % (n_rows // 2))) + NA_ROWS - 1, 0, N_BIAS_ROWS - 1)]
                 for lb in range(n_rows)], axis=1)
             for a in range(c * chunk_rows, (c + 1) * chunk_rows)], axis=0)
        return jnp.where(valid(c), _dot(k_loc, qt) + bias, NEG)

    def d_local_values(p, c):
        blk0 = (u + c * chunk_rows) // blk_rows
        return jnp.concatenate([v2x_ref[0, blk0 + b, p * PAIR:(p + 1) * PAIR, :]
                                for b in range(chunk_rows // blk_rows)], axis=1)

    def d_finish(p, o):
        pair_t = jnp.concatenate([o[0:HEAD_DIM, 0:tq], o[HEAD_DIM:PAIR, tq:2 * tq]], axis=0)
        o_ref[0, :, Q_WIDTH + p * PAIR:Q_WIDTH + (p + 1) * PAIR] = pair_t.T.astype(BF16)

    for p in range(N_SLOTS // 2):
        cols = slice(p * PAIR, (p + 1) * PAIR)
        n_chunks = span_rows // chunk_rows
        jobs.append(AttnJob(
            functools.partial(d_query, p),
            [functools.partial(d_local_scores, p, c) for c in range(n_chunks)]
            + [lambda qt, cols=cols: _dot(k2c_ref[0, :, cols], qt)],
            [functools.partial(d_local_values, p, c) for c in range(n_chunks)]
            + [lambda cols=cols: jnp.concatenate([v2c_ref[0, b, cols, :] for b in range(n_ctx_blk)], axis=1)],
            functools.partial(d_finish, p)))
    _attend_jobs(jobs, s_ref)


def _attn_odd(qx, x_kv, ctx_kv, sink, tp, *, tq, sink_order):
    q1, q2 = qx
    b, _, _, s = q1.shape
    slot_spec = pl.BlockSpec((1, N_SLOTS, PAIR, tq), lambda i, j: (i, 0, 0, j))

    def full(a):
        return pl.BlockSpec((1,) + a.shape[1:], lambda i, j: (i,) + (0,) * (a.ndim - 1))

    kv = list(x_kv) + list(ctx_kv)
    n_ctx = ctx_kv[0].shape[1]
    rows = s // GRID_W
    span_rows = -(-(tq // GRID_W - 1 + min(NA_ROWS, rows)) // 2) * 2
    chunks_per_job = max(2, span_rows // NA_CHUNK_ROWS + 1)
    key_i = np.arange(span_rows * GRID_W)[:, None]
    cq = np.arange(2 * tq)[None, :] % GRID_W
    cs = np.clip(cq - NA_COLS // 2, 0, GRID_W - NA_COLS)
    ck = key_i % GRID_W
    col_ok = (ck >= cs) & (ck < cs + NA_COLS)
    window = jnp.asarray(key_i // GRID_W + np.where(col_ok, 0, NA_INVALID_COLUMN), jnp.int32)
    score_ring = pltpu.VMEM((chunks_per_job + ATTEND_EXTRA_LEAD + 1, max(3 * C_WINDOW, NA_CHUNK_ROWS * GRID_W, n_ctx),
                             max(N_SLOTS * C_WINDOW, 2 * tq)), F32)
    return pl.pallas_call(
        functools.partial(_attn_odd_kernel, tq=tq, sink_order=sink_order),
        grid=(b, s // tq),
        scratch_shapes=[score_ring],
        in_specs=[slot_spec, slot_spec] + [full(a) for a in kv]
                 + [pl.BlockSpec(memory_space=pltpu.SMEM), pl.BlockSpec(tp.shape, lambda i, j: (0, 0, 0, 0)),
                    pl.BlockSpec(window.shape, lambda i, j: (0, 0))],
        out_specs=pl.BlockSpec((1, tq, 2 * Q_WIDTH), lambda i, j: (i, j, 0)),
        out_shape=jax.ShapeDtypeStruct((b, s, 2 * Q_WIDTH), BF16),
        compiler_params=_compiler_params(2),
        name="attn_odd",
    )(q1, q2, *kv, sink, tp, window)


def _post_kernel(m_ref, x_ref, mod_ref, g_ref, wo_ref, w1_ref, w2_ref, o_ref, *, ff_chunk, sub_rows):
    d = x_ref.shape[-1]

    def mod(k):
        return mod_ref[0, :, k * d:(k + 1) * d]

    def mixer_residual(rows):
        x1 = x_ref[0, rows] + mod(2) * _rms(_dot(m_ref[0, rows], wo_ref[...]), g_ref[1:2])
        return x1, (_rms(x1, g_ref[2:3]) * (1.0 + mod(4)) + mod(3)).astype(BF16)

    def mlp(h):
        f = None
        for c in range(w1_ref.shape[1] // ff_chunk):
            cols = slice(c * ff_chunk, (c + 1) * ff_chunk)
            a = jnp.maximum(_dot(h, w1_ref[:, cols]), 0.0)
            part = _dot((a * a).astype(BF16), w2_ref[cols, :])
            f = part if f is None else f + part
        return f

    tm = x_ref.shape[1]
    sub = min(sub_rows, tm)
    blocks = [slice(r, r + sub) for r in range(0, tm, sub)]
    stage1 = [mixer_residual(rows) for rows in blocks]
    stage2 = [mlp(h) for _, h in stage1]
    for rows, (x1, _), f in zip(blocks, stage1, stage2):
        o_ref[0, rows] = x1 + mod(5) * _rms(f, g_ref[3:4])


def _post(m, x, mod, gains, wo, w1, w2, *, tm, ff_chunk):
    b, s, d = x.shape
    bm = mod.shape[0]
    mod_map = (lambda i, j: (i, 0, 0)) if bm > 1 else (lambda i, j: (0, 0, 0))

    def const(a):
        return pl.BlockSpec(a.shape, lambda i, j: (0, 0), pipeline_mode=pl.Buffered(1))

    return pl.pallas_call(
        functools.partial(_post_kernel, ff_chunk=ff_chunk, sub_rows=POST_SUB_ROWS),
        grid=(b, s // tm),
        in_specs=[pl.BlockSpec((1, tm, m.shape[-1]), lambda i, j: (i, j, 0)),
                  pl.BlockSpec((1, tm, d), lambda i, j: (i, j, 0)),
                  pl.BlockSpec((1, 1, mod.shape[-1]), mod_map),
                  pl.BlockSpec(gains.shape, lambda i, j: (0, 0)),
                  const(wo), const(w1), const(w2)],
        out_specs=pl.BlockSpec((1, tm, d), lambda i, j: (i, j, 0)),
        out_shape=jax.ShapeDtypeStruct((b, s, d), F32),
        compiler_params=_compiler_params(2),
        name="post",
    )(m, x, mod, gains, wo, w1, w2)


def _rope_tables(t):
    tok = jnp.arange(t, dtype=jnp.int32)
    pos = jnp.stack([tok // GRID_W, tok % GRID_W], axis=-1).astype(F32)
    inv = ROPE_THETA ** (-jnp.arange(ROPE_FREQS, dtype=F32) / ROPE_FREQS)
    ang = pos[..., None] * inv
    cos, sin = jnp.cos(ang), jnp.sin(ang)
    rope_t = jnp.stack([cos.reshape(t, -1).T, sin.reshape(t, -1).T])
    zero = jnp.zeros_like(sin[:, 0])
    c = jnp.concatenate([cos[:, 0], cos[:, 0], cos[:, 1], cos[:, 1]], axis=-1)
    up = jnp.concatenate([-sin[:, 0], zero, -sin[:, 1], zero], axis=-1)
    down = jnp.concatenate([zero, sin[:, 0], zero, sin[:, 1]], axis=-1)
    rope_n = jnp.stack([jnp.tile(a, (1, PAIR // HEAD_DIM)) for a in (c, up, down)])
    return rope_t, rope_n


def _pair_gqa_heads(w, axis):
    shape = w.shape
    w = w.reshape(shape[:axis] + (2, N_SLOTS // 2, HEAD_DIM) + shape[axis + 1:])
    return jnp.swapaxes(w, axis, axis + 1).reshape(shape)


def _lambda_init(layer):
    return 0.8 - 0.6 * math.exp(-0.3 * layer)


def kernel(x, c, ctx, c_ctx, w_mod, b_mod, norm_g, w_in, w_out, w_mlp_in, w_mlp_out, qk_norm_a, diff_lambda,
           diff_subln, sink_c, rpb_d):
    b, t, d = x.shape
    n_ctx = ctx.shape[1]
    depth = w_mod.shape[0]
    mod_rows = 16
    cc = jnp.concatenate([c, c_ctx[None], jnp.zeros((mod_rows - b - 1, d), F32)], axis=0)
    mod_all = _modulation(cc, w_mod, b_mod)

    rope_args = _rope_tables(t)
    seg = jnp.kron(jnp.eye(2, dtype=F32), jnp.ones((HEAD_DIM, HEAD_DIM), F32)).astype(BF16)
    tm_x = 2 * IN_SUB_ROWS

    for l in range(depth):
        last = l == depth - 1
        even = l % 2 == 0
        i = l // 2
        mod_x = mod_all[l, :b][:, None, :]
        mod_c = mod_all[l, b][None, None, :]
        w = w_in[l]
        o_k1, o_v1, o_q2, o_k2, o_v2 = Q_WIDTH, Q_WIDTH + PAIR, Q_WIDTH + 2 * PAIR, 2 * Q_WIDTH + 2 * PAIR, \
            3 * Q_WIDTH + 2 * PAIR
        wt = jnp.concatenate([_pair_gqa_heads(w[:, 0:o_k1], 1), w[:, o_v1:o_q2], w[:, o_q2:o_k2], w[:, o_v2:]],
                             axis=1).T.astype(BF16)
        wk = jnp.concatenate([w[:, o_k1:o_v1], w[:, o_k2:o_v2]], axis=1).astype(BF16)
        wo = jnp.concatenate([_pair_gqa_heads(w_out[l, 0:Q_WIDTH], 0), w_out[l, Q_WIDTH:]], axis=0).astype(BF16)
        w1 = w_mlp_in[l].astype(BF16)
        w2 = w_mlp_out[l].astype(BF16)
        g0 = norm_g[l, 0][None, :]

        def norm_args(tm):
            if not even:
                return None
            return (jnp.broadcast_to(qk_norm_a[i, 0][:, None], (HEAD_DIM, tm)),
                    jnp.tile(qk_norm_a[i, 1], PAIR // HEAD_DIM)[None, :], seg)

        q1x, k1x, v1x, q2x, k2x, v2x = _in_proj(x, mod_x, g0, wt, wk, norm_args(tm_x), rope_args,
                                                even=even, tm=tm_x)
        q1c, k1c, v1c, q2c, k2c, v2c = _in_proj(ctx, mod_c, g0, wt, wk, norm_args(n_ctx), None,
                                                even=even, tm=n_ctx)
        x_kv = (k1x, v1x, k2x, v2x)
        ctx_kv = (k1c, v1c, k2c, v2c)
        if even:
            lam0 = _lambda_init(l)
            sub = diff_subln[i][None, :]
            m_x = _attn_even((q1x, q2x), ctx_kv, x_kv, diff_lambda[i], sub, lam0=lam0, tq=256)
            if not last:
                m_c = _attn_even((q1c, q2c), ctx_kv, None, diff_lambda[i], sub, lam0=lam0, tq=256)
        else:
            if not last:
                raise NotImplementedError("context update after an odd layer is not needed for depth 2")
            tp = _bias_table(rpb_d[i])
            m_x = _attn_odd((q1x, q2x), x_kv, ctx_kv, sink_c[i], tp, tq=256, sink_order=GQA_SLOT_HEADS)
        x = _post(m_x, x, mod_x, norm_g[l], wo, w1, w2, tm=512, ff_chunk=1024)
        if not last:
            ctx = _post(m_c, ctx, mod_c, norm_g[l], wo, w1, w2, tm=n_ctx, ff_chunk=1024)
    return x
```

```python
import functools
import math
from typing import Any, Callable, NamedTuple, Sequence

import numpy as np
import jax
import jax.numpy as jnp
from jax import lax
from jax.experimental import pallas as pl
from jax.experimental.pallas import tpu as pltpu

F32 = jnp.float32
BF16 = jnp.bfloat16

GRID_W = 64
HEAD_DIM = 64
ROPE_THETA = 10000.0
ROPE_FREQS = HEAD_DIM // 4
N_SLOTS = 8
PAIR = 2 * HEAD_DIM
Q_WIDTH = N_SLOTS * HEAD_DIM
LANES = 128
MXU_QUERY_COLS = 512
EVEN_A_QUERY_COLS = 512
EVEN_KEY_CHUNK = 1024
IN_SUB_ROWS = 512
ATTEND_EXTRA_LEAD = 2
POST_SUB_ROWS = 256
C_SLOTS_PER_JOB = 8
NA_INVALID_COLUMN = 1 << 20
NA_CHUNK_ROWS = 4
C_WINDOW = 128
NA_ROWS = 8
NA_COLS = 16
EPS = 1e-6
NEG = -1e30
LOG2E = math.log2(math.e)
Q_SCALE = HEAD_DIM ** -0.5 * LOG2E
GQA_SLOT_HEADS = tuple(half * (N_SLOTS // 2) + p for p in range(N_SLOTS // 2) for half in range(2))

VMEM_LIMIT_BYTES = 56 * 1024 * 1024
ATTN_SCHEDULER_FLAGS = None


def _compiler_params(n_grid_dims, flags=None):
    return pltpu.CompilerParams(dimension_semantics=("arbitrary",) * n_grid_dims,
                                vmem_limit_bytes=VMEM_LIMIT_BYTES, flags=flags)


def _dot(a, b):
    return jnp.dot(a, b, preferred_element_type=F32)


def _dot_nt(a, b):
    return lax.dot_general(a, b, (((1,), (1,)), ((), ())), preferred_element_type=F32)


def _log2(n):
    assert n & (n - 1) == 0
    return n.bit_length() - 1


def _rms(x, gain):
    ms = jnp.mean(x * x, axis=-1, keepdims=True)
    return x * lax.rsqrt(ms + EPS) * gain


def _mod_kernel(c_ref, w_ref, b_ref, o_ref):
    c = c_ref[...]
    s = c * jax.nn.sigmoid(c)
    o_ref[0] = _dot(s.astype(BF16), w_ref[0].astype(BF16)) + b_ref[0]


def _modulation(cc, w_mod, b_mod):
    depth, d, n = w_mod.shape
    rows = cc.shape[0]
    tn = n // 4
    return pl.pallas_call(
        _mod_kernel,
        grid=(depth, n // tn),
        in_specs=[pl.BlockSpec((rows, d), lambda l, j: (0, 0)),
                  pl.BlockSpec((1, d, tn), lambda l, j: (l, 0, j)),
                  pl.BlockSpec((1, 1, tn), lambda l, j: (l, 0, j))],
        out_specs=pl.BlockSpec((1, rows, tn), lambda l, j: (l, 0, j)),
        out_shape=jax.ShapeDtypeStruct((depth, rows, n), F32),
        compiler_params=_compiler_params(2),
        name="modulation",
    )(cc, w_mod, b_mod.reshape(depth, 1, n))


def _in_kernel(*refs, even, rope, sub_rows):
    refs = list(refs)
    tok_ref, mod_ref, g_ref, wt_ref, wk_ref = refs[:5]
    del refs[:5]
    if even:
        gq_ref, gk_ref, seg_ref = refs[:3]
        del refs[:3]
    if rope:
        rope_t_ref, rope_ref = refs[:2]
        del refs[:2]
    q1_ref, k1_ref, v1_ref, q2_ref, k2_ref, v2_ref = refs
    d = tok_ref.shape[-1]
    tm = tok_ref.shape[1]
    sub = min(sub_rows, tm)

    def normed(rows):
        h = _rms(tok_ref[0, rows], g_ref[...])
        h = h * (1.0 + mod_ref[0, :, d:2 * d]) + mod_ref[0, :, 0:d]
        return h.astype(BF16)

    def project(rows, hb):
        def rotary_t(c):
            f = ROPE_FREQS
            out = []
            for a in range(2):
                x1, x2 = c[2 * a * f:(2 * a + 1) * f], c[(2 * a + 1) * f:(2 * a + 2) * f]
                cos, sin = rope_t_ref[0, a * f:(a + 1) * f, rows], rope_t_ref[1, a * f:(a + 1) * f, rows]
                out += [x1 * cos - x2 * sin, x2 * cos + x1 * sin]
            return jnp.concatenate(out, axis=0)

        def store_slots(q_ref, qt, use_norm, use_rope):
            zeros = jnp.zeros((HEAD_DIM, sub), F32)
            for j in range(N_SLOTS):
                c = qt[j * HEAD_DIM:(j + 1) * HEAD_DIM]
                if use_norm:
                    ms = jnp.mean(c * c, axis=0, keepdims=True)
                    c = c * lax.rsqrt(ms + EPS) * gq_ref[:, 0:sub]
                if use_rope:
                    c = rotary_t(c)
                c = c * Q_SCALE
                slot = jnp.concatenate([c, zeros] if j % 2 == 0 else [zeros, c], axis=0)
                q_ref[0, j, :, rows] = slot.astype(BF16)

        def store_vt(v_ref, vt):
            if even:
                v_ref[0, :, rows] = vt.astype(BF16)
            else:
                for c in range(sub // LANES):
                    v_ref[0, rows.start // LANES + c] = vt[:, c * LANES:(c + 1) * LANES].astype(BF16)

        def rotary(c):
            up = pltpu.roll(c, PAIR - ROPE_FREQS, axis=1)
            down = pltpu.roll(c, ROPE_FREQS, axis=1)
            return c * rope_ref[0, rows] + up * rope_ref[1, rows] + down * rope_ref[2, rows]

        o = 0
        store_slots(q1_ref, _dot_nt(wt_ref[o:o + Q_WIDTH], hb), even, rope)
        o += Q_WIDTH
        store_vt(v1_ref, _dot_nt(wt_ref[o:o + PAIR], hb))
        o += PAIR
        store_slots(q2_ref, _dot_nt(wt_ref[o:o + Q_WIDTH], hb), False, rope and even)
        o += Q_WIDTH
        store_vt(v2_ref, _dot_nt(wt_ref[o:o + Q_WIDTH], hb))

        k = _dot(hb, wk_ref[...])
        k1 = k[:, 0:PAIR]
        if even:
            sq = k1 * k1
            hi = sq.astype(BF16)
            lo = (sq - hi.astype(F32)).astype(BF16)
            ms = (_dot(hi, seg_ref[...]) + _dot(lo, seg_ref[...])) * (1.0 / HEAD_DIM)
            k1 = k1 * lax.rsqrt(ms + EPS) * gk_ref[...]
        if rope:
            k1 = rotary(k1)
        k1_ref[0, rows] = k1.astype(BF16)
        for j in range(Q_WIDTH // PAIR):
            c = k[:, PAIR + j * PAIR:PAIR + (j + 1) * PAIR]
            if rope and even:
                c = rotary(c)
            k2_ref[0, rows, j * PAIR:(j + 1) * PAIR] = c.astype(BF16)

    blocks = [slice(r, r + sub) for r in range(0, tm, sub)]
    hbs = [normed(rows) for rows in blocks]
    for rows, hb in zip(blocks, hbs):
        project(rows, hb)


def _in_proj(tok, mod, gain, wt, wk, norm_args, rope_args, *, even, tm):
    b, s, d = tok.shape
    bm = mod.shape[0]
    rope = rope_args is not None
    mod_map = (lambda i, j: (i, 0, 0)) if bm > 1 else (lambda i, j: (0, 0, 0))

    def const(a):
        return pl.BlockSpec(a.shape, lambda i, j: (0,) * a.ndim)

    in_specs = [pl.BlockSpec((1, tm, d), lambda i, j: (i, j, 0)),
                pl.BlockSpec((1, 1, mod.shape[-1]), mod_map),
                const(gain), const(wt), const(wk)]
    args = [tok, mod, gain, wt, wk]
    if even:
        gq, gk, seg = norm_args
        in_specs += [const(gq), const(gk), const(seg)]
        args += [gq, gk, seg]
    if rope:
        rope_t, rope_n = rope_args
        in_specs += [pl.BlockSpec((2, 2 * ROPE_FREQS, tm), lambda i, j: (0, 0, j)),
                     pl.BlockSpec((3, tm, PAIR), lambda i, j: (0, j, 0))]
        args += [rope_t, rope_n]

    def vt_out(rows):
        if even:
            return (pl.BlockSpec((1, rows, tm), lambda i, j: (i, 0, j)),
                    jax.ShapeDtypeStruct((b, rows, s), BF16))
        return (pl.BlockSpec((1, tm // LANES, rows, LANES), lambda i, j: (i, j, 0, 0)),
                jax.ShapeDtypeStruct((b, s // LANES, rows, LANES), BF16))

    slot_out = (pl.BlockSpec((1, N_SLOTS, PAIR, tm), lambda i, j: (i, 0, 0, j)),
                jax.ShapeDtypeStruct((b, N_SLOTS, PAIR, s), BF16))

    def nat_out(cols):
        return (pl.BlockSpec((1, tm, cols), lambda i, j: (i, j, 0)), jax.ShapeDtypeStruct((b, s, cols), BF16))

    outs = [slot_out, nat_out(PAIR), vt_out(PAIR), slot_out, nat_out(Q_WIDTH), vt_out(Q_WIDTH)]
    return pl.pallas_call(
        functools.partial(_in_kernel, even=even, rope=rope, sub_rows=IN_SUB_ROWS),
        grid=(b, s // tm),
        in_specs=in_specs,
        out_specs=[o[0] for o in outs],
        out_shape=[o[1] for o in outs],
        compiler_params=_compiler_params(2),
        name="in_proj_" + ("even" if even else "odd") + ("_x" if rope else "_ctx"),
    )(*args)


class AttnJob(NamedTuple):
    query: Callable[[], Any]
    scores: Sequence[Callable[[Any], Any]]
    values: Sequence[Callable[[], Any]]
    finish: Callable[[Any], None]
    sink: Any = None


def _attend_lead(jobs):
    return max(len(job.scores) for job in jobs) + ATTEND_EXTRA_LEAD


def _attend_jobs(jobs, s_ref):
    items = [(j, c) for j, job in enumerate(jobs) for c in range(len(job.scores))]
    lead = _attend_lead(jobs)
    n_slots = s_ref.shape[0]
    assert n_slots > lead
    qt, shape, m, l, o = {}, {}, {}, {}, {}
    for t in range(len(items) + lead):
        if t < len(items):
            j, c = items[t]
            job = jobs[j]
            if c == 0:
                qt[j] = job.query()
            x = job.scores[c](qt[j])
            shape[t] = x.shape
            s_ref[t % n_slots, 0:x.shape[0], 0:x.shape[1]] = x
            part_m = jnp.max(x, axis=0, keepdims=True)
            if c == 0:
                m[j] = part_m if job.sink is None else jnp.maximum(part_m, job.sink)
            else:
                m[j] = jnp.maximum(m[j], part_m)
        if t >= lead:
            j, c = items[t - lead]
            job = jobs[j]
            rows, cols = shape.pop(t - lead)
            e = jnp.exp2(s_ref[(t - lead) % n_slots, 0:rows, 0:cols] - m[j])
            part_l = jnp.sum(e, axis=0, keepdims=True)
            part_o = _dot(job.values[c](), e.astype(BF16))
            l[j] = part_l if c == 0 else l[j] + part_l
            o[j] = part_o if c == 0 else o[j] + part_o
            if c == len(job.scores) - 1:
                denom = l.pop(j) if job.sink is None else l.pop(j) + jnp.exp2(job.sink - m[j])
                job.finish(o.pop(j) * (1.0 / denom))


def _attn_even_kernel(*refs, with_x, lam0, tq, key_chunk):
    if with_x:
        q1_ref, q2_ref, k1c_ref, v1c_ref, k2c_ref, v2c_ref, k1x_ref, v1x_ref, k2x_ref, v2x_ref = refs[:10]
    else:
        q1_ref, q2_ref, k1c_ref, v1c_ref, k2c_ref, v2c_ref = refs[:6]
    lam_ref, sub_ref, o_ref, s_ref = refs[-4:]

    def chunk_fns(c_ref, x_ref, cols, transposed):
        def make(ref, lo, hi):
            if transposed:
                return lambda: ref[0, cols, lo:hi]
            return lambda qt: _dot(ref[0, lo:hi, cols], qt)
        n_ctx = c_ref.shape[2] if transposed else c_ref.shape[1]
        fns = [make(c_ref, 0, n_ctx)]
        if with_x:
            n_x = x_ref.shape[2] if transposed else x_ref.shape[1]
            fns += [make(x_ref, lo, lo + key_chunk) for lo in range(0, n_x, key_chunk)]
        return fns

    group = max(2, EVEN_A_QUERY_COLS // tq)
    jobs = []
    pair_cols = slice(0, PAIR)

    def a_query(g):
        return jnp.concatenate([q1_ref[0, g * group + j] for j in range(group)], axis=1)

    def a_finish(g, o):
        for pp in range(group // 2):
            pair_t = jnp.concatenate([o[0:HEAD_DIM, (2 * pp) * tq:(2 * pp + 1) * tq],
                                      o[HEAD_DIM:PAIR, (2 * pp + 1) * tq:(2 * pp + 2) * tq]], axis=0)
            col = (g * (group // 2) + pp) * PAIR
            o_ref[0, :, col:col + PAIR] = pair_t.T.astype(BF16)

    for g in range(N_SLOTS // group):
        jobs.append(AttnJob(functools.partial(a_query, g),
                            chunk_fns(k1c_ref, k1x_ref if with_x else None, pair_cols, False),
                            chunk_fns(v1c_ref, v1x_ref if with_x else None, pair_cols, True),
                            functools.partial(a_finish, g)))

    def b_query(h):
        return jnp.concatenate([q2_ref[0, 2 * h], q2_ref[0, 2 * h + 1]], axis=1)

    def b_finish(h, o):
        lp = lam_ref[...]
        lam = (jnp.exp(jnp.sum(lp[0:1] * lp[1:2], axis=-1, keepdims=True))
               - jnp.exp(jnp.sum(lp[2:3] * lp[3:4], axis=-1, keepdims=True)) + lam0)
        ob = (o[:, 0:tq] - lam * o[:, tq:2 * tq]).T
        ob = _rms(ob, sub_ref[...]) * (1.0 - lam0)
        col = Q_WIDTH + h * PAIR
        o_ref[0, :, col:col + PAIR] = ob.astype(BF16)

    for h in range(N_SLOTS // 2):
        cols = slice(h * PAIR, (h + 1) * PAIR)
        jobs.append(AttnJob(functools.partial(b_query, h),
                            chunk_fns(k2c_ref, k2x_ref if with_x else None, cols, False),
                            chunk_fns(v2c_ref, v2x_ref if with_x else None, cols, True),
                            functools.partial(b_finish, h)))
    _attend_jobs(jobs, s_ref)


def _ctx_specs(ctx_kv, n_ctx, blocked):
    def natural(a):
        return pl.BlockSpec((1, n_ctx, a.shape[2]), lambda i, j: (0, i, 0))

    def transposed(a):
        if blocked:
            return pl.BlockSpec((1, n_ctx // LANES) + a.shape[2:], lambda i, j: (0, i, 0, 0))
        return pl.BlockSpec((1, a.shape[1], n_ctx), lambda i, j: (0, 0, i))

    k1, v1, k2, v2 = ctx_kv
    return [natural(k1), transposed(v1), natural(k2), transposed(v2)]


def _attn_even(qx, ctx_kv, x_kv, lam_p, subln, *, n_ctx, lam0, tq):
    q1, q2 = qx
    with_x = x_kv is not None
    if with_x:
        b, _, _, s = q1.shape
        slot_spec = pl.BlockSpec((1, N_SLOTS, PAIR, tq), lambda i, j: (i, 0, 0, j))
    else:
        b, s = q1.shape[3] // n_ctx, n_ctx
        slot_spec = pl.BlockSpec((1, N_SLOTS, PAIR, tq), lambda i, j: (0, 0, 0, i * (n_ctx // tq) + j))

    def full(a):
        return pl.BlockSpec((1,) + a.shape[1:], lambda i, j: (i, 0, 0))

    kv = list(ctx_kv) + (list(x_kv) if with_x else [])
    key_chunk = min(EVEN_KEY_CHUNK, x_kv[0].shape[1]) if with_x else n_ctx
    chunks_per_job = 1 + (x_kv[0].shape[1] // key_chunk if with_x else 0)
    score_ring = pltpu.VMEM((chunks_per_job + ATTEND_EXTRA_LEAD + 1, max(n_ctx, key_chunk),
                             max(2 * tq, EVEN_A_QUERY_COLS)), F32)
    return pl.pallas_call(
        functools.partial(_attn_even_kernel, with_x=with_x, lam0=lam0, tq=tq, key_chunk=key_chunk),
        grid=(b, s // tq),
        scratch_shapes=[score_ring],
        in_specs=[slot_spec, slot_spec] + _ctx_specs(ctx_kv, n_ctx, blocked=False)
                 + ([full(a) for a in x_kv] if with_x else [])
                 + [pl.BlockSpec(lam_p.shape, lambda i, j: (0, 0)), pl.BlockSpec(subln.shape, lambda i, j: (0, 0))],
        out_specs=pl.BlockSpec((1, tq, 2 * Q_WIDTH), lambda i, j: (i, j, 0)),
        out_shape=jax.ShapeDtypeStruct((b, s, 2 * Q_WIDTH), BF16),
        compiler_params=_compiler_params(2, ATTN_SCHEDULER_FLAGS),
        name="attn_even" + ("_x" if with_x else "_ctx"),
    )(q1, q2, *kv, lam_p, subln)


N_BIAS_ROWS = 2 * NA_ROWS


def _bias_kernel(rpb_ref, o_ref):
    head = pl.program_id(0)
    n_dr = 2 * NA_ROWS - 1
    n_dc = 2 * NA_COLS - 1
    shape = (GRID_W, LANES)
    ck = lax.broadcasted_iota(jnp.int32, shape, 0)
    lane = lax.broadcasted_iota(jnp.int32, shape, 1)
    second = lane >= GRID_W
    dcl = jnp.clip(ck - (lane & (GRID_W - 1)), -(NA_COLS - 1), NA_COLS - 1) + (NA_COLS - 1)
    for t in range(N_BIAS_ROWS):
        acc = jnp.zeros(shape, F32)
        for j in range(n_dc):
            first_v = rpb_ref[(head * n_dr + min(t, n_dr - 1)) * n_dc + j]
            second_v = rpb_ref[(head * n_dr + max(t - 1, 0)) * n_dc + j]
            acc = jnp.where(dcl == j, jnp.where(second, second_v, first_v), acc)
        o_ref[0, t] = acc * LOG2E


def _bias_table(rpb):
    heads = rpb.shape[0]
    return pl.pallas_call(
        _bias_kernel,
        grid=(heads,),
        in_specs=[pl.BlockSpec(memory_space=pltpu.SMEM)],
        out_specs=pl.BlockSpec((1, N_BIAS_ROWS, GRID_W, LANES), lambda h: (h, 0, 0, 0)),
        out_shape=jax.ShapeDtypeStruct((heads, N_BIAS_ROWS, GRID_W, LANES), F32),
        compiler_params=_compiler_params(1),
        name="rel_bias_table",
    )(rpb.reshape(-1))


def _attn_odd_kernel(q1_ref, q2_ref, k1x_ref, v1x_ref, k2x_ref, v2x_ref, k1c_ref, v1c_ref, k2c_ref, v2c_ref,
                     sink_ref, tp_ref, win_ref, o_ref, s_ref, *, tq, sink_order):
    i = pl.program_id(1)
    t = k1x_ref.shape[1]
    n_ctx_blk = v1c_ref.shape[1]
    rows = t // GRID_W
    kh = min(NA_ROWS, rows)
    n_blk = tq // LANES

    jobs = []

    n_q = N_SLOTS * C_WINDOW
    n_seq_blk = t // C_WINDOW
    lane_q = lax.broadcasted_iota(jnp.int32, (1, n_q), 1)
    sink = jnp.zeros((1, n_q), F32)
    for j in range(N_SLOTS):
        sink = jnp.where(lane_q >> _log2(C_WINDOW) == j, sink_ref[sink_order[j]], sink)
    sink = sink * LOG2E
    n_qj = C_SLOTS_PER_JOB * C_WINDOW
    tri = (lax.broadcasted_iota(jnp.int32, (C_WINDOW, n_qj), 0)
           - (lax.broadcasted_iota(jnp.int32, (C_WINDOW, n_qj), 1) & (C_WINDOW - 1)))

    def c_blocks(jb):
        blk = i * n_blk + jb
        return blk, jnp.maximum(blk - 1, 0), jnp.minimum(blk + 1, n_seq_blk - 1)

    def c_query(jb, s0):
        return jnp.concatenate([q1_ref[0, j, :, jb * LANES:(jb + 1) * LANES]
                                for j in range(s0, s0 + C_SLOTS_PER_JOB)], axis=1)

    def c_local_scores(jb, qt):
        blk, prev, nxt = c_blocks(jb)
        k_loc = jnp.concatenate([k1x_ref[0, pl.ds(pl.multiple_of(n * C_WINDOW, C_WINDOW), C_WINDOW), :]
                                 for n in (prev, blk, nxt)], axis=0)
        s = _dot(k_loc, qt)
        lo_bound = jnp.where(blk > 0, 0, C_WINDOW)
        hi_bound = jnp.where(blk < n_seq_blk - 1, 0, -C_WINDOW)
        return jnp.concatenate([jnp.where(tri >= lo_bound, s[0:C_WINDOW], NEG),
                                s[C_WINDOW:2 * C_WINDOW],
                                jnp.where(tri <= hi_bound, s[2 * C_WINDOW:3 * C_WINDOW], NEG)], axis=0)

    def c_local_values(jb):
        blk, prev, nxt = c_blocks(jb)
        return jnp.concatenate([v1x_ref[0, n] for n in (prev, blk, nxt)], axis=1)

    def c_finish(jb, s0, o):
        for p in range(C_SLOTS_PER_JOB // 2):
            pair_t = jnp.concatenate([o[0:HEAD_DIM, (2 * p) * LANES:(2 * p + 1) * LANES],
                                      o[HEAD_DIM:PAIR, (2 * p + 1) * LANES:(2 * p + 2) * LANES]], axis=0)
            col = (s0 // 2 + p) * PAIR
            o_ref[0, jb * LANES:(jb + 1) * LANES, col:col + PAIR] = pair_t.T.astype(BF16)

    for jb in range(n_blk):
        for s0 in range(0, N_SLOTS, C_SLOTS_PER_JOB):
            jobs.append(AttnJob(
                functools.partial(c_query, jb, s0),
                [functools.partial(c_local_scores, jb), lambda qt: _dot(k1c_ref[0], qt)],
                [functools.partial(c_local_values, jb),
                 lambda: jnp.concatenate([v1c_ref[0, c] for c in range(n_ctx_blk)], axis=1)],
                functools.partial(c_finish, jb, s0), sink[:, s0 * C_WINDOW:(s0 + C_SLOTS_PER_JOB) * C_WINDOW]))

    n_rows = tq // GRID_W
    span_rows = -(-(n_rows - 1 + kh) // 2) * 2
    n_loc = span_rows * GRID_W
    r0 = i * n_rows
    u = jnp.minimum(jnp.clip(r0 - kh // 2, 0, rows - kh), rows - span_rows)
    q_row = (lax.broadcasted_iota(jnp.int32, (1, 2 * tq), 1) >> _log2(GRID_W)) & (n_rows - 1)
    first = jnp.zeros((1, 2 * tq), jnp.int32)
    for j in range(n_rows):
        first = jnp.where(q_row == j, jnp.clip(r0 + j - kh // 2, 0, rows - kh) - u, first)
    window_row = win_ref[...] - first

    def valid(c):
        w = window_row[c * NA_CHUNK_ROWS * GRID_W:(c + 1) * NA_CHUNK_ROWS * GRID_W]
        return lax.bitcast_convert_type(w, jnp.uint32) < jnp.uint32(kh)

    chunk_rows = NA_CHUNK_ROWS
    chunk_keys = chunk_rows * GRID_W
    blk_rows = LANES // GRID_W

    def d_query(p):
        return jnp.concatenate([q2_ref[0, 2 * p], q2_ref[0, 2 * p + 1]], axis=1)

    def d_local_scores(p, c, qt):
        k_loc = k2x_ref[0, pl.ds(pl.multiple_of((u + c * chunk_rows) * GRID_W, LANES), chunk_keys),
                        p * PAIR:(p + 1) * PAIR]
        bias = jnp.concatenate(
            [jnp.concatenate(
                [tp_ref[2 * p + lb // (n_rows // 2),
                        jnp.clip(u + a - (r0 + 2 * (lb % (n_rows // 2))) + NA_ROWS - 1, 0, N_BIAS_ROWS - 1)]
                 for lb in range(n_rows)], axis=1)
             for a in range(c * chunk_rows, (c + 1) * chunk_rows)], axis=0)
        return jnp.where(valid(c), _dot(k_loc, qt) + bias, NEG)

    def d_local_values(p, c):
        blk0 = (u + c * chunk_rows) // blk_rows
        return jnp.concatenate([v2x_ref[0, blk0 + b, p * PAIR:(p + 1) * PAIR, :]
                                for b in range(chunk_rows // blk_rows)], axis=1)

    def d_finish(p, o):
        pair_t = jnp.concatenate([o[0:HEAD_DIM, 0:tq], o[HEAD_DIM:PAIR, tq:2 * tq]], axis=0)
        o_ref[0, :, Q_WIDTH + p * PAIR:Q_WIDTH + (p + 1) * PAIR] = pair_t.T.astype(BF16)

    for p in range(N_SLOTS // 2):
        cols = slice(p * PAIR, (p + 1) * PAIR)
        n_chunks = span_rows // chunk_rows
        jobs.append(AttnJob(
            functools.partial(d_query, p),
            [functools.partial(d_local_scores, p, c) for c in range(n_chunks)]
            + [lambda qt, cols=cols: _dot(k2c_ref[0, :, cols], qt)],
            [functools.partial(d_local_values, p, c) for c in range(n_chunks)]
            + [lambda cols=cols: jnp.concatenate([v2c_ref[0, b, cols, :] for b in range(n_ctx_blk)], axis=1)],
            functools.partial(d_finish, p)))
    _attend_jobs(jobs, s_ref)


def _attn_odd(qx, x_kv, ctx_kv, sink, tp, *, n_ctx, tq, sink_order):
    q1, q2 = qx
    b, _, _, s = q1.shape
    slot_spec = pl.BlockSpec((1, N_SLOTS, PAIR, tq), lambda i, j: (i, 0, 0, j))

    def full(a):
        return pl.BlockSpec((1,) + a.shape[1:], lambda i, j: (i,) + (0,) * (a.ndim - 1))

    kv = list(x_kv) + list(ctx_kv)
    rows = s // GRID_W
    span_rows = -(-(tq // GRID_W - 1 + min(NA_ROWS, rows)) // 2) * 2
    chunks_per_job = max(2, span_rows // NA_CHUNK_ROWS + 1)
    key_i = np.arange(span_rows * GRID_W)[:, None]
    cq = np.arange(2 * tq)[None, :] % GRID_W
    cs = np.clip(cq - NA_COLS // 2, 0, GRID_W - NA_COLS)
    ck = key_i % GRID_W
    col_ok = (ck >= cs) & (ck < cs + NA_COLS)
    window = jnp.asarray(key_i // GRID_W + np.where(col_ok, 0, NA_INVALID_COLUMN), jnp.int32)
    score_ring = pltpu.VMEM((chunks_per_job + ATTEND_EXTRA_LEAD + 1, max(3 * C_WINDOW, NA_CHUNK_ROWS * GRID_W, n_ctx),
                             max(N_SLOTS * C_WINDOW, 2 * tq)), F32)
    return pl.pallas_call(
        functools.partial(_attn_odd_kernel, tq=tq, sink_order=sink_order),
        grid=(b, s // tq),
        scratch_shapes=[score_ring],
        in_specs=[slot_spec, slot_spec] + [full(a) for a in x_kv] + _ctx_specs(ctx_kv, n_ctx, blocked=True)
                 + [pl.BlockSpec(memory_space=pltpu.SMEM), pl.BlockSpec(tp.shape, lambda i, j: (0, 0, 0, 0)),
                    pl.BlockSpec(window.shape, lambda i, j: (0, 0))],
        out_specs=pl.BlockSpec((1, tq, 2 * Q_WIDTH), lambda i, j: (i, j, 0)),
        out_shape=jax.ShapeDtypeStruct((b, s, 2 * Q_WIDTH), BF16),
        compiler_params=_compiler_params(2),
        name="attn_odd",
    )(q1, q2, *kv, sink, tp, window)


def _post_kernel(m_ref, x_ref, mod_ref, g_ref, wo_ref, w1_ref, w2_ref, o_ref, *, ff_chunk, sub_rows):
    d = x_ref.shape[-1]

    def mod(k):
        return mod_ref[0, :, k * d:(k + 1) * d]

    def mixer_residual(rows):
        x1 = x_ref[0, rows] + mod(2) * _rms(_dot(m_ref[0, rows], wo_ref[...]), g_ref[1:2])
        return x1, (_rms(x1, g_ref[2:3]) * (1.0 + mod(4)) + mod(3)).astype(BF16)

    def mlp(h):
        f = None
        for c in range(w1_ref.shape[1] // ff_chunk):
            cols = slice(c * ff_chunk, (c + 1) * ff_chunk)
            a = jnp.maximum(_dot(h, w1_ref[:, cols]), 0.0)
            part = _dot((a * a).astype(BF16), w2_ref[cols, :])
            f = part if f is None else f + part
        return f

    tm = x_ref.shape[1]
    sub = min(sub_rows, tm)
    blocks = [slice(r, r + sub) for r in range(0, tm, sub)]
    stage1 = [mixer_residual(rows) for rows in blocks]
    stage2 = [mlp(h) for _, h in stage1]
    for rows, (x1, _), f in zip(blocks, stage1, stage2):
        o_ref[0, rows] = x1 + mod(5) * _rms(f, g_ref[3:4])


def _post(m, x, mod, gains, wo, w1, w2, *, tm, ff_chunk):
    b, s, d = x.shape
    bm = mod.shape[0]
    mod_map = (lambda i, j: (i, 0, 0)) if bm > 1 else (lambda i, j: (0, 0, 0))

    def const(a):
        return pl.BlockSpec(a.shape, lambda i, j: (0, 0), pipeline_mode=pl.Buffered(1))

    return pl.pallas_call(
        functools.partial(_post_kernel, ff_chunk=ff_chunk, sub_rows=POST_SUB_ROWS),
        grid=(b, s // tm),
        in_specs=[pl.BlockSpec((1, tm, m.shape[-1]), lambda i, j: (i, j, 0)),
                  pl.BlockSpec((1, tm, d), lambda i, j: (i, j, 0)),
                  pl.BlockSpec((1, 1, mod.shape[-1]), mod_map),
                  pl.BlockSpec(gains.shape, lambda i, j: (0, 0)),
                  const(wo), const(w1), const(w2)],
        out_specs=pl.BlockSpec((1, tm, d), lambda i, j: (i, j, 0)),
        out_shape=jax.ShapeDtypeStruct((b, s, d), F32),
        compiler_params=_compiler_params(2),
        name="post",
    )(m, x, mod, gains, wo, w1, w2)


def _rope_tables(t):
    tok = jnp.arange(t, dtype=jnp.int32)
    pos = jnp.stack([tok // GRID_W, tok % GRID_W], axis=-1).astype(F32)
    inv = ROPE_THETA ** (-jnp.arange(ROPE_FREQS, dtype=F32) / ROPE_FREQS)
    ang = pos[..., None] * inv
    cos, sin = jnp.cos(ang), jnp.sin(ang)
    rope_t = jnp.stack([cos.reshape(t, -1).T, sin.reshape(t, -1).T])
    zero = jnp.zeros_like(sin[:, 0])
    c = jnp.concatenate([cos[:, 0], cos[:, 0], cos[:, 1], cos[:, 1]], axis=-1)
    up = jnp.concatenate([-sin[:, 0], zero, -sin[:, 1], zero], axis=-1)
    down = jnp.concatenate([zero, sin[:, 0], zero, sin[:, 1]], axis=-1)
    rope_n = jnp.stack([jnp.tile(a, (1, PAIR // HEAD_DIM)) for a in (c, up, down)])
    return rope_t, rope_n


def _pair_gqa_heads(w, axis):
    shape = w.shape
    w = w.reshape(shape[:axis] + (2, N_SLOTS // 2, HEAD_DIM) + shape[axis + 1:])
    return jnp.swapaxes(w, axis, axis + 1).reshape(shape)


def _lambda_init(layer):
    return 0.8 - 0.6 * math.exp(-0.3 * layer)


def kernel(x, c, ctx, c_ctx, w_mod, b_mod, norm_g, w_in, w_out, w_mlp_in, w_mlp_out, qk_norm_a, diff_lambda,
           diff_subln, sink_c, rpb_d):
    b, t, d = x.shape
    n_ctx = ctx.shape[1]
    depth = w_mod.shape[0]
    mod_rows = 16
    cc = jnp.concatenate([c, c_ctx[None], jnp.zeros((mod_rows - b - 1, d), F32)], axis=0)
    mod_all = _modulation(cc, w_mod, b_mod)

    rope_args = _rope_tables(t)
    seg = jnp.kron(jnp.eye(2, dtype=F32), jnp.ones((HEAD_DIM, HEAD_DIM), F32)).astype(BF16)
    tm_x = 2 * IN_SUB_ROWS
    ctx = ctx.reshape(1, b * n_ctx, d)

    for l in range(depth):
        last = l == depth - 1
        even = l % 2 == 0
        i = l // 2
        mod_x = mod_all[l, :b][:, None, :]
        mod_c = mod_all[l, b][None, None, :]
        w = w_in[l]
        o_k1, o_v1, o_q2, o_k2, o_v2 = Q_WIDTH, Q_WIDTH + PAIR, Q_WIDTH + 2 * PAIR, 2 * Q_WIDTH + 2 * PAIR, \
            3 * Q_WIDTH + 2 * PAIR
        wt = jnp.concatenate([_pair_gqa_heads(w[:, 0:o_k1], 1), w[:, o_v1:o_q2], w[:, o_q2:o_k2], w[:, o_v2:]],
                             axis=1).T.astype(BF16)
        wk = jnp.concatenate([w[:, o_k1:o_v1], w[:, o_k2:o_v2]], axis=1).astype(BF16)
        wo = jnp.concatenate([_pair_gqa_heads(w_out[l, 0:Q_WIDTH], 0), w_out[l, Q_WIDTH:]], axis=0).astype(BF16)
        w1 = w_mlp_in[l].astype(BF16)
        w2 = w_mlp_out[l].astype(BF16)
        g0 = norm_g[l, 0][None, :]

        def norm_args(tm):
            if not even:
                return None
            return (jnp.broadcast_to(qk_norm_a[i, 0][:, None], (HEAD_DIM, tm)),
                    jnp.tile(qk_norm_a[i, 1], PAIR // HEAD_DIM)[None, :], seg)

        q1x, k1x, v1x, q2x, k2x, v2x = _in_proj(x, mod_x, g0, wt, wk, norm_args(tm_x), rope_args,
                                                even=even, tm=tm_x)
        q1c, k1c, v1c, q2c, k2c, v2c = _in_proj(ctx, mod_c, g0, wt, wk, norm_args(tm_x), None,
                                                even=even, tm=tm_x)
        x_kv = (k1x, v1x, k2x, v2x)
        ctx_kv = (k1c, v1c, k2c, v2c)
        if even:
            lam0 = _lambda_init(l)
            sub = diff_subln[i][None, :]
            m_x = _attn_even((q1x, q2x), ctx_kv, x_kv, diff_lambda[i], sub, n_ctx=n_ctx, lam0=lam0, tq=256)
            if not last:
                m_c = _attn_even((q1c, q2c), ctx_kv, None, diff_lambda[i], sub, n_ctx=n_ctx, lam0=lam0, tq=256)
        else:
            if not last:
                raise NotImplementedError("context update after an odd layer is not needed for depth 2")
            tp = _bias_table(rpb_d[i])
            m_x = _attn_odd((q1x, q2x), x_kv, ctx_kv, sink_c[i], tp, n_ctx=n_ctx, tq=256,
                            sink_order=GQA_SLOT_HEADS)
        x = _post(m_x, x, mod_x, norm_g[l], wo, w1, w2, tm=512, ff_chunk=1024)
        if not last:
            ctx = _post(m_c.reshape(1, b * n_ctx, -1), ctx, mod_c, norm_g[l], wo, w1, w2, tm=512, ff_chunk=1024)
    return x
```

```python
import functools
import math
from typing import Any, Callable, NamedTuple, Sequence

import numpy as np
import jax
import jax.numpy as jnp
from jax import lax
from jax.experimental import pallas as pl
from jax.experimental.pallas import tpu as pltpu

F32 = jnp.float32
BF16 = jnp.bfloat16

GRID_W = 64
HEAD_DIM = 64
ROPE_THETA = 10000.0
ROPE_FREQS = HEAD_DIM // 4
N_SLOTS = 8
PAIR = 2 * HEAD_DIM
Q_WIDTH = N_SLOTS * HEAD_DIM
LANES = 128
MXU_QUERY_COLS = 512
EVEN_A_QUERY_COLS = MXU_QUERY_COLS
EVEN_KEY_CHUNK = 1024
ATTEND_EXTRA_LEAD = 2
IN_SUB_ROWS = MXU_QUERY_COLS
POST_SUB_ROWS = 256
C_SLOTS_PER_JOB = 8
NA_CHUNK_ROWS = 4
NA_INVALID_COLUMN = 1 << 20
C_WINDOW = 128
NA_ROWS = 8
NA_COLS = 16
EPS = 1e-6
NEG = -1e30
LOG2E = math.log2(math.e)
Q_SCALE = HEAD_DIM ** -0.5 * LOG2E
GQA_SLOT_HEADS = tuple(half * (N_SLOTS // 2) + p for p in range(N_SLOTS // 2) for half in range(2))

VMEM_LIMIT_BYTES = 56 * 1024 * 1024


def _compiler_params(n_grid_dims):
    return pltpu.CompilerParams(dimension_semantics=("arbitrary",) * n_grid_dims,
                                vmem_limit_bytes=VMEM_LIMIT_BYTES)


def _dot(a, b):
    return jnp.dot(a, b, preferred_element_type=F32)


def _dot_nt(a, b):
    return lax.dot_general(a, b, (((1,), (1,)), ((), ())), preferred_element_type=F32)


def _log2(n):
    assert n & (n - 1) == 0
    return n.bit_length() - 1


def _rms(x, gain):
    ms = jnp.mean(x * x, axis=-1, keepdims=True)
    return x * lax.rsqrt(ms + EPS) * gain


def _mod_kernel(c_ref, w_ref, b_ref, o_ref):
    c = c_ref[...]
    s = c * jax.nn.sigmoid(c)
    o_ref[0] = _dot(s.astype(BF16), w_ref[0].astype(BF16)) + b_ref[0]


def _modulation(cc, w_mod, b_mod):
    depth, d, n = w_mod.shape
    rows = cc.shape[0]
    tn = n // 4
    return pl.pallas_call(
        _mod_kernel,
        grid=(depth, n // tn),
        in_specs=[pl.BlockSpec((rows, d), lambda l, j: (0, 0)),
                  pl.BlockSpec((1, d, tn), lambda l, j: (l, 0, j)),
                  pl.BlockSpec((1, 1, tn), lambda l, j: (l, 0, j))],
        out_specs=pl.BlockSpec((1, rows, tn), lambda l, j: (l, 0, j)),
        out_shape=jax.ShapeDtypeStruct((depth, rows, n), F32),
        compiler_params=_compiler_params(2),
        name="modulation",
    )(cc, w_mod, b_mod.reshape(depth, 1, n))


def _in_kernel(*refs, even, n_batch, sub_rows):
    refs = list(refs)
    tok_ref, ctx_ref, mod_ref, g_ref, wt_ref, wk_ref = refs[:6]
    del refs[:6]
    if even:
        gq_ref, gk_ref, seg_ref = refs[:3]
        del refs[:3]
    rope_t_ref, rope_ref = refs[:2]
    q1_ref, k1_ref, v1_ref, q2_ref, k2_ref, v2_ref = refs[2:]
    d = tok_ref.shape[-1]
    tm = tok_ref.shape[1]
    sub = min(sub_rows, tm)
    is_ctx = pl.program_id(0) == n_batch

    def normed(rows):
        h = _rms(jnp.where(is_ctx, ctx_ref[0, rows], tok_ref[0, rows]), g_ref[...])
        h = h * (1.0 + mod_ref[0, :, d:2 * d]) + mod_ref[0, :, 0:d]
        return h.astype(BF16)

    def project(rows, hb):
        def rotary_t(c):
            f = ROPE_FREQS
            out = []
            for a in range(2):
                x1, x2 = c[2 * a * f:(2 * a + 1) * f], c[(2 * a + 1) * f:(2 * a + 2) * f]
                cos, sin = rope_t_ref[0, 0, a * f:(a + 1) * f, rows], rope_t_ref[0, 1, a * f:(a + 1) * f, rows]
                out += [x1 * cos - x2 * sin, x2 * cos + x1 * sin]
            return jnp.concatenate(out, axis=0)

        def store_slots(q_ref, qt, use_norm, use_rope):
            zeros = jnp.zeros((HEAD_DIM, sub), F32)
            for j in range(N_SLOTS):
                c = qt[j * HEAD_DIM:(j + 1) * HEAD_DIM]
                if use_norm:
                    ms = jnp.mean(c * c, axis=0, keepdims=True)
                    c = c * lax.rsqrt(ms + EPS) * gq_ref[:, 0:sub]
                if use_rope:
                    c = rotary_t(c)
                c = c * Q_SCALE
                slot = jnp.concatenate([c, zeros] if j % 2 == 0 else [zeros, c], axis=0)
                q_ref[0, j, :, rows] = slot.astype(BF16)

        def store_vt(v_ref, vt):
            if even:
                v_ref[0, :, rows] = vt.astype(BF16)
            else:
                for c in range(sub // LANES):
                    v_ref[0, rows.start // LANES + c] = vt[:, c * LANES:(c + 1) * LANES].astype(BF16)

        def rotary(c):
            up = pltpu.roll(c, PAIR - ROPE_FREQS, axis=1)
            down = pltpu.roll(c, ROPE_FREQS, axis=1)
            return c * rope_ref[0, 0, rows] + up * rope_ref[0, 1, rows] + down * rope_ref[0, 2, rows]

        o = 0
        store_slots(q1_ref, _dot_nt(wt_ref[o:o + Q_WIDTH], hb), even, True)
        o += Q_WIDTH
        store_vt(v1_ref, _dot_nt(wt_ref[o:o + PAIR], hb))
        o += PAIR
        store_slots(q2_ref, _dot_nt(wt_ref[o:o + Q_WIDTH], hb), False, even)
        o += Q_WIDTH
        store_vt(v2_ref, _dot_nt(wt_ref[o:o + Q_WIDTH], hb))

        k = _dot(hb, wk_ref[...])
        k1 = k[:, 0:PAIR]
        if even:
            sq = k1 * k1
            hi = sq.astype(BF16)
            lo = (sq - hi.astype(F32)).astype(BF16)
            ms = (_dot(hi, seg_ref[...]) + _dot(lo, seg_ref[...])) * (1.0 / HEAD_DIM)
            k1 = k1 * lax.rsqrt(ms + EPS) * gk_ref[...]
        k1 = rotary(k1)
        k1_ref[0, rows] = k1.astype(BF16)
        for j in range(Q_WIDTH // PAIR):
            c = k[:, PAIR + j * PAIR:PAIR + (j + 1) * PAIR]
            if even:
                c = rotary(c)
            k2_ref[0, rows, j * PAIR:(j + 1) * PAIR] = c.astype(BF16)

    blocks = [slice(r, r + sub) for r in range(0, tm, sub)]
    hbs = [normed(rows) for rows in blocks]
    for rows, hb in zip(blocks, hbs):
        project(rows, hb)


def _in_proj(tok, ctx, mod, gain, wt, wk, norm_args, rope_args, *, even, tm):
    n, s, d = tok.shape
    assert ctx.shape == (1, s, d) and mod.shape[0] == n + 1
    b = n + 1

    def const(a):
        return pl.BlockSpec(a.shape, lambda i, j: (0,) * a.ndim)

    def is_ctx(i):
        return (i == n).astype(jnp.int32)

    in_specs = [pl.BlockSpec((1, tm, d), lambda i, j: (jnp.minimum(i, n - 1), j, 0)),
                pl.BlockSpec((1, tm, d), lambda i, j: (0, is_ctx(i) * j, 0)),
                pl.BlockSpec((1, 1, mod.shape[-1]), lambda i, j: (i, 0, 0)),
                const(gain), const(wt), const(wk)]
    args = [tok, ctx, mod, gain, wt, wk]
    if even:
        gq, gk, seg = norm_args
        in_specs += [const(gq), const(gk), const(seg)]
        args += [gq, gk, seg]
    rope_t, rope_n = rope_args
    in_specs += [pl.BlockSpec((1, 2, 2 * ROPE_FREQS, tm), lambda i, j: (is_ctx(i), 0, 0, j)),
                 pl.BlockSpec((1, 3, tm, PAIR), lambda i, j: (is_ctx(i), 0, j, 0))]
    args += [rope_t, rope_n]

    def vt_out(rows):
        if even:
            return (pl.BlockSpec((1, rows, tm), lambda i, j: (i, 0, j)),
                    jax.ShapeDtypeStruct((b, rows, s), BF16))
        return (pl.BlockSpec((1, tm // LANES, rows, LANES), lambda i, j: (i, j, 0, 0)),
                jax.ShapeDtypeStruct((b, s // LANES, rows, LANES), BF16))

    slot_out = (pl.BlockSpec((1, N_SLOTS, PAIR, tm), lambda i, j: (i, 0, 0, j)),
                jax.ShapeDtypeStruct((b, N_SLOTS, PAIR, s), BF16))

    def nat_out(cols):
        return (pl.BlockSpec((1, tm, cols), lambda i, j: (i, j, 0)), jax.ShapeDtypeStruct((b, s, cols), BF16))

    outs = [slot_out, nat_out(PAIR), vt_out(PAIR), slot_out, nat_out(Q_WIDTH), vt_out(Q_WIDTH)]
    return pl.pallas_call(
        functools.partial(_in_kernel, even=even, n_batch=n, sub_rows=IN_SUB_ROWS),
        grid=(b, s // tm),
        in_specs=in_specs,
        out_specs=[o[0] for o in outs],
        out_shape=[o[1] for o in outs],
        compiler_params=_compiler_params(2),
        name="in_proj_" + ("even" if even else "odd"),
    )(*args)


class AttnJob(NamedTuple):
    query: Callable[[], Any]
    scores: Sequence[Callable[[Any], Any]]
    values: Sequence[Callable[[], Any]]
    finish: Callable[[Any], None]
    sink: Any = None


def _attend_lead(jobs):
    return max(len(job.scores) for job in jobs) + ATTEND_EXTRA_LEAD


def _attend_jobs(jobs, s_ref):
    items = [(j, c) for j, job in enumerate(jobs) for c in range(len(job.scores))]
    lead = _attend_lead(jobs)
    n_slots = s_ref.shape[0]
    assert n_slots > lead
    qt, shape, m, l, o = {}, {}, {}, {}, {}
    for t in range(len(items) + lead):
        if t < len(items):
            j, c = items[t]
            job = jobs[j]
            if c == 0:
                qt[j] = job.query()
            x = job.scores[c](qt[j])
            shape[t] = x.shape
            s_ref[t % n_slots, 0:x.shape[0], 0:x.shape[1]] = x
            part_m = jnp.max(x, axis=0, keepdims=True)
            if c == 0:
                m[j] = part_m if job.sink is None else jnp.maximum(part_m, job.sink)
            else:
                m[j] = jnp.maximum(m[j], part_m)
        if t >= lead:
            j, c = items[t - lead]
            job = jobs[j]
            rows, cols = shape.pop(t - lead)
            e = jnp.exp2(s_ref[(t - lead) % n_slots, 0:rows, 0:cols] - m[j])
            part_l = jnp.sum(e, axis=0, keepdims=True)
            part_o = _dot(job.values[c](), e.astype(BF16))
            l[j] = part_l if c == 0 else l[j] + part_l
            o[j] = part_o if c == 0 else o[j] + part_o
            if c == len(job.scores) - 1:
                denom = l.pop(j) if job.sink is None else l.pop(j) + jnp.exp2(job.sink - m[j])
                job.finish(o.pop(j) * (1.0 / denom))


def _attn_even_kernel(*refs, with_x, lam0, tq, key_chunk):
    if with_x:
        q1_ref, q2_ref, k1c_ref, v1c_ref, k2c_ref, v2c_ref, k1x_ref, v1x_ref, k2x_ref, v2x_ref = refs[:10]
    else:
        q1_ref, q2_ref, k1c_ref, v1c_ref, k2c_ref, v2c_ref = refs[:6]
    lam_ref, sub_ref, o_ref, s_ref = refs[-4:]

    def chunk_fns(c_ref, x_ref, cols, transposed):
        def make(ref, lo, hi):
            if transposed:
                return lambda: ref[0, cols, lo:hi]
            return lambda qt: _dot(ref[0, lo:hi, cols], qt)
        n_ctx = c_ref.shape[2] if transposed else c_ref.shape[1]
        fns = [make(c_ref, 0, n_ctx)]
        if with_x:
            n_x = x_ref.shape[2] if transposed else x_ref.shape[1]
            fns += [make(x_ref, lo, lo + key_chunk) for lo in range(0, n_x, key_chunk)]
        return fns

    group = max(2, EVEN_A_QUERY_COLS // tq)
    jobs = []
    pair_cols = slice(0, PAIR)

    def a_query(g):
        return jnp.concatenate([q1_ref[0, g * group + j] for j in range(group)], axis=1)

    def a_finish(g, o):
        for pp in range(group // 2):
            pair_t = jnp.concatenate([o[0:HEAD_DIM, (2 * pp) * tq:(2 * pp + 1) * tq],
                                      o[HEAD_DIM:PAIR, (2 * pp + 1) * tq:(2 * pp + 2) * tq]], axis=0)
            col = (g * (group // 2) + pp) * PAIR
            o_ref[0, :, col:col + PAIR] = pair_t.T.astype(BF16)

    for g in range(N_SLOTS // group):
        jobs.append(AttnJob(functools.partial(a_query, g),
                            chunk_fns(k1c_ref, k1x_ref if with_x else None, pair_cols, False),
                            chunk_fns(v1c_ref, v1x_ref if with_x else None, pair_cols, True),
                            functools.partial(a_finish, g)))

    def b_query(h):
        return jnp.concatenate([q2_ref[0, 2 * h], q2_ref[0, 2 * h + 1]], axis=1)

    def b_finish(h, o):
        lp = lam_ref[...]
        lam = (jnp.exp(jnp.sum(lp[0:1] * lp[1:2], axis=-1, keepdims=True))
               - jnp.exp(jnp.sum(lp[2:3] * lp[3:4], axis=-1, keepdims=True)) + lam0)
        ob = (o[:, 0:tq] - lam * o[:, tq:2 * tq]).T
        ob = _rms(ob, sub_ref[...]) * (1.0 - lam0)
        col = Q_WIDTH + h * PAIR
        o_ref[0, :, col:col + PAIR] = ob.astype(BF16)

    for h in range(N_SLOTS // 2):
        cols = slice(h * PAIR, (h + 1) * PAIR)
        jobs.append(AttnJob(functools.partial(b_query, h),
                            chunk_fns(k2c_ref, k2x_ref if with_x else None, cols, False),
                            chunk_fns(v2c_ref, v2x_ref if with_x else None, cols, True),
                            functools.partial(b_finish, h)))
    _attend_jobs(jobs, s_ref)


def _ctx_specs(kv, n_ctx, blocked):
    row = kv[0].shape[0] - 1

    def natural(a):
        return pl.BlockSpec((1, n_ctx, a.shape[2]), lambda i, j: (row, i, 0))

    def transposed(a):
        if blocked:
            return pl.BlockSpec((1, n_ctx // LANES) + a.shape[2:], lambda i, j: (row, i, 0, 0))
        return pl.BlockSpec((1, a.shape[1], n_ctx), lambda i, j: (row, 0, i))

    k1, v1, k2, v2 = kv
    return [natural(k1), transposed(v1), natural(k2), transposed(v2)]


def _attn_even(qx, kv, lam_p, subln, *, with_x, n_ctx, lam0, tq):
    q1, q2 = qx
    row = q1.shape[0] - 1
    if with_x:
        b, s = row, q1.shape[3]
        slot_spec = pl.BlockSpec((1, N_SLOTS, PAIR, tq), lambda i, j: (i, 0, 0, j))
    else:
        b, s = q1.shape[3] // n_ctx, n_ctx
        slot_spec = pl.BlockSpec((1, N_SLOTS, PAIR, tq), lambda i, j: (row, 0, 0, i * (n_ctx // tq) + j))

    def full(a):
        return pl.BlockSpec((1,) + a.shape[1:], lambda i, j: (i, 0, 0))

    n_x = kv[0].shape[1]
    key_chunk = min(EVEN_KEY_CHUNK, n_x) if with_x else n_ctx
    chunks_per_job = 1 + (n_x // key_chunk if with_x else 0)
    score_ring = pltpu.VMEM((chunks_per_job + ATTEND_EXTRA_LEAD + 1, max(n_ctx, key_chunk),
                             max(2 * tq, EVEN_A_QUERY_COLS)), F32)
    return pl.pallas_call(
        functools.partial(_attn_even_kernel, with_x=with_x, lam0=lam0, tq=tq, key_chunk=key_chunk),
        grid=(b, s // tq),
        scratch_shapes=[score_ring],
        in_specs=[slot_spec, slot_spec] + _ctx_specs(kv, n_ctx, blocked=False)
                 + ([full(a) for a in kv] if with_x else [])
                 + [pl.BlockSpec(lam_p.shape, lambda i, j: (0, 0)), pl.BlockSpec(subln.shape, lambda i, j: (0, 0))],
        out_specs=pl.BlockSpec((1, tq, 2 * Q_WIDTH), lambda i, j: (i, j, 0)),
        out_shape=jax.ShapeDtypeStruct((b, s, 2 * Q_WIDTH), BF16),
        compiler_params=_compiler_params(2),
        name="attn_even" + ("_x" if with_x else "_ctx"),
    )(q1, q2, *kv, *(kv if with_x else ()), lam_p, subln)


N_BIAS_ROWS = 2 * NA_ROWS


def _bias_kernel(rpb_ref, o_ref):
    head = pl.program_id(0)
    n_dr = 2 * NA_ROWS - 1
    n_dc = 2 * NA_COLS - 1
    shape = (GRID_W, LANES)
    ck = lax.broadcasted_iota(jnp.int32, shape, 0)
    lane = lax.broadcasted_iota(jnp.int32, shape, 1)
    second = lane >= GRID_W
    dcl = jnp.clip(ck - (lane & (GRID_W - 1)), -(NA_COLS - 1), NA_COLS - 1) + (NA_COLS - 1)
    tiles = []
    for d in range(n_dr):
        acc = jnp.zeros(shape, F32)
        for j in range(n_dc):
            acc = jnp.where(dcl == j, rpb_ref[(head * n_dr + d) * n_dc + j], acc)
        tiles.append(acc * LOG2E)
    for t in range(N_BIAS_ROWS):
        o_ref[0, t] = jnp.where(second, tiles[max(t - 1, 0)], tiles[min(t, n_dr - 1)])


def _bias_table(rpb):
    heads = rpb.shape[0]
    return pl.pallas_call(
        _bias_kernel,
        grid=(heads,),
        in_specs=[pl.BlockSpec(memory_space=pltpu.SMEM)],
        out_specs=pl.BlockSpec((1, N_BIAS_ROWS, GRID_W, LANES), lambda h: (h, 0, 0, 0)),
        out_shape=jax.ShapeDtypeStruct((heads, N_BIAS_ROWS, GRID_W, LANES), F32),
        compiler_params=_compiler_params(1),
        name="rel_bias_table",
    )(rpb.reshape(-1))


def _attn_odd_kernel(q1_ref, q2_ref, k1x_ref, v1x_ref, k2x_ref, v2x_ref, k1c_ref, v1c_ref, k2c_ref, v2c_ref,
                     sink_ref, tp_ref, win_ref, o_ref, s_ref, *, tq, sink_order):
    i = pl.program_id(1)
    t = k1x_ref.shape[1]
    n_ctx_blk = v1c_ref.shape[1]
    rows = t // GRID_W
    kh = min(NA_ROWS, rows)
    n_blk = tq // LANES

    jobs = []

    n_q = N_SLOTS * C_WINDOW
    n_seq_blk = t // C_WINDOW
    lane_q = lax.broadcasted_iota(jnp.int32, (1, n_q), 1)
    sink = jnp.zeros((1, n_q), F32)
    for j in range(N_SLOTS):
        sink = jnp.where(lane_q >> _log2(C_WINDOW) == j, sink_ref[sink_order[j]], sink)
    sink = sink * LOG2E
    n_qj = C_SLOTS_PER_JOB * C_WINDOW
    tri = (lax.broadcasted_iota(jnp.int32, (C_WINDOW, n_qj), 0)
           - (lax.broadcasted_iota(jnp.int32, (C_WINDOW, n_qj), 1) & (C_WINDOW - 1)))

    def c_blocks(jb):
        blk = i * n_blk + jb
        return blk, jnp.maximum(blk - 1, 0), jnp.minimum(blk + 1, n_seq_blk - 1)

    def c_query(jb, s0):
        return jnp.concatenate([q1_ref[0, j, :, jb * LANES:(jb + 1) * LANES]
                                for j in range(s0, s0 + C_SLOTS_PER_JOB)], axis=1)

    def c_local_scores(jb, qt):
        blk, prev, nxt = c_blocks(jb)
        k_loc = jnp.concatenate([k1x_ref[0, pl.ds(pl.multiple_of(n * C_WINDOW, C_WINDOW), C_WINDOW), :]
                                 for n in (prev, blk, nxt)], axis=0)
        s = _dot(k_loc, qt)
        lo_bound = jnp.where(blk > 0, 0, C_WINDOW)
        hi_bound = jnp.where(blk < n_seq_blk - 1, 0, -C_WINDOW)
        return jnp.concatenate([jnp.where(tri >= lo_bound, s[0:C_WINDOW], NEG),
                                s[C_WINDOW:2 * C_WINDOW],
                                jnp.where(tri <= hi_bound, s[2 * C_WINDOW:3 * C_WINDOW], NEG)], axis=0)

    def c_local_values(jb):
        blk, prev, nxt = c_blocks(jb)
        return jnp.concatenate([v1x_ref[0, n] for n in (prev, blk, nxt)], axis=1)

    def c_finish(jb, s0, o):
        for p in range(C_SLOTS_PER_JOB // 2):
            pair_t = jnp.concatenate([o[0:HEAD_DIM, (2 * p) * LANES:(2 * p + 1) * LANES],
                                      o[HEAD_DIM:PAIR, (2 * p + 1) * LANES:(2 * p + 2) * LANES]], axis=0)
            col = (s0 // 2 + p) * PAIR
            o_ref[0, jb * LANES:(jb + 1) * LANES, col:col + PAIR] = pair_t.T.astype(BF16)

    for jb in range(n_blk):
        for s0 in range(0, N_SLOTS, C_SLOTS_PER_JOB):
            jobs.append(AttnJob(
                functools.partial(c_query, jb, s0),
                [functools.partial(c_local_scores, jb), lambda qt: _dot(k1c_ref[0], qt)],
                [functools.partial(c_local_values, jb),
                 lambda: jnp.concatenate([v1c_ref[0, c] for c in range(n_ctx_blk)], axis=1)],
                functools.partial(c_finish, jb, s0), sink[:, s0 * C_WINDOW:(s0 + C_SLOTS_PER_JOB) * C_WINDOW]))

    n_rows = tq // GRID_W
    span_rows = -(-(n_rows - 1 + kh) // 2) * 2
    n_loc = span_rows * GRID_W
    r0 = i * n_rows
    u = jnp.minimum(jnp.clip(r0 - kh // 2, 0, rows - kh), rows - span_rows)
    q_row = (lax.broadcasted_iota(jnp.int32, (1, 2 * tq), 1) >> _log2(GRID_W)) & (n_rows - 1)
    first = jnp.zeros((1, 2 * tq), jnp.int32)
    for j in range(n_rows):
        first = jnp.where(q_row == j, jnp.clip(r0 + j - kh // 2, 0, rows - kh) - u, first)
    window_row = win_ref[...] - first

    def valid(c):
        w = window_row[c * NA_CHUNK_ROWS * GRID_W:(c + 1) * NA_CHUNK_ROWS * GRID_W]
        return lax.bitcast_convert_type(w, jnp.uint32) < jnp.uint32(kh)

    chunk_rows = NA_CHUNK_ROWS
    chunk_keys = chunk_rows * GRID_W
    blk_rows = LANES // GRID_W

    def d_query(p):
        return jnp.concatenate([q2_ref[0, 2 * p], q2_ref[0, 2 * p + 1]], axis=1)

    def d_local_scores(p, c, qt):
        k_loc = k2x_ref[0, pl.ds(pl.multiple_of((u + c * chunk_rows) * GRID_W, LANES), chunk_keys),
                        p * PAIR:(p + 1) * PAIR]
        bias = jnp.concatenate(
            [jnp.concatenate(
                [tp_ref[2 * p + lb // (n_rows // 2),
                        jnp.clip(u + a - (r0 + 2 * (lb % (n_rows // 2))) + NA_ROWS - 1, 0, N_BIAS_ROWS - 1)]
                 for lb in range(n_rows)], axis=1)
             for a in range(c * chunk_rows, (c + 1) * chunk_rows)], axis=0)
        return jnp.where(valid(c), _dot(k_loc, qt) + bias, NEG)

    def d_local_values(p, c):
        blk0 = (u + c * chunk_rows) // blk_rows
        return jnp.concatenate([v2x_ref[0, blk0 + b, p * PAIR:(p + 1) * PAIR, :]
                                for b in range(chunk_rows // blk_rows)], axis=1)

    def d_finish(p, o):
        pair_t = jnp.concatenate([o[0:HEAD_DIM, 0:tq], o[HEAD_DIM:PAIR, tq:2 * tq]], axis=0)
        o_ref[0, :, Q_WIDTH + p * PAIR:Q_WIDTH + (p + 1) * PAIR] = pair_t.T.astype(BF16)

    for p in range(N_SLOTS // 2):
        cols = slice(p * PAIR, (p + 1) * PAIR)
        n_chunks = span_rows // chunk_rows
        jobs.append(AttnJob(
            functools.partial(d_query, p),
            [functools.partial(d_local_scores, p, c) for c in range(n_chunks)]
            + [lambda qt, cols=cols: _dot(k2c_ref[0, :, cols], qt)],
            [functools.partial(d_local_values, p, c) for c in range(n_chunks)]
            + [lambda cols=cols: jnp.concatenate([v2c_ref[0, b, cols, :] for b in range(n_ctx_blk)], axis=1)],
            functools.partial(d_finish, p)))
    _attend_jobs(jobs, s_ref)


def _attn_odd(qx, kv, sink, tp, *, n_ctx, tq, sink_order):
    q1, q2 = qx
    b, s = q1.shape[0] - 1, q1.shape[3]
    slot_spec = pl.BlockSpec((1, N_SLOTS, PAIR, tq), lambda i, j: (i, 0, 0, j))

    def full(a):
        return pl.BlockSpec((1,) + a.shape[1:], lambda i, j: (i,) + (0,) * (a.ndim - 1))

    rows = s // GRID_W
    span_rows = -(-(tq // GRID_W - 1 + min(NA_ROWS, rows)) // 2) * 2
    chunks_per_job = max(2, span_rows // NA_CHUNK_ROWS + 1)
    key_i = np.arange(span_rows * GRID_W)[:, None]
    cq = np.arange(2 * tq)[None, :] % GRID_W
    cs = np.clip(cq - NA_COLS // 2, 0, GRID_W - NA_COLS)
    ck = key_i % GRID_W
    col_ok = (ck >= cs) & (ck < cs + NA_COLS)
    window = jnp.asarray(key_i // GRID_W + np.where(col_ok, 0, NA_INVALID_COLUMN), jnp.int32)
    score_ring = pltpu.VMEM((chunks_per_job + ATTEND_EXTRA_LEAD + 1, max(3 * C_WINDOW, NA_CHUNK_ROWS * GRID_W, n_ctx),
                             max(N_SLOTS * C_WINDOW, 2 * tq)), F32)
    return pl.pallas_call(
        functools.partial(_attn_odd_kernel, tq=tq, sink_order=sink_order),
        grid=(b, s // tq),
        scratch_shapes=[score_ring],
        in_specs=[slot_spec, slot_spec] + [full(a) for a in kv] + _ctx_specs(kv, n_ctx, blocked=True)
                 + [pl.BlockSpec(memory_space=pltpu.SMEM), pl.BlockSpec(tp.shape, lambda i, j: (0, 0, 0, 0)),
                    pl.BlockSpec(window.shape, lambda i, j: (0, 0))],
        out_specs=pl.BlockSpec((1, tq, 2 * Q_WIDTH), lambda i, j: (i, j, 0)),
        out_shape=jax.ShapeDtypeStruct((b, s, 2 * Q_WIDTH), BF16),
        compiler_params=_compiler_params(2),
        name="attn_odd",
    )(q1, q2, *kv, *kv, sink, tp, window)


def _post_kernel(m_ref, x_ref, mod_ref, g_ref, wo_ref, w1_ref, w2_ref, o_ref, *, ff_chunk, sub_rows):
    d = x_ref.shape[-1]

    def mod(k):
        return mod_ref[0, :, k * d:(k + 1) * d]

    def mixer_residual(rows):
        x1 = x_ref[0, rows] + mod(2) * _rms(_dot(m_ref[0, rows], wo_ref[...]), g_ref[1:2])
        return x1, (_rms(x1, g_ref[2:3]) * (1.0 + mod(4)) + mod(3)).astype(BF16)

    def mlp(h):
        f = None
        for c in range(w1_ref.shape[1] // ff_chunk):
            cols = slice(c * ff_chunk, (c + 1) * ff_chunk)
            a = jnp.maximum(_dot(h, w1_ref[:, cols]), 0.0)
            part = _dot((a * a).astype(BF16), w2_ref[cols, :])
            f = part if f is None else f + part
        return f

    tm = x_ref.shape[1]
    sub = min(sub_rows, tm)
    blocks = [slice(r, r + sub) for r in range(0, tm, sub)]
    stage1 = [mixer_residual(rows) for rows in blocks]
    stage2 = [mlp(h) for _, h in stage1]
    for rows, (x1, _), f in zip(blocks, stage1, stage2):
        o_ref[0, rows] = x1 + mod(5) * _rms(f, g_ref[3:4])


def _post(m, x, mod, gains, wo, w1, w2, *, tm, ff_chunk):
    b, s, d = x.shape
    bm = mod.shape[0]
    mod_map = (lambda i, j: (i, 0, 0)) if bm > 1 else (lambda i, j: (0, 0, 0))

    def const(a):
        return pl.BlockSpec(a.shape, lambda i, j: (0, 0), pipeline_mode=pl.Buffered(1))

    return pl.pallas_call(
        functools.partial(_post_kernel, ff_chunk=ff_chunk, sub_rows=POST_SUB_ROWS),
        grid=(b, s // tm),
        in_specs=[pl.BlockSpec((1, tm, m.shape[-1]), lambda i, j: (i, j, 0)),
                  pl.BlockSpec((1, tm, d), lambda i, j: (i, j, 0)),
                  pl.BlockSpec((1, 1, mod.shape[-1]), mod_map),
                  pl.BlockSpec(gains.shape, lambda i, j: (0, 0)),
                  const(wo), const(w1), const(w2)],
        out_specs=pl.BlockSpec((1, tm, d), lambda i, j: (i, j, 0)),
        out_shape=jax.ShapeDtypeStruct((b, s, d), F32),
        compiler_params=_compiler_params(2),
        name="post",
    )(m, x, mod, gains, wo, w1, w2)


def _rope_tables(t):
    tok = jnp.arange(t, dtype=jnp.int32)
    pos = jnp.stack([tok // GRID_W, tok % GRID_W], axis=-1).astype(F32)
    inv = ROPE_THETA ** (-jnp.arange(ROPE_FREQS, dtype=F32) / ROPE_FREQS)
    ang = pos[..., None] * inv
    cos, sin = jnp.cos(ang), jnp.sin(ang)
    rope_t = jnp.stack([cos.reshape(t, -1).T, sin.reshape(t, -1).T])
    zero = jnp.zeros_like(sin[:, 0])
    c = jnp.concatenate([cos[:, 0], cos[:, 0], cos[:, 1], cos[:, 1]], axis=-1)
    up = jnp.concatenate([-sin[:, 0], zero, -sin[:, 1], zero], axis=-1)
    down = jnp.concatenate([zero, sin[:, 0], zero, sin[:, 1]], axis=-1)
    rope_n = jnp.stack([jnp.tile(a, (1, PAIR // HEAD_DIM)) for a in (c, up, down)])
    one_t, one_n = jnp.ones_like(rope_t[0]), jnp.ones_like(rope_n[0])
    identity_t = jnp.stack([one_t, jnp.zeros_like(one_t)])
    identity_n = jnp.stack([one_n, jnp.zeros_like(one_n), jnp.zeros_like(one_n)])
    return jnp.stack([rope_t, identity_t]), jnp.stack([rope_n, identity_n])


def _pair_gqa_heads(w, axis):
    shape = w.shape
    w = w.reshape(shape[:axis] + (2, N_SLOTS // 2, HEAD_DIM) + shape[axis + 1:])
    return jnp.swapaxes(w, axis, axis + 1).reshape(shape)


def _lambda_init(layer):
    return 0.8 - 0.6 * math.exp(-0.3 * layer)


def kernel(x, c, ctx, c_ctx, w_mod, b_mod, norm_g, w_in, w_out, w_mlp_in, w_mlp_out, qk_norm_a, diff_lambda,
           diff_subln, sink_c, rpb_d):
    b, t, d = x.shape
    n_ctx = ctx.shape[1]
    depth = w_mod.shape[0]
    mod_rows = 16
    cc = jnp.concatenate([c, c_ctx[None], jnp.zeros((mod_rows - b - 1, d), F32)], axis=0)
    mod_all = _modulation(cc, w_mod, b_mod)

    rope_args = _rope_tables(t)
    seg = jnp.kron(jnp.eye(2, dtype=F32), jnp.ones((HEAD_DIM, HEAD_DIM), F32)).astype(BF16)
    tm_x = 2 * IN_SUB_ROWS
    assert b * n_ctx == t
    ctx = ctx.reshape(1, b * n_ctx, d)

    for l in range(depth):
        last = l == depth - 1
        even = l % 2 == 0
        i = l // 2
        mod_x = mod_all[l, :b][:, None, :]
        mod_c = mod_all[l, b][None, None, :]
        w = w_in[l]
        o_k1, o_v1, o_q2, o_k2, o_v2 = Q_WIDTH, Q_WIDTH + PAIR, Q_WIDTH + 2 * PAIR, 2 * Q_WIDTH + 2 * PAIR, \
            3 * Q_WIDTH + 2 * PAIR
        wt = jnp.concatenate([_pair_gqa_heads(w[:, 0:o_k1], 1), w[:, o_v1:o_q2], w[:, o_q2:o_k2], w[:, o_v2:]],
                             axis=1).T.astype(BF16)
        wk = jnp.concatenate([w[:, o_k1:o_v1], w[:, o_k2:o_v2]], axis=1).astype(BF16)
        wo = jnp.concatenate([_pair_gqa_heads(w_out[l, 0:Q_WIDTH], 0), w_out[l, Q_WIDTH:]], axis=0).astype(BF16)
        w1 = w_mlp_in[l].astype(BF16)
        w2 = w_mlp_out[l].astype(BF16)
        g0 = norm_g[l, 0][None, :]

        def norm_args(tm):
            if not even:
                return None
            return (jnp.broadcast_to(qk_norm_a[i, 0][:, None], (HEAD_DIM, tm)),
                    jnp.tile(qk_norm_a[i, 1], PAIR // HEAD_DIM)[None, :], seg)

        mod_xc = mod_all[l, :b + 1][:, None, :]
        q1, k1, v1, q2, k2, v2 = _in_proj(x, ctx, mod_xc, g0, wt, wk, norm_args(tm_x), rope_args,
                                          even=even, tm=tm_x)
        qx, kv = (q1, q2), (k1, v1, k2, v2)
        if even:
            lam0 = _lambda_init(l)
            sub = diff_subln[i][None, :]
            m_x = _attn_even(qx, kv, diff_lambda[i], sub, with_x=True, n_ctx=n_ctx, lam0=lam0, tq=256)
            if not last:
                m_c = _attn_even(qx, kv, diff_lambda[i], sub, with_x=False, n_ctx=n_ctx, lam0=lam0, tq=256)
        else:
            if not last:
                raise NotImplementedError("context update after an odd layer is not needed for depth 2")
            tp = _bias_table(rpb_d[i])
            m_x = _attn_odd(qx, kv, sink_c[i], tp, n_ctx=n_ctx, tq=256, sink_order=GQA_SLOT_HEADS)
        x = _post(m_x, x, mod_x, norm_g[l], wo, w1, w2, tm=512, ff_chunk=1024)
        if not last:
            ctx = _post(m_c.reshape(1, b * n_ctx, -1), ctx, mod_c, norm_g[l], wo, w1, w2, tm=512, ff_chunk=1024)
    return x
```

```python
import functools
import math
from typing import Any, Callable, NamedTuple, Sequence

import numpy as np
import jax
import jax.numpy as jnp
from jax import lax
from jax.experimental import pallas as pl
from jax.experimental.pallas import tpu as pltpu

F32 = jnp.float32
BF16 = jnp.bfloat16

GRID_W = 64
HEAD_DIM = 64
ROPE_THETA = 10000.0
ROPE_FREQS = HEAD_DIM // 4
N_SLOTS = 8
PAIR = 2 * HEAD_DIM
Q_WIDTH = N_SLOTS * HEAD_DIM
LANES = 128
MXU_QUERY_COLS = 512
EVEN_A_QUERY_COLS = MXU_QUERY_COLS
EVEN_KEY_CHUNK = 1024
ATTEND_EXTRA_LEAD = 2
IN_SUB_ROWS = MXU_QUERY_COLS
POST_SUB_ROWS = 256
C_SLOTS_PER_JOB = 8
NA_CHUNK_ROWS = 4
NA_INVALID_COLUMN = 1 << 20
C_WINDOW = 128
NA_ROWS = 8
NA_COLS = 16
EPS = 1e-6
NEG = -1e30
LOG2E = math.log2(math.e)
Q_SCALE = HEAD_DIM ** -0.5 * LOG2E
GQA_SLOT_HEADS = tuple(half * (N_SLOTS // 2) + p for p in range(N_SLOTS // 2) for half in range(2))

VMEM_LIMIT_BYTES = 56 * 1024 * 1024


def _compiler_params(n_grid_dims):
    return pltpu.CompilerParams(dimension_semantics=("arbitrary",) * n_grid_dims,
                                vmem_limit_bytes=VMEM_LIMIT_BYTES)


def _dot(a, b):
    return jnp.dot(a, b, preferred_element_type=F32)


def _dot_nt(a, b):
    return lax.dot_general(a, b, (((1,), (1,)), ((), ())), preferred_element_type=F32)


def _log2(n):
    assert n & (n - 1) == 0
    return n.bit_length() - 1


def _rms(x, gain):
    ms = jnp.mean(x * x, axis=-1, keepdims=True)
    return x * lax.rsqrt(ms + EPS) * gain


def _mod_kernel(c_ref, w_ref, b_ref, o_ref):
    c = c_ref[...]
    s = c * jax.nn.sigmoid(c)
    o_ref[0] = _dot(s.astype(BF16), w_ref[0].astype(BF16)) + b_ref[0]


def _modulation(cc, w_mod, b_mod):
    depth, d, n = w_mod.shape
    rows = cc.shape[0]
    tn = n // 4
    return pl.pallas_call(
        _mod_kernel,
        grid=(depth, n // tn),
        in_specs=[pl.BlockSpec((rows, d), lambda l, j: (0, 0)),
                  pl.BlockSpec((1, d, tn), lambda l, j: (l, 0, j)),
                  pl.BlockSpec((1, 1, tn), lambda l, j: (l, 0, j))],
        out_specs=pl.BlockSpec((1, rows, tn), lambda l, j: (l, 0, j)),
        out_shape=jax.ShapeDtypeStruct((depth, rows, n), F32),
        compiler_params=_compiler_params(2),
        name="modulation",
    )(cc, w_mod, b_mod.reshape(depth, 1, n))


def _in_kernel(*refs, even, n_batch, sub_rows):
    refs = list(refs)
    tok_ref, ctx_ref, mod_ref, g_ref, wt_ref, wk_ref = refs[:6]
    del refs[:6]
    if even:
        gq_ref, gk_ref, seg_ref = refs[:3]
        del refs[:3]
    rope_t_ref, rope_ref = refs[:2]
    q1_ref, k1_ref, v1_ref, q2_ref, k2_ref, v2_ref = refs[2:]
    d = tok_ref.shape[-1]
    tm = tok_ref.shape[1]
    sub = min(sub_rows, tm)
    is_ctx = pl.program_id(0) == n_batch

    def normed(rows):
        h = _rms(jnp.where(is_ctx, ctx_ref[0, rows], tok_ref[0, rows]), g_ref[...])
        h = h * (1.0 + mod_ref[0, :, d:2 * d]) + mod_ref[0, :, 0:d]
        return h.astype(BF16)

    def project(rows, hb):
        def rotary_t(c):
            f = ROPE_FREQS
            out = []
            for a in range(2):
                x1, x2 = c[2 * a * f:(2 * a + 1) * f], c[(2 * a + 1) * f:(2 * a + 2) * f]
                cos, sin = rope_t_ref[0, 0, a * f:(a + 1) * f, rows], rope_t_ref[0, 1, a * f:(a + 1) * f, rows]
                out += [x1 * cos - x2 * sin, x2 * cos + x1 * sin]
            return jnp.concatenate(out, axis=0)

        def store_slots(q_ref, qt, use_norm, use_rope):
            zeros = jnp.zeros((HEAD_DIM, sub), F32)
            for j in range(N_SLOTS):
                c = qt[j * HEAD_DIM:(j + 1) * HEAD_DIM]
                if use_norm:
                    ms = jnp.mean(c * c, axis=0, keepdims=True)
                    c = c * lax.rsqrt(ms + EPS) * gq_ref[:, 0:sub]
                if use_rope:
                    c = rotary_t(c)
                c = c * Q_SCALE
                slot = jnp.concatenate([c, zeros] if j % 2 == 0 else [zeros, c], axis=0)
                q_ref[0, j, :, rows] = slot.astype(BF16)

        def store_vt(v_ref, vt):
            if even:
                v_ref[0, :, rows] = vt.astype(BF16)
            else:
                for c in range(sub // LANES):
                    v_ref[0, rows.start // LANES + c] = vt[:, c * LANES:(c + 1) * LANES].astype(BF16)

        def rotary(c):
            up = pltpu.roll(c, PAIR - ROPE_FREQS, axis=1)
            down = pltpu.roll(c, ROPE_FREQS, axis=1)
            return c * rope_ref[0, 0, rows] + up * rope_ref[0, 1, rows] + down * rope_ref[0, 2, rows]

        o_v1, o_q2, o_v2 = Q_WIDTH, Q_WIDTH + PAIR, 2 * Q_WIDTH + PAIR
        project_keys(rows, hb, rotary)
        store_slots(q1_ref, _dot_nt(wt_ref[0:o_v1], hb), even, True)
        store_slots(q2_ref, _dot_nt(wt_ref[o_q2:o_v2], hb), False, even)
        store_vt(v1_ref, _dot_nt(wt_ref[o_v1:o_q2], hb))
        store_vt(v2_ref, _dot_nt(wt_ref[o_v2:o_v2 + Q_WIDTH], hb))

    def project_keys(rows, hb, rotary):
        k = _dot(hb, wk_ref[...])
        k1 = k[:, 0:PAIR]
        if even:
            sq = k1 * k1
            hi = sq.astype(BF16)
            lo = (sq - hi.astype(F32)).astype(BF16)
            ms = (_dot(hi, seg_ref[...]) + _dot(lo, seg_ref[...])) * (1.0 / HEAD_DIM)
            k1 = k1 * lax.rsqrt(ms + EPS) * gk_ref[...]
        k1 = rotary(k1)
        k1_ref[0, rows] = k1.astype(BF16)
        for j in range(Q_WIDTH // PAIR):
            c = k[:, PAIR + j * PAIR:PAIR + (j + 1) * PAIR]
            if even:
                c = rotary(c)
            k2_ref[0, rows, j * PAIR:(j + 1) * PAIR] = c.astype(BF16)

    blocks = [slice(r, r + sub) for r in range(0, tm, sub)]
    hbs = [normed(rows) for rows in blocks]
    for rows, hb in zip(blocks, hbs):
        project(rows, hb)


def _in_proj(tok, ctx, mod, gain, wt, wk, norm_args, rope_args, *, even, tm):
    n, s, d = tok.shape
    assert ctx.shape == (1, s, d) and mod.shape[0] == n + 1
    b = n + 1

    def const(a):
        return pl.BlockSpec(a.shape, lambda i, j: (0,) * a.ndim)

    def is_ctx(i):
        return (i == n).astype(jnp.int32)

    in_specs = [pl.BlockSpec((1, tm, d), lambda i, j: (jnp.minimum(i, n - 1), j, 0)),
                pl.BlockSpec((1, tm, d), lambda i, j: (0, is_ctx(i) * j, 0)),
                pl.BlockSpec((1, 1, mod.shape[-1]), lambda i, j: (i, 0, 0)),
                const(gain), const(wt), const(wk)]
    args = [tok, ctx, mod, gain, wt, wk]
    if even:
        gq, gk, seg = norm_args
        in_specs += [const(gq), const(gk), const(seg)]
        args += [gq, gk, seg]
    rope_t, rope_n = rope_args
    in_specs += [pl.BlockSpec((1, 2, 2 * ROPE_FREQS, tm), lambda i, j: (is_ctx(i), 0, 0, j)),
                 pl.BlockSpec((1, 3, tm, PAIR), lambda i, j: (is_ctx(i), 0, j, 0))]
    args += [rope_t, rope_n]

    def vt_out(rows):
        if even:
            return (pl.BlockSpec((1, rows, tm), lambda i, j: (i, 0, j)),
                    jax.ShapeDtypeStruct((b, rows, s), BF16))
        return (pl.BlockSpec((1, tm // LANES, rows, LANES), lambda i, j: (i, j, 0, 0)),
                jax.ShapeDtypeStruct((b, s // LANES, rows, LANES), BF16))

    slot_out = (pl.BlockSpec((1, N_SLOTS, PAIR, tm), lambda i, j: (i, 0, 0, j)),
                jax.ShapeDtypeStruct((b, N_SLOTS, PAIR, s), BF16))

    def nat_out(cols):
        return (pl.BlockSpec((1, tm, cols), lambda i, j: (i, j, 0)), jax.ShapeDtypeStruct((b, s, cols), BF16))

    outs = [slot_out, nat_out(PAIR), vt_out(PAIR), slot_out, nat_out(Q_WIDTH), vt_out(Q_WIDTH)]
    return pl.pallas_call(
        functools.partial(_in_kernel, even=even, n_batch=n, sub_rows=IN_SUB_ROWS),
        grid=(b, s // tm),
        in_specs=in_specs,
        out_specs=[o[0] for o in outs],
        out_shape=[o[1] for o in outs],
        compiler_params=_compiler_params(2),
        name="in_proj_" + ("even" if even else "odd"),
    )(*args)


class AttnJob(NamedTuple):
    query: Callable[[], Any]
    scores: Sequence[Callable[[Any], Any]]
    values: Sequence[Callable[[], Any]]
    finish: Callable[[Any], None]
    sink: Any = None


def _attend_lead(jobs):
    return max(len(job.scores) for job in jobs) + ATTEND_EXTRA_LEAD


def _attend_jobs(jobs, s_ref):
    items = [(j, c) for j, job in enumerate(jobs) for c in range(len(job.scores))]
    lead = _attend_lead(jobs)
    n_slots = s_ref.shape[0]
    assert n_slots > lead
    qt, shape, m, l, o = {}, {}, {}, {}, {}
    for t in range(len(items) + lead):
        if t < len(items):
            j, c = items[t]
            job = jobs[j]
            if c == 0:
                qt[j] = job.query()
            x = job.scores[c](qt[j])
            shape[t] = x.shape
            s_ref[t % n_slots, 0:x.shape[0], 0:x.shape[1]] = x
            part_m = jnp.max(x, axis=0, keepdims=True)
            if c == 0:
                m[j] = part_m if job.sink is None else jnp.maximum(part_m, job.sink)
            else:
                m[j] = jnp.maximum(m[j], part_m)
        if t >= lead:
            j, c = items[t - lead]
            job = jobs[j]
            rows, cols = shape.pop(t - lead)
            e = jnp.exp2(s_ref[(t - lead) % n_slots, 0:rows, 0:cols] - m[j])
            part_l = jnp.sum(e, axis=0, keepdims=True)
            part_o = _dot(job.values[c](), e.astype(BF16))
            l[j] = part_l if c == 0 else l[j] + part_l
            o[j] = part_o if c == 0 else o[j] + part_o
            if c == len(job.scores) - 1:
                denom = l.pop(j) if job.sink is None else l.pop(j) + jnp.exp2(job.sink - m[j])
                job.finish(o.pop(j) * (1.0 / denom))


def _attn_even_kernel(*refs, with_x, lam0, tq, key_chunk):
    if with_x:
        q1_ref, q2_ref, k1c_ref, v1c_ref, k2c_ref, v2c_ref, k1x_ref, v1x_ref, k2x_ref, v2x_ref = refs[:10]
    else:
        q1_ref, q2_ref, k1c_ref, v1c_ref, k2c_ref, v2c_ref = refs[:6]
    lam_ref, sub_ref, o_ref, s_ref = refs[-4:]

    def chunk_fns(c_ref, x_ref, cols, transposed):
        def make(ref, lo, hi):
            if transposed:
                return lambda: ref[0, cols, lo:hi]
            return lambda qt: _dot(ref[0, lo:hi, cols], qt)
        n_ctx = c_ref.shape[2] if transposed else c_ref.shape[1]
        fns = [make(c_ref, 0, n_ctx)]
        if with_x:
            n_x = x_ref.shape[2] if transposed else x_ref.shape[1]
            fns += [make(x_ref, lo, lo + key_chunk) for lo in range(0, n_x, key_chunk)]
        return fns

    group = max(2, EVEN_A_QUERY_COLS // tq)
    jobs = []
    pair_cols = slice(0, PAIR)

    def a_query(g):
        return jnp.concatenate([q1_ref[0, g * group + j] for j in range(group)], axis=1)

    def a_finish(g, o):
        for pp in range(group // 2):
            pair_t = jnp.concatenate([o[0:HEAD_DIM, (2 * pp) * tq:(2 * pp + 1) * tq],
                                      o[HEAD_DIM:PAIR, (2 * pp + 1) * tq:(2 * pp + 2) * tq]], axis=0)
            col = (g * (group // 2) + pp) * PAIR
            o_ref[0, :, col:col + PAIR] = pair_t.T.astype(BF16)

    for g in range(N_SLOTS // group):
        jobs.append(AttnJob(functools.partial(a_query, g),
                            chunk_fns(k1c_ref, k1x_ref if with_x else None, pair_cols, False),
                            chunk_fns(v1c_ref, v1x_ref if with_x else None, pair_cols, True),
                            functools.partial(a_finish, g)))

    def b_query(h):
        return jnp.concatenate([q2_ref[0, 2 * h], q2_ref[0, 2 * h + 1]], axis=1)

    def b_finish(h, o):
        lp = lam_ref[...]
        lam = (jnp.exp(jnp.sum(lp[0:1] * lp[1:2], axis=-1, keepdims=True))
               - jnp.exp(jnp.sum(lp[2:3] * lp[3:4], axis=-1, keepdims=True)) + lam0)
        ob = (o[:, 0:tq] - lam * o[:, tq:2 * tq]).T
        ob = _rms(ob, sub_ref[...]) * (1.0 - lam0)
        col = Q_WIDTH + h * PAIR
        o_ref[0, :, col:col + PAIR] = ob.astype(BF16)

    for h in range(N_SLOTS // 2):
        cols = slice(h * PAIR, (h + 1) * PAIR)
        jobs.append(AttnJob(functools.partial(b_query, h),
                            chunk_fns(k2c_ref, k2x_ref if with_x else None, cols, False),
                            chunk_fns(v2c_ref, v2x_ref if with_x else None, cols, True),
                            functools.partial(b_finish, h)))
    _attend_jobs(jobs, s_ref)


def _ctx_specs(kv, n_ctx, blocked):
    row = kv[0].shape[0] - 1

    def natural(a):
        return pl.BlockSpec((1, n_ctx, a.shape[2]), lambda i, j: (row, i, 0))

    def transposed(a):
        if blocked:
            return pl.BlockSpec((1, n_ctx // LANES) + a.shape[2:], lambda i, j: (row, i, 0, 0))
        return pl.BlockSpec((1, a.shape[1], n_ctx), lambda i, j: (row, 0, i))

    k1, v1, k2, v2 = kv
    return [natural(k1), transposed(v1), natural(k2), transposed(v2)]


def _attn_even(qx, kv, lam_p, subln, *, with_x, n_ctx, lam0, tq):
    q1, q2 = qx
    row = q1.shape[0] - 1
    if with_x:
        b, s = row, q1.shape[3]
        slot_spec = pl.BlockSpec((1, N_SLOTS, PAIR, tq), lambda i, j: (i, 0, 0, j))
    else:
        b, s = q1.shape[3] // n_ctx, n_ctx
        slot_spec = pl.BlockSpec((1, N_SLOTS, PAIR, tq), lambda i, j: (row, 0, 0, i * (n_ctx // tq) + j))

    def full(a):
        return pl.BlockSpec((1,) + a.shape[1:], lambda i, j: (i, 0, 0))

    n_x = kv[0].shape[1]
    key_chunk = min(EVEN_KEY_CHUNK, n_x) if with_x else n_ctx
    chunks_per_job = 1 + (n_x // key_chunk if with_x else 0)
    score_ring = pltpu.VMEM((chunks_per_job + ATTEND_EXTRA_LEAD + 1, max(n_ctx, key_chunk),
                             max(2 * tq, EVEN_A_QUERY_COLS)), F32)
    return pl.pallas_call(
        functools.partial(_attn_even_kernel, with_x=with_x, lam0=lam0, tq=tq, key_chunk=key_chunk),
        grid=(b, s // tq),
        scratch_shapes=[score_ring],
        in_specs=[slot_spec, slot_spec] + _ctx_specs(kv, n_ctx, blocked=False)
                 + ([full(a) for a in kv] if with_x else [])
                 + [pl.BlockSpec(lam_p.shape, lambda i, j: (0, 0)), pl.BlockSpec(subln.shape, lambda i, j: (0, 0))],
        out_specs=pl.BlockSpec((1, tq, 2 * Q_WIDTH), lambda i, j: (i, j, 0)),
        out_shape=jax.ShapeDtypeStruct((b, s, 2 * Q_WIDTH), BF16),
        compiler_params=_compiler_params(2),
        name="attn_even" + ("_x" if with_x else "_ctx"),
    )(q1, q2, *kv, *(kv if with_x else ()), lam_p, subln)


N_BIAS_ROWS = 2 * NA_ROWS


def _bias_kernel(rpb_ref, o_ref):
    head = pl.program_id(0)
    n_dr = 2 * NA_ROWS - 1
    n_dc = 2 * NA_COLS - 1
    shape = (GRID_W, LANES)
    ck = lax.broadcasted_iota(jnp.int32, shape, 0)
    lane = lax.broadcasted_iota(jnp.int32, shape, 1)
    second = lane >= GRID_W
    dcl = jnp.clip(ck - (lane & (GRID_W - 1)), -(NA_COLS - 1), NA_COLS - 1) + (NA_COLS - 1)
    tiles = []
    for d in range(n_dr):
        acc = jnp.zeros(shape, F32)
        for j in range(n_dc):
            acc = jnp.where(dcl == j, rpb_ref[(head * n_dr + d) * n_dc + j], acc)
        tiles.append(acc * LOG2E)
    for t in range(N_BIAS_ROWS):
        o_ref[0, t] = jnp.where(second, tiles[max(t - 1, 0)], tiles[min(t, n_dr - 1)])


def _bias_table(rpb):
    heads = rpb.shape[0]
    return pl.pallas_call(
        _bias_kernel,
        grid=(heads,),
        in_specs=[pl.BlockSpec(memory_space=pltpu.SMEM)],
        out_specs=pl.BlockSpec((1, N_BIAS_ROWS, GRID_W, LANES), lambda h: (h, 0, 0, 0)),
        out_shape=jax.ShapeDtypeStruct((heads, N_BIAS_ROWS, GRID_W, LANES), F32),
        compiler_params=_compiler_params(1),
        name="rel_bias_table",
    )(rpb.reshape(-1))


def _attn_odd_kernel(q1_ref, q2_ref, k1x_ref, v1x_ref, k2x_ref, v2x_ref, k1c_ref, v1c_ref, k2c_ref, v2c_ref,
                     sink_ref, tp_ref, win_ref, o_ref, s_ref, *, tq, sink_order):
    i = pl.program_id(1)
    t = k1x_ref.shape[1]
    n_ctx_blk = v1c_ref.shape[1]
    rows = t // GRID_W
    kh = min(NA_ROWS, rows)
    n_blk = tq // LANES

    jobs = []

    n_q = N_SLOTS * C_WINDOW
    n_seq_blk = t // C_WINDOW
    lane_q = lax.broadcasted_iota(jnp.int32, (1, n_q), 1)
    sink = jnp.zeros((1, n_q), F32)
    for j in range(N_SLOTS):
        sink = jnp.where(lane_q >> _log2(C_WINDOW) == j, sink_ref[sink_order[j]], sink)
    sink = sink * LOG2E
    n_qj = C_SLOTS_PER_JOB * C_WINDOW
    tri = (lax.broadcasted_iota(jnp.int32, (C_WINDOW, n_qj), 0)
           - (lax.broadcasted_iota(jnp.int32, (C_WINDOW, n_qj), 1) & (C_WINDOW - 1)))

    def c_blocks(jb):
        blk = i * n_blk + jb
        return blk, jnp.maximum(blk - 1, 0), jnp.minimum(blk + 1, n_seq_blk - 1)

    def c_query(jb, s0):
        return jnp.concatenate([q1_ref[0, j, :, jb * LANES:(jb + 1) * LANES]
                                for j in range(s0, s0 + C_SLOTS_PER_JOB)], axis=1)

    def c_local_scores(jb, qt):
        blk, prev, nxt = c_blocks(jb)
        k_loc = jnp.concatenate([k1x_ref[0, pl.ds(pl.multiple_of(n * C_WINDOW, C_WINDOW), C_WINDOW), :]
                                 for n in (prev, blk, nxt)], axis=0)
        s = _dot(k_loc, qt)
        lo_bound = jnp.where(blk > 0, 0, C_WINDOW)
        hi_bound = jnp.where(blk < n_seq_blk - 1, 0, -C_WINDOW)
        return jnp.concatenate([jnp.where(tri >= lo_bound, s[0:C_WINDOW], NEG),
                                s[C_WINDOW:2 * C_WINDOW],
                                jnp.where(tri <= hi_bound, s[2 * C_WINDOW:3 * C_WINDOW], NEG)], axis=0)

    def c_local_values(jb):
        blk, prev, nxt = c_blocks(jb)
        return jnp.concatenate([v1x_ref[0, n] for n in (prev, blk, nxt)], axis=1)

    def c_finish(jb, s0, o):
        for p in range(C_SLOTS_PER_JOB // 2):
            pair_t = jnp.concatenate([o[0:HEAD_DIM, (2 * p) * LANES:(2 * p + 1) * LANES],
                                      o[HEAD_DIM:PAIR, (2 * p + 1) * LANES:(2 * p + 2) * LANES]], axis=0)
            col = (s0 // 2 + p) * PAIR
            o_ref[0, jb * LANES:(jb + 1) * LANES, col:col + PAIR] = pair_t.T.astype(BF16)

    for jb in range(n_blk):
        for s0 in range(0, N_SLOTS, C_SLOTS_PER_JOB):
            jobs.append(AttnJob(
                functools.partial(c_query, jb, s0),
                [functools.partial(c_local_scores, jb), lambda qt: _dot(k1c_ref[0], qt)],
                [functools.partial(c_local_values, jb),
                 lambda: jnp.concatenate([v1c_ref[0, c] for c in range(n_ctx_blk)], axis=1)],
                functools.partial(c_finish, jb, s0), sink[:, s0 * C_WINDOW:(s0 + C_SLOTS_PER_JOB) * C_WINDOW]))

    n_rows = tq // GRID_W
    span_rows = -(-(n_rows - 1 + kh) // 2) * 2
    n_loc = span_rows * GRID_W
    r0 = i * n_rows
    u = jnp.minimum(jnp.clip(r0 - kh // 2, 0, rows - kh), rows - span_rows)
    q_row = (lax.broadcasted_iota(jnp.int32, (1, 2 * tq), 1) >> _log2(GRID_W)) & (n_rows - 1)
    first = jnp.zeros((1, 2 * tq), jnp.int32)
    for j in range(n_rows):
        first = jnp.where(q_row == j, jnp.clip(r0 + j - kh // 2, 0, rows - kh) - u, first)
    window_row = win_ref[...] - first

    def valid(c):
        w = window_row[c * NA_CHUNK_ROWS * GRID_W:(c + 1) * NA_CHUNK_ROWS * GRID_W]
        return lax.bitcast_convert_type(w, jnp.uint32) < jnp.uint32(kh)

    chunk_rows = NA_CHUNK_ROWS
    chunk_keys = chunk_rows * GRID_W
    blk_rows = LANES // GRID_W

    def d_query(p):
        return jnp.concatenate([q2_ref[0, 2 * p], q2_ref[0, 2 * p + 1]], axis=1)

    def d_local_scores(p, c, qt):
        k_loc = k2x_ref[0, pl.ds(pl.multiple_of((u + c * chunk_rows) * GRID_W, LANES), chunk_keys),
                        p * PAIR:(p + 1) * PAIR]
        bias = jnp.concatenate(
            [jnp.concatenate(
                [tp_ref[2 * p + lb // (n_rows // 2),
                        jnp.clip(u + a - (r0 + 2 * (lb % (n_rows // 2))) + NA_ROWS - 1, 0, N_BIAS_ROWS - 1)]
                 for lb in range(n_rows)], axis=1)
             for a in range(c * chunk_rows, (c + 1) * chunk_rows)], axis=0)
        return jnp.where(valid(c), _dot(k_loc, qt) + bias, NEG)

    def d_local_values(p, c):
        blk0 = (u + c * chunk_rows) // blk_rows
        return jnp.concatenate([v2x_ref[0, blk0 + b, p * PAIR:(p + 1) * PAIR, :]
                                for b in range(chunk_rows // blk_rows)], axis=1)

    def d_finish(p, o):
        pair_t = jnp.concatenate([o[0:HEAD_DIM, 0:tq], o[HEAD_DIM:PAIR, tq:2 * tq]], axis=0)
        o_ref[0, :, Q_WIDTH + p * PAIR:Q_WIDTH + (p + 1) * PAIR] = pair_t.T.astype(BF16)

    for p in range(N_SLOTS // 2):
        cols = slice(p * PAIR, (p + 1) * PAIR)
        n_chunks = span_rows // chunk_rows
        jobs.append(AttnJob(
            functools.partial(d_query, p),
            [functools.partial(d_local_scores, p, c) for c in range(n_chunks)]
            + [lambda qt, cols=cols: _dot(k2c_ref[0, :, cols], qt)],
            [functools.partial(d_local_values, p, c) for c in range(n_chunks)]
            + [lambda cols=cols: jnp.concatenate([v2c_ref[0, b, cols, :] for b in range(n_ctx_blk)], axis=1)],
            functools.partial(d_finish, p)))
    _attend_jobs(jobs, s_ref)


def _attn_odd(qx, kv, sink, tp, *, n_ctx, tq, sink_order):
    q1, q2 = qx
    b, s = q1.shape[0] - 1, q1.shape[3]
    slot_spec = pl.BlockSpec((1, N_SLOTS, PAIR, tq), lambda i, j: (i, 0, 0, j))

    def full(a):
        return pl.BlockSpec((1,) + a.shape[1:], lambda i, j: (i,) + (0,) * (a.ndim - 1))

    rows = s // GRID_W
    span_rows = -(-(tq // GRID_W - 1 + min(NA_ROWS, rows)) // 2) * 2
    chunks_per_job = max(2, span_rows // NA_CHUNK_ROWS + 1)
    key_i = np.arange(span_rows * GRID_W)[:, None]
    cq = np.arange(2 * tq)[None, :] % GRID_W
    cs = np.clip(cq - NA_COLS // 2, 0, GRID_W - NA_COLS)
    ck = key_i % GRID_W
    col_ok = (ck >= cs) & (ck < cs + NA_COLS)
    window = jnp.asarray(key_i // GRID_W + np.where(col_ok, 0, NA_INVALID_COLUMN), jnp.int32)
    score_ring = pltpu.VMEM((chunks_per_job + ATTEND_EXTRA_LEAD + 1, max(3 * C_WINDOW, NA_CHUNK_ROWS * GRID_W, n_ctx),
                             max(N_SLOTS * C_WINDOW, 2 * tq)), F32)
    return pl.pallas_call(
        functools.partial(_attn_odd_kernel, tq=tq, sink_order=sink_order),
        grid=(b, s // tq),
        scratch_shapes=[score_ring],
        in_specs=[slot_spec, slot_spec] + [full(a) for a in kv] + _ctx_specs(kv, n_ctx, blocked=True)
                 + [pl.BlockSpec(memory_space=pltpu.SMEM), pl.BlockSpec(tp.shape, lambda i, j: (0, 0, 0, 0)),
                    pl.BlockSpec(window.shape, lambda i, j: (0, 0))],
        out_specs=pl.BlockSpec((1, tq, 2 * Q_WIDTH), lambda i, j: (i, j, 0)),
        out_shape=jax.ShapeDtypeStruct((b, s, 2 * Q_WIDTH), BF16),
        compiler_params=_compiler_params(2),
        name="attn_odd",
    )(q1, q2, *kv, *kv, sink, tp, window)


def _post_kernel(m_ref, x_ref, mod_ref, g_ref, wo_ref, w1_ref, w2_ref, o_ref, *, ff_chunk, sub_rows):
    d = x_ref.shape[-1]

    def mod(k):
        return mod_ref[0, :, k * d:(k + 1) * d]

    def mixer_residual(rows):
        x1 = x_ref[0, rows] + mod(2) * _rms(_dot(m_ref[0, rows], wo_ref[...]), g_ref[1:2])
        return x1, (_rms(x1, g_ref[2:3]) * (1.0 + mod(4)) + mod(3)).astype(BF16)

    def mlp(h):
        f = None
        for c in range(w1_ref.shape[1] // ff_chunk):
            cols = slice(c * ff_chunk, (c + 1) * ff_chunk)
            a = jnp.maximum(_dot(h, w1_ref[:, cols]), 0.0)
            part = _dot((a * a).astype(BF16), w2_ref[cols, :])
            f = part if f is None else f + part
        return f

    tm = x_ref.shape[1]
    sub = min(sub_rows, tm)
    blocks = [slice(r, r + sub) for r in range(0, tm, sub)]
    stage1 = [mixer_residual(rows) for rows in blocks]
    stage2 = [mlp(h) for _, h in stage1]
    for rows, (x1, _), f in zip(blocks, stage1, stage2):
        o_ref[0, rows] = x1 + mod(5) * _rms(f, g_ref[3:4])


def _post(m, x, mod, gains, wo, w1, w2, *, tm, ff_chunk):
    b, s, d = x.shape
    bm = mod.shape[0]
    mod_map = (lambda i, j: (i, 0, 0)) if bm > 1 else (lambda i, j: (0, 0, 0))

    def const(a):
        return pl.BlockSpec(a.shape, lambda i, j: (0, 0), pipeline_mode=pl.Buffered(1))

    return pl.pallas_call(
        functools.partial(_post_kernel, ff_chunk=ff_chunk, sub_rows=POST_SUB_ROWS),
        grid=(b, s // tm),
        in_specs=[pl.BlockSpec((1, tm, m.shape[-1]), lambda i, j: (i, j, 0)),
                  pl.BlockSpec((1, tm, d), lambda i, j: (i, j, 0)),
                  pl.BlockSpec((1, 1, mod.shape[-1]), mod_map),
                  pl.BlockSpec(gains.shape, lambda i, j: (0, 0)),
                  const(wo), const(w1), const(w2)],
        out_specs=pl.BlockSpec((1, tm, d), lambda i, j: (i, j, 0)),
        out_shape=jax.ShapeDtypeStruct((b, s, d), F32),
        compiler_params=_compiler_params(2),
        name="post",
    )(m, x, mod, gains, wo, w1, w2)


def _rope_tables(t):
    tok = jnp.arange(t, dtype=jnp.int32)
    pos = jnp.stack([tok // GRID_W, tok % GRID_W], axis=-1).astype(F32)
    inv = ROPE_THETA ** (-jnp.arange(ROPE_FREQS, dtype=F32) / ROPE_FREQS)
    ang = pos[..., None] * inv
    cos, sin = jnp.cos(ang), jnp.sin(ang)
    rope_t = jnp.stack([cos.reshape(t, -1).T, sin.reshape(t, -1).T])
    zero = jnp.zeros_like(sin[:, 0])
    c = jnp.concatenate([cos[:, 0], cos[:, 0], cos[:, 1], cos[:, 1]], axis=-1)
    up = jnp.concatenate([-sin[:, 0], zero, -sin[:, 1], zero], axis=-1)
    down = jnp.concatenate([zero, sin[:, 0], zero, sin[:, 1]], axis=-1)
    rope_n = jnp.stack([jnp.tile(a, (1, PAIR // HEAD_DIM)) for a in (c, up, down)])
    one_t, one_n = jnp.ones_like(rope_t[0]), jnp.ones_like(rope_n[0])
    identity_t = jnp.stack([one_t, jnp.zeros_like(one_t)])
    identity_n = jnp.stack([one_n, jnp.zeros_like(one_n), jnp.zeros_like(one_n)])
    return jnp.stack([rope_t, identity_t]), jnp.stack([rope_n, identity_n])


def _pair_gqa_heads(w, axis):
    shape = w.shape
    w = w.reshape(shape[:axis] + (2, N_SLOTS // 2, HEAD_DIM) + shape[axis + 1:])
    return jnp.swapaxes(w, axis, axis + 1).reshape(shape)


def _lambda_init(layer):
    return 0.8 - 0.6 * math.exp(-0.3 * layer)


def kernel(x, c, ctx, c_ctx, w_mod, b_mod, norm_g, w_in, w_out, w_mlp_in, w_mlp_out, qk_norm_a, diff_lambda,
           diff_subln, sink_c, rpb_d):
    b, t, d = x.shape
    n_ctx = ctx.shape[1]
    depth = w_mod.shape[0]
    mod_rows = 16
    cc = jnp.concatenate([c, c_ctx[None], jnp.zeros((mod_rows - b - 1, d), F32)], axis=0)
    mod_all = _modulation(cc, w_mod, b_mod)

    rope_args = _rope_tables(t)
    seg = jnp.kron(jnp.eye(2, dtype=F32), jnp.ones((HEAD_DIM, HEAD_DIM), F32)).astype(BF16)
    tm_x = 2 * IN_SUB_ROWS
    assert b * n_ctx == t
    ctx = ctx.reshape(1, b * n_ctx, d)

    for l in range(depth):
        last = l == depth - 1
        even = l % 2 == 0
        i = l // 2
        mod_x = mod_all[l, :b][:, None, :]
        mod_c = mod_all[l, b][None, None, :]
        w = w_in[l]
        o_k1, o_v1, o_q2, o_k2, o_v2 = Q_WIDTH, Q_WIDTH + PAIR, Q_WIDTH + 2 * PAIR, 2 * Q_WIDTH + 2 * PAIR, \
            3 * Q_WIDTH + 2 * PAIR
        wt = jnp.concatenate([_pair_gqa_heads(w[:, 0:o_k1], 1), w[:, o_v1:o_q2], w[:, o_q2:o_k2], w[:, o_v2:]],
                             axis=1).T.astype(BF16)
        wk = jnp.concatenate([w[:, o_k1:o_v1], w[:, o_k2:o_v2]], axis=1).astype(BF16)
        wo = jnp.concatenate([_pair_gqa_heads(w_out[l, 0:Q_WIDTH], 0), w_out[l, Q_WIDTH:]], axis=0).astype(BF16)
        w1 = w_mlp_in[l].astype(BF16)
        w2 = w_mlp_out[l].astype(BF16)
        g0 = norm_g[l, 0][None, :]

        def norm_args(tm):
            if not even:
                return None
            return (jnp.broadcast_to(qk_norm_a[i, 0][:, None], (HEAD_DIM, tm)),
                    jnp.tile(qk_norm_a[i, 1], PAIR // HEAD_DIM)[None, :], seg)

        mod_xc = mod_all[l, :b + 1][:, None, :]
        q1, k1, v1, q2, k2, v2 = _in_proj(x, ctx, mod_xc, g0, wt, wk, norm_args(tm_x), rope_args,
                                          even=even, tm=tm_x)
        qx, kv = (q1, q2), (k1, v1, k2, v2)
        if even:
            lam0 = _lambda_init(l)
            sub = diff_subln[i][None, :]
            m_x = _attn_even(qx, kv, diff_lambda[i], sub, with_x=True, n_ctx=n_ctx, lam0=lam0, tq=256)
            if not last:
                m_c = _attn_even(qx, kv, diff_lambda[i], sub, with_x=False, n_ctx=n_ctx, lam0=lam0, tq=256)
        else:
            if not last:
                raise NotImplementedError("context update after an odd layer is not needed for depth 2")
            tp = _bias_table(rpb_d[i])
            m_x = _attn_odd(qx, kv, sink_c[i], tp, n_ctx=n_ctx, tq=256, sink_order=GQA_SLOT_HEADS)
        x = _post(m_x, x, mod_x, norm_g[l], wo, w1, w2, tm=512, ff_chunk=1024)
        if not last:
            ctx = _post(m_c.reshape(1, b * n_ctx, -1), ctx, mod_c, norm_g[l], wo, w1, w2, tm=512, ff_chunk=1024)
    return x
```

```python
import functools
import math
from typing import Any, Callable, NamedTuple, Sequence

import numpy as np
import jax
import jax.numpy as jnp
from jax import lax
from jax.experimental import pallas as pl
from jax.experimental.pallas import tpu as pltpu

F32 = jnp.float32
BF16 = jnp.bfloat16

GRID_W = 64
HEAD_DIM = 64
ROPE_THETA = 10000.0
ROPE_FREQS = HEAD_DIM // 4
N_SLOTS = 8
PAIR = 2 * HEAD_DIM
Q_WIDTH = N_SLOTS * HEAD_DIM
LANES = 128
MXU_QUERY_COLS = 512
EVEN_A_QUERY_COLS = MXU_QUERY_COLS
EVEN_KEY_CHUNK = 1024
ATTEND_EXTRA_LEAD = 2
IN_SUB_ROWS = MXU_QUERY_COLS
POST_SUB_ROWS = 256
C_SLOTS_PER_JOB = 8
NA_CHUNK_ROWS = 4
NA_INVALID_COLUMN = 1 << 20
C_WINDOW = 128
NA_ROWS = 8
NA_COLS = 16
EPS = 1e-6
NEG = -1e30
LOG2E = math.log2(math.e)
Q_SCALE = HEAD_DIM ** -0.5 * LOG2E
GQA_SLOT_HEADS = tuple(half * (N_SLOTS // 2) + p for p in range(N_SLOTS // 2) for half in range(2))

VMEM_LIMIT_BYTES = 56 * 1024 * 1024


def _compiler_params(n_grid_dims):
    return pltpu.CompilerParams(dimension_semantics=("arbitrary",) * n_grid_dims,
                                vmem_limit_bytes=VMEM_LIMIT_BYTES)


def _dot(a, b):
    return jnp.dot(a, b, preferred_element_type=F32)


def _dot_nt(a, b):
    return lax.dot_general(a, b, (((1,), (1,)), ((), ())), preferred_element_type=F32)


def _log2(n):
    assert n & (n - 1) == 0
    return n.bit_length() - 1


def _rms(x, gain):
    ms = jnp.mean(x * x, axis=-1, keepdims=True)
    return x * lax.rsqrt(ms + EPS) * gain


def _mod_kernel(c_ref, w_ref, b_ref, o_ref):
    c = c_ref[...]
    s = c * jax.nn.sigmoid(c)
    o_ref[0] = _dot(s.astype(BF16), w_ref[0].astype(BF16)) + b_ref[0]


def _modulation(cc, w_mod, b_mod):
    depth, d, n = w_mod.shape
    rows = cc.shape[0]
    tn = n // 4
    return pl.pallas_call(
        _mod_kernel,
        grid=(depth, n // tn),
        in_specs=[pl.BlockSpec((rows, d), lambda l, j: (0, 0)),
                  pl.BlockSpec((1, d, tn), lambda l, j: (l, 0, j)),
                  pl.BlockSpec((1, 1, tn), lambda l, j: (l, 0, j))],
        out_specs=pl.BlockSpec((1, rows, tn), lambda l, j: (l, 0, j)),
        out_shape=jax.ShapeDtypeStruct((depth, rows, n), F32),
        compiler_params=_compiler_params(2),
        name="modulation",
    )(cc, w_mod, b_mod.reshape(depth, 1, n))


def _in_kernel(*refs, even, n_batch, sub_rows):
    refs = list(refs)
    tok_ref, ctx_ref, mod_ref, g_ref, wt_ref, wk_ref = refs[:6]
    del refs[:6]
    if even:
        gq_ref, gk_ref, seg_ref = refs[:3]
        del refs[:3]
    rope_t_ref, rope_ref = refs[:2]
    q1_ref, k1_ref, v1_ref, q2_ref, k2_ref, v2_ref = refs[2:]
    d = tok_ref.shape[-1]
    tm = tok_ref.shape[1]
    sub = min(sub_rows, tm)
    is_ctx = pl.program_id(0) == n_batch

    def normed(rows):
        h = _rms(jnp.where(is_ctx, ctx_ref[0, rows], tok_ref[0, rows]), g_ref[...])
        h = h * (1.0 + mod_ref[0, :, d:2 * d]) + mod_ref[0, :, 0:d]
        return h.astype(BF16)

    def project(rows, hb):
        def rotary_t(c):
            f = ROPE_FREQS
            out = []
            for a in range(2):
                x1, x2 = c[2 * a * f:(2 * a + 1) * f], c[(2 * a + 1) * f:(2 * a + 2) * f]
                cos, sin = rope_t_ref[0, 0, a * f:(a + 1) * f, rows], rope_t_ref[0, 1, a * f:(a + 1) * f, rows]
                out += [x1 * cos - x2 * sin, x2 * cos + x1 * sin]
            return jnp.concatenate(out, axis=0)

        def store_slots(q_ref, qt, use_norm, use_rope):
            zeros = jnp.zeros((HEAD_DIM, sub), F32)
            for j in range(N_SLOTS):
                c = qt[j * HEAD_DIM:(j + 1) * HEAD_DIM]
                if use_norm:
                    ms = jnp.mean(c * c, axis=0, keepdims=True)
                    c = c * lax.rsqrt(ms + EPS) * gq_ref[:, 0:sub]
                if use_rope:
                    c = rotary_t(c)
                c = c * Q_SCALE
                slot = jnp.concatenate([c, zeros] if j % 2 == 0 else [zeros, c], axis=0)
                q_ref[0, j, :, rows] = slot.astype(BF16)

        def store_vt(v_ref, vt):
            if even:
                v_ref[0, :, rows] = vt.astype(BF16)
            else:
                for c in range(sub // LANES):
                    v_ref[0, rows.start // LANES + c] = vt[:, c * LANES:(c + 1) * LANES].astype(BF16)

        def rotary(c):
            up = pltpu.roll(c, PAIR - ROPE_FREQS, axis=1)
            down = pltpu.roll(c, ROPE_FREQS, axis=1)
            return c * rope_ref[0, 0, rows] + up * rope_ref[0, 1, rows] + down * rope_ref[0, 2, rows]

        o_v1, o_q2, o_v2 = Q_WIDTH, Q_WIDTH + PAIR, 2 * Q_WIDTH + PAIR
        project_keys(rows, hb, rotary)
        store_slots(q1_ref, _dot_nt(wt_ref[0:o_v1], hb), even, True)
        store_slots(q2_ref, _dot_nt(wt_ref[o_q2:o_v2], hb), False, even)
        store_vt(v1_ref, _dot_nt(wt_ref[o_v1:o_q2], hb))
        store_vt(v2_ref, _dot_nt(wt_ref[o_v2:o_v2 + Q_WIDTH], hb))

    def project_keys(rows, hb, rotary):
        k = _dot(hb, wk_ref[...])
        k1 = k[:, 0:PAIR]
        if even:
            sq = k1 * k1
            hi = sq.astype(BF16)
            lo = (sq - hi.astype(F32)).astype(BF16)
            ms = (_dot(hi, seg_ref[...]) + _dot(lo, seg_ref[...])) * (1.0 / HEAD_DIM)
            k1 = k1 * lax.rsqrt(ms + EPS) * gk_ref[...]
        k1 = rotary(k1)
        k1_ref[0, rows] = k1.astype(BF16)
        for j in range(Q_WIDTH // PAIR):
            c = k[:, PAIR + j * PAIR:PAIR + (j + 1) * PAIR]
            if even:
                c = rotary(c)
            k2_ref[0, rows, j * PAIR:(j + 1) * PAIR] = c.astype(BF16)

    blocks = [slice(r, r + sub) for r in range(0, tm, sub)]
    hbs = [normed(rows) for rows in blocks]
    for rows, hb in zip(blocks, hbs):
        project(rows, hb)


def _in_proj(tok, ctx, mod, gain, wt, wk, norm_args, rope_args, *, even, tm):
    n, s, d = tok.shape
    assert ctx.shape == (1, s, d) and mod.shape[0] == n + 1
    b = n + 1

    def const(a):
        return pl.BlockSpec(a.shape, lambda i, j: (0,) * a.ndim)

    def is_ctx(i):
        return (i == n).astype(jnp.int32)

    in_specs = [pl.BlockSpec((1, tm, d), lambda i, j: (jnp.minimum(i, n - 1), j, 0)),
                pl.BlockSpec((1, tm, d), lambda i, j: (0, is_ctx(i) * j, 0)),
                pl.BlockSpec((1, 1, mod.shape[-1]), lambda i, j: (i, 0, 0)),
                const(gain), const(wt), const(wk)]
    args = [tok, ctx, mod, gain, wt, wk]
    if even:
        gq, gk, seg = norm_args
        in_specs += [const(gq), const(gk), const(seg)]
        args += [gq, gk, seg]
    rope_t, rope_n = rope_args
    in_specs += [pl.BlockSpec((1, 2, 2 * ROPE_FREQS, tm), lambda i, j: (is_ctx(i), 0, 0, j)),
                 pl.BlockSpec((1, 3, tm, PAIR), lambda i, j: (is_ctx(i), 0, j, 0))]
    args += [rope_t, rope_n]

    def vt_out(rows):
        if even:
            return (pl.BlockSpec((1, rows, tm), lambda i, j: (i, 0, j)),
                    jax.ShapeDtypeStruct((b, rows, s), BF16))
        return (pl.BlockSpec((1, tm // LANES, rows, LANES), lambda i, j: (i, j, 0, 0)),
                jax.ShapeDtypeStruct((b, s // LANES, rows, LANES), BF16))

    slot_out = (pl.BlockSpec((1, N_SLOTS, PAIR, tm), lambda i, j: (i, 0, 0, j)),
                jax.ShapeDtypeStruct((b, N_SLOTS, PAIR, s), BF16))

    def nat_out(cols):
        return (pl.BlockSpec((1, tm, cols), lambda i, j: (i, j, 0)), jax.ShapeDtypeStruct((b, s, cols), BF16))

    outs = [slot_out, nat_out(PAIR), vt_out(PAIR), slot_out, nat_out(Q_WIDTH), vt_out(Q_WIDTH)]
    return pl.pallas_call(
        functools.partial(_in_kernel, even=even, n_batch=n, sub_rows=IN_SUB_ROWS),
        grid=(b, s // tm),
        in_specs=in_specs,
        out_specs=[o[0] for o in outs],
        out_shape=[o[1] for o in outs],
        compiler_params=_compiler_params(2),
        name="in_proj_" + ("even" if even else "odd"),
    )(*args)


class AttnJob(NamedTuple):
    query: Callable[[], Any]
    scores: Sequence[Callable[[Any], Any]]
    values: Sequence[Callable[[Any], Any]]
    finish: Callable[[Any], None]
    sink: Any = None


def _attend_lead(jobs):
    return max(len(job.scores) for job in jobs) + ATTEND_EXTRA_LEAD


def _attend_jobs(jobs, s_ref):
    items = [(j, c) for j, job in enumerate(jobs) for c in range(len(job.scores))]
    lead = _attend_lead(jobs)
    n_slots = s_ref.shape[0]
    assert n_slots > lead
    qt, shape, m, l, o = {}, {}, {}, {}, {}
    for t in range(len(items) + lead):
        if t < len(items):
            j, c = items[t]
            job = jobs[j]
            if c == 0:
                qt[j] = job.query()
            x = job.scores[c](qt[j])
            shape[t] = x.shape
            s_ref[t % n_slots, 0:x.shape[0], 0:x.shape[1]] = x
            part_m = jnp.max(x, axis=0, keepdims=True)
            if c == 0:
                m[j] = part_m if job.sink is None else jnp.maximum(part_m, job.sink)
            else:
                m[j] = jnp.maximum(m[j], part_m)
        if t >= lead:
            j, c = items[t - lead]
            job = jobs[j]
            rows, cols = shape.pop(t - lead)
            e = jnp.exp2(s_ref[(t - lead) % n_slots, 0:rows, 0:cols] - m[j])
            part_l = jnp.sum(e, axis=0, keepdims=True)
            part_o = job.values[c](e.astype(BF16))
            l[j] = part_l if c == 0 else l[j] + part_l
            o[j] = part_o if c == 0 else o[j] + part_o
            if c == len(job.scores) - 1:
                denom = l.pop(j) if job.sink is None else l.pop(j) + jnp.exp2(job.sink - m[j])
                job.finish(o.pop(j) * (1.0 / denom))


def _attn_even_kernel(*refs, with_x, lam0, tq, key_chunk):
    if with_x:
        q1_ref, q2_ref, k1c_ref, v1c_ref, k2c_ref, v2c_ref, k1x_ref, v1x_ref, k2x_ref, v2x_ref = refs[:10]
    else:
        q1_ref, q2_ref, k1c_ref, v1c_ref, k2c_ref, v2c_ref = refs[:6]
    lam_ref, sub_ref, o_ref, s_ref = refs[-4:]

    def chunk_fns(c_ref, x_ref, cols, transposed):
        def make(ref, lo, hi):
            if transposed:
                return lambda e: _dot(ref[0, cols, lo:hi], e)
            return lambda qt: _dot(ref[0, lo:hi, cols], qt)
        n_ctx = c_ref.shape[2] if transposed else c_ref.shape[1]
        fns = [make(c_ref, 0, n_ctx)]
        if with_x:
            n_x = x_ref.shape[2] if transposed else x_ref.shape[1]
            fns += [make(x_ref, lo, lo + key_chunk) for lo in range(0, n_x, key_chunk)]
        return fns

    group = max(2, EVEN_A_QUERY_COLS // tq)
    jobs = []
    pair_cols = slice(0, PAIR)

    def a_query(g):
        return jnp.concatenate([q1_ref[0, g * group + j] for j in range(group)], axis=1)

    def a_finish(g, o):
        for pp in range(group // 2):
            pair_t = jnp.concatenate([o[0:HEAD_DIM, (2 * pp) * tq:(2 * pp + 1) * tq],
                                      o[HEAD_DIM:PAIR, (2 * pp + 1) * tq:(2 * pp + 2) * tq]], axis=0)
            col = (g * (group // 2) + pp) * PAIR
            o_ref[0, :, col:col + PAIR] = pair_t.T.astype(BF16)

    for g in range(N_SLOTS // group):
        jobs.append(AttnJob(functools.partial(a_query, g),
                            chunk_fns(k1c_ref, k1x_ref if with_x else None, pair_cols, False),
                            chunk_fns(v1c_ref, v1x_ref if with_x else None, pair_cols, True),
                            functools.partial(a_finish, g)))

    def b_query(h):
        return jnp.concatenate([q2_ref[0, 2 * h], q2_ref[0, 2 * h + 1]], axis=1)

    def b_finish(h, o):
        lp = lam_ref[...]
        lam = (jnp.exp(jnp.sum(lp[0:1] * lp[1:2], axis=-1, keepdims=True))
               - jnp.exp(jnp.sum(lp[2:3] * lp[3:4], axis=-1, keepdims=True)) + lam0)
        ob = (o[:, 0:tq] - lam * o[:, tq:2 * tq]).T
        ob = _rms(ob, sub_ref[...]) * (1.0 - lam0)
        col = Q_WIDTH + h * PAIR
        o_ref[0, :, col:col + PAIR] = ob.astype(BF16)

    for h in range(N_SLOTS // 2):
        cols = slice(h * PAIR, (h + 1) * PAIR)
        jobs.append(AttnJob(functools.partial(b_query, h),
                            chunk_fns(k2c_ref, k2x_ref if with_x else None, cols, False),
                            chunk_fns(v2c_ref, v2x_ref if with_x else None, cols, True),
                            functools.partial(b_finish, h)))
    _attend_jobs(jobs, s_ref)


def _ctx_specs(kv, n_ctx, blocked):
    row = kv[0].shape[0] - 1

    def natural(a):
        return pl.BlockSpec((1, n_ctx, a.shape[2]), lambda i, j: (row, i, 0))

    def transposed(a):
        if blocked:
            return pl.BlockSpec((1, n_ctx // LANES) + a.shape[2:], lambda i, j: (row, i, 0, 0))
        return pl.BlockSpec((1, a.shape[1], n_ctx), lambda i, j: (row, 0, i))

    k1, v1, k2, v2 = kv
    return [natural(k1), transposed(v1), natural(k2), transposed(v2)]


def _attn_even(qx, kv, lam_p, subln, *, with_x, n_ctx, lam0, tq):
    q1, q2 = qx
    row = q1.shape[0] - 1
    if with_x:
        b, s = row, q1.shape[3]
        slot_spec = pl.BlockSpec((1, N_SLOTS, PAIR, tq), lambda i, j: (i, 0, 0, j))
    else:
        b, s = q1.shape[3] // n_ctx, n_ctx
        slot_spec = pl.BlockSpec((1, N_SLOTS, PAIR, tq), lambda i, j: (row, 0, 0, i * (n_ctx // tq) + j))

    def full(a):
        return pl.BlockSpec((1,) + a.shape[1:], lambda i, j: (i, 0, 0))

    n_x = kv[0].shape[1]
    key_chunk = min(EVEN_KEY_CHUNK, n_x) if with_x else n_ctx
    chunks_per_job = 1 + (n_x // key_chunk if with_x else 0)
    score_ring = pltpu.VMEM((chunks_per_job + ATTEND_EXTRA_LEAD + 1, max(n_ctx, key_chunk),
                             max(2 * tq, EVEN_A_QUERY_COLS)), F32)
    return pl.pallas_call(
        functools.partial(_attn_even_kernel, with_x=with_x, lam0=lam0, tq=tq, key_chunk=key_chunk),
        grid=(b, s // tq),
        scratch_shapes=[score_ring],
        in_specs=[slot_spec, slot_spec] + _ctx_specs(kv, n_ctx, blocked=False)
                 + ([full(a) for a in kv] if with_x else [])
                 + [pl.BlockSpec(lam_p.shape, lambda i, j: (0, 0)), pl.BlockSpec(subln.shape, lambda i, j: (0, 0))],
        out_specs=pl.BlockSpec((1, tq, 2 * Q_WIDTH), lambda i, j: (i, j, 0)),
        out_shape=jax.ShapeDtypeStruct((b, s, 2 * Q_WIDTH), BF16),
        compiler_params=_compiler_params(2),
        name="attn_even" + ("_x" if with_x else "_ctx"),
    )(q1, q2, *kv, *(kv if with_x else ()), lam_p, subln)


N_BIAS_ROWS = 2 * NA_ROWS


def _bias_kernel(rpb_ref, o_ref):
    head = pl.program_id(0)
    n_dr = 2 * NA_ROWS - 1
    n_dc = 2 * NA_COLS - 1
    shape = (GRID_W, LANES)
    ck = lax.broadcasted_iota(jnp.int32, shape, 0)
    lane = lax.broadcasted_iota(jnp.int32, shape, 1)
    second = lane >= GRID_W
    dcl = jnp.clip(ck - (lane & (GRID_W - 1)), -(NA_COLS - 1), NA_COLS - 1) + (NA_COLS - 1)
    tiles = []
    for d in range(n_dr):
        acc = jnp.zeros(shape, F32)
        for j in range(n_dc):
            acc = jnp.where(dcl == j, rpb_ref[(head * n_dr + d) * n_dc + j], acc)
        tiles.append(acc * LOG2E)
    for t in range(N_BIAS_ROWS):
        o_ref[0, t] = jnp.where(second, tiles[max(t - 1, 0)], tiles[min(t, n_dr - 1)])


def _bias_table(rpb):
    heads = rpb.shape[0]
    return pl.pallas_call(
        _bias_kernel,
        grid=(heads,),
        in_specs=[pl.BlockSpec(memory_space=pltpu.SMEM)],
        out_specs=pl.BlockSpec((1, N_BIAS_ROWS, GRID_W, LANES), lambda h: (h, 0, 0, 0)),
        out_shape=jax.ShapeDtypeStruct((heads, N_BIAS_ROWS, GRID_W, LANES), F32),
        compiler_params=_compiler_params(1),
        name="rel_bias_table",
    )(rpb.reshape(-1))


def _attn_odd_kernel(q1_ref, q2_ref, k1x_ref, v1x_ref, k2x_ref, v2x_ref, k1c_ref, v1c_ref, k2c_ref, v2c_ref,
                     sink_ref, tp_ref, win_ref, o_ref, s_ref, *, tq, sink_order):
    i = pl.program_id(1)
    t = k1x_ref.shape[1]
    n_ctx_blk = v1c_ref.shape[1]
    rows = t // GRID_W
    kh = min(NA_ROWS, rows)
    n_blk = tq // LANES

    jobs = []

    n_q = N_SLOTS * C_WINDOW
    n_seq_blk = t // C_WINDOW
    lane_q = lax.broadcasted_iota(jnp.int32, (1, n_q), 1)
    sink = jnp.zeros((1, n_q), F32)
    for j in range(N_SLOTS):
        sink = jnp.where(lane_q >> _log2(C_WINDOW) == j, sink_ref[sink_order[j]], sink)
    sink = sink * LOG2E
    n_qj = C_SLOTS_PER_JOB * C_WINDOW
    tri = (lax.broadcasted_iota(jnp.int32, (C_WINDOW, n_qj), 0)
           - (lax.broadcasted_iota(jnp.int32, (C_WINDOW, n_qj), 1) & (C_WINDOW - 1)))

    def c_blocks(jb):
        blk = i * n_blk + jb
        return blk, jnp.maximum(blk - 1, 0), jnp.minimum(blk + 1, n_seq_blk - 1)

    def c_query(jb, s0):
        return jnp.concatenate([q1_ref[0, j, :, jb * LANES:(jb + 1) * LANES]
                                for j in range(s0, s0 + C_SLOTS_PER_JOB)], axis=1)

    def c_local_scores(jb, qt):
        blk, prev, nxt = c_blocks(jb)
        k_loc = jnp.concatenate([k1x_ref[0, pl.ds(pl.multiple_of(n * C_WINDOW, C_WINDOW), C_WINDOW), :]
                                 for n in (prev, blk, nxt)], axis=0)
        s = _dot(k_loc, qt)
        lo_bound = jnp.where(blk > 0, 0, C_WINDOW)
        hi_bound = jnp.where(blk < n_seq_blk - 1, 0, -C_WINDOW)
        return jnp.concatenate([jnp.where(tri >= lo_bound, s[0:C_WINDOW], NEG),
                                s[C_WINDOW:2 * C_WINDOW],
                                jnp.where(tri <= hi_bound, s[2 * C_WINDOW:3 * C_WINDOW], NEG)], axis=0)

    def c_local_values(jb, e):
        blk, prev, nxt = c_blocks(jb)
        return _dot(jnp.concatenate([v1x_ref[0, n] for n in (prev, blk, nxt)], axis=1), e)

    def c_finish(jb, s0, o):
        for p in range(C_SLOTS_PER_JOB // 2):
            pair_t = jnp.concatenate([o[0:HEAD_DIM, (2 * p) * LANES:(2 * p + 1) * LANES],
                                      o[HEAD_DIM:PAIR, (2 * p + 1) * LANES:(2 * p + 2) * LANES]], axis=0)
            col = (s0 // 2 + p) * PAIR
            o_ref[0, jb * LANES:(jb + 1) * LANES, col:col + PAIR] = pair_t.T.astype(BF16)

    for jb in range(n_blk):
        for s0 in range(0, N_SLOTS, C_SLOTS_PER_JOB):
            jobs.append(AttnJob(
                functools.partial(c_query, jb, s0),
                [functools.partial(c_local_scores, jb), lambda qt: _dot(k1c_ref[0], qt)],
                [functools.partial(c_local_values, jb),
                 lambda e: _dot(jnp.concatenate([v1c_ref[0, c] for c in range(n_ctx_blk)], axis=1), e)],
                functools.partial(c_finish, jb, s0), sink[:, s0 * C_WINDOW:(s0 + C_SLOTS_PER_JOB) * C_WINDOW]))

    blk_rows = LANES // GRID_W
    span_rows = -(-(blk_rows - 1 + kh) // 2) * 2
    n_loc = span_rows * GRID_W
    q_row = (lax.broadcasted_iota(jnp.int32, (1, 4 * LANES), 1) >> _log2(GRID_W)) & (blk_rows - 1)

    def d_rows(bk):
        r0 = (i * n_blk + bk) * blk_rows
        return r0, jnp.minimum(jnp.clip(r0 - kh // 2, 0, rows - kh), rows - span_rows)

    def d_query(bk, p0):
        tok = slice(bk * LANES, (bk + 1) * LANES)
        blocks = [jnp.concatenate([q2_ref[0, 2 * p, :, tok], q2_ref[0, 2 * p + 1, :, tok]], axis=1)
                  for p in (p0, p0 + 1)]
        zeros = jnp.zeros_like(blocks[0])
        return jnp.concatenate([jnp.concatenate([blocks[0], zeros], axis=1),
                                jnp.concatenate([zeros, blocks[1]], axis=1)], axis=0)

    def d_local_scores(bk, p0, qt):
        r0, u = d_rows(bk)
        k_loc = k2x_ref[0, pl.ds(pl.multiple_of(u * GRID_W, LANES), n_loc), p0 * PAIR:(p0 + 2) * PAIR]
        bias = jnp.concatenate(
            [jnp.concatenate([tp_ref[2 * p0 + lb, jnp.clip(u + a - r0 + NA_ROWS - 1, 0, N_BIAS_ROWS - 1)]
                              for lb in range(4)], axis=1)
             for a in range(span_rows)], axis=0)
        first = jnp.zeros((1, 4 * LANES), jnp.int32)
        for j in range(blk_rows):
            first = jnp.where(q_row == j, jnp.clip(r0 + j - kh // 2, 0, rows - kh) - u, first)
        valid = lax.bitcast_convert_type(win_ref[...] - first, jnp.uint32) < jnp.uint32(kh)
        return jnp.where(valid, _dot(k_loc, qt) + bias, NEG)

    def d_values(bk, p0, local, e):
        _, u = d_rows(bk)
        out = []
        for n, p in enumerate((p0, p0 + 1)):
            cols = slice(p * PAIR, (p + 1) * PAIR)
            if local:
                vt = jnp.concatenate([v2x_ref[0, u // blk_rows + b, cols, :] for b in range(n_loc // LANES)],
                                     axis=1)
            else:
                vt = jnp.concatenate([v2c_ref[0, b, cols, :] for b in range(n_ctx_blk)], axis=1)
            out.append(_dot(vt, e[:, n * 2 * LANES:(n + 1) * 2 * LANES]))
        return jnp.concatenate(out, axis=1)

    def d_finish(bk, p0, o):
        for n, p in enumerate((p0, p0 + 1)):
            lo = o[0:HEAD_DIM, (2 * n) * LANES:(2 * n + 1) * LANES]
            hi = o[HEAD_DIM:PAIR, (2 * n + 1) * LANES:(2 * n + 2) * LANES]
            o_ref[0, bk * LANES:(bk + 1) * LANES, Q_WIDTH + p * PAIR:Q_WIDTH + (p + 1) * PAIR] = (
                jnp.concatenate([lo, hi], axis=0).T.astype(BF16))

    for bk in range(n_blk):
        for p0 in range(0, N_SLOTS // 2, 2):
            jobs.append(AttnJob(
                functools.partial(d_query, bk, p0),
                [functools.partial(d_local_scores, bk, p0),
                 lambda qt, p0=p0: _dot(k2c_ref[0, :, p0 * PAIR:(p0 + 2) * PAIR], qt)],
                [functools.partial(d_values, bk, p0, True), functools.partial(d_values, bk, p0, False)],
                functools.partial(d_finish, bk, p0)))
    _attend_jobs(jobs, s_ref)


def _attn_odd(qx, kv, sink, tp, *, n_ctx, tq, sink_order):
    q1, q2 = qx
    b, s = q1.shape[0] - 1, q1.shape[3]
    slot_spec = pl.BlockSpec((1, N_SLOTS, PAIR, tq), lambda i, j: (i, 0, 0, j))

    def full(a):
        return pl.BlockSpec((1,) + a.shape[1:], lambda i, j: (i,) + (0,) * (a.ndim - 1))

    rows = s // GRID_W
    span_rows = -(-(LANES // GRID_W - 1 + min(NA_ROWS, rows)) // 2) * 2
    chunks_per_job = 2
    key_i = np.arange(span_rows * GRID_W)[:, None]
    cq = np.arange(4 * LANES)[None, :] % GRID_W
    cs = np.clip(cq - NA_COLS // 2, 0, GRID_W - NA_COLS)
    ck = key_i % GRID_W
    col_ok = (ck >= cs) & (ck < cs + NA_COLS)
    window = jnp.asarray(key_i // GRID_W + np.where(col_ok, 0, NA_INVALID_COLUMN), jnp.int32)
    score_ring = pltpu.VMEM((chunks_per_job + ATTEND_EXTRA_LEAD + 1, max(3 * C_WINDOW, span_rows * GRID_W, n_ctx),
                             max(C_SLOTS_PER_JOB * C_WINDOW, 4 * LANES)), F32)
    return pl.pallas_call(
        functools.partial(_attn_odd_kernel, tq=tq, sink_order=sink_order),
        grid=(b, s // tq),
        scratch_shapes=[score_ring],
        in_specs=[slot_spec, slot_spec] + [full(a) for a in kv] + _ctx_specs(kv, n_ctx, blocked=True)
                 + [pl.BlockSpec(memory_space=pltpu.SMEM), pl.BlockSpec(tp.shape, lambda i, j: (0, 0, 0, 0)),
                    pl.BlockSpec(window.shape, lambda i, j: (0, 0))],
        out_specs=pl.BlockSpec((1, tq, 2 * Q_WIDTH), lambda i, j: (i, j, 0)),
        out_shape=jax.ShapeDtypeStruct((b, s, 2 * Q_WIDTH), BF16),
        compiler_params=_compiler_params(2),
        name="attn_odd",
    )(q1, q2, *kv, *kv, sink, tp, window)


def _post_kernel(m_ref, x_ref, mod_ref, g_ref, wo_ref, w1_ref, w2_ref, o_ref, *, ff_chunk, sub_rows):
    d = x_ref.shape[-1]

    def mod(k):
        return mod_ref[0, :, k * d:(k + 1) * d]

    def mixer_residual(rows):
        x1 = x_ref[0, rows] + mod(2) * _rms(_dot(m_ref[0, rows], wo_ref[...]), g_ref[1:2])
        return x1, (_rms(x1, g_ref[2:3]) * (1.0 + mod(4)) + mod(3)).astype(BF16)

    def mlp(h):
        f = None
        for c in range(w1_ref.shape[1] // ff_chunk):
            cols = slice(c * ff_chunk, (c + 1) * ff_chunk)
            a = jnp.maximum(_dot(h, w1_ref[:, cols]), 0.0)
            part = _dot((a * a).astype(BF16), w2_ref[cols, :])
            f = part if f is None else f + part
        return f

    tm = x_ref.shape[1]
    sub = min(sub_rows, tm)
    blocks = [slice(r, r + sub) for r in range(0, tm, sub)]
    stage1 = [mixer_residual(rows) for rows in blocks]
    stage2 = [mlp(h) for _, h in stage1]
    for rows, (x1, _), f in zip(blocks, stage1, stage2):
        o_ref[0, rows] = x1 + mod(5) * _rms(f, g_ref[3:4])


def _post(m, x, mod, gains, wo, w1, w2, *, tm, ff_chunk):
    b, s, d = x.shape
    bm = mod.shape[0]
    mod_map = (lambda i, j: (i, 0, 0)) if bm > 1 else (lambda i, j: (0, 0, 0))

    def const(a):
        return pl.BlockSpec(a.shape, lambda i, j: (0, 0), pipeline_mode=pl.Buffered(1))

    return pl.pallas_call(
        functools.partial(_post_kernel, ff_chunk=ff_chunk, sub_rows=POST_SUB_ROWS),
        grid=(b, s // tm),
        in_specs=[pl.BlockSpec((1, tm, m.shape[-1]), lambda i, j: (i, j, 0)),
                  pl.BlockSpec((1, tm, d), lambda i, j: (i, j, 0)),
                  pl.BlockSpec((1, 1, mod.shape[-1]), mod_map),
                  pl.BlockSpec(gains.shape, lambda i, j: (0, 0)),
                  const(wo), const(w1), const(w2)],
        out_specs=pl.BlockSpec((1, tm, d), lambda i, j: (i, j, 0)),
        out_shape=jax.ShapeDtypeStruct((b, s, d), F32),
        compiler_params=_compiler_params(2),
        name="post",
    )(m, x, mod, gains, wo, w1, w2)


def _rope_tables(t):
    tok = jnp.arange(t, dtype=jnp.int32)
    pos = jnp.stack([tok // GRID_W, tok % GRID_W], axis=-1).astype(F32)
    inv = ROPE_THETA ** (-jnp.arange(ROPE_FREQS, dtype=F32) / ROPE_FREQS)
    ang = pos[..., None] * inv
    cos, sin = jnp.cos(ang), jnp.sin(ang)
    rope_t = jnp.stack([cos.reshape(t, -1).T, sin.reshape(t, -1).T])
    zero = jnp.zeros_like(sin[:, 0])
    c = jnp.concatenate([cos[:, 0], cos[:, 0], cos[:, 1], cos[:, 1]], axis=-1)
    up = jnp.concatenate([-sin[:, 0], zero, -sin[:, 1], zero], axis=-1)
    down = jnp.concatenate([zero, sin[:, 0], zero, sin[:, 1]], axis=-1)
    rope_n = jnp.stack([jnp.tile(a, (1, PAIR // HEAD_DIM)) for a in (c, up, down)])
    one_t, one_n = jnp.ones_like(rope_t[0]), jnp.ones_like(rope_n[0])
    identity_t = jnp.stack([one_t, jnp.zeros_like(one_t)])
    identity_n = jnp.stack([one_n, jnp.zeros_like(one_n), jnp.zeros_like(one_n)])
    return jnp.stack([rope_t, identity_t]), jnp.stack([rope_n, identity_n])


def _pair_gqa_heads(w, axis):
    shape = w.shape
    w = w.reshape(shape[:axis] + (2, N_SLOTS // 2, HEAD_DIM) + shape[axis + 1:])
    return jnp.swapaxes(w, axis, axis + 1).reshape(shape)


def _lambda_init(layer):
    return 0.8 - 0.6 * math.exp(-0.3 * layer)


def kernel(x, c, ctx, c_ctx, w_mod, b_mod, norm_g, w_in, w_out, w_mlp_in, w_mlp_out, qk_norm_a, diff_lambda,
           diff_subln, sink_c, rpb_d):
    b, t, d = x.shape
    n_ctx = ctx.shape[1]
    depth = w_mod.shape[0]
    mod_rows = 16
    cc = jnp.concatenate([c, c_ctx[None], jnp.zeros((mod_rows - b - 1, d), F32)], axis=0)
    mod_all = _modulation(cc, w_mod, b_mod)

    rope_args = _rope_tables(t)
    seg = jnp.kron(jnp.eye(2, dtype=F32), jnp.ones((HEAD_DIM, HEAD_DIM), F32)).astype(BF16)
    tm_x = 2 * IN_SUB_ROWS
    assert b * n_ctx == t
    ctx = ctx.reshape(1, b * n_ctx, d)

    for l in range(depth):
        last = l == depth - 1
        even = l % 2 == 0
        i = l // 2
        mod_x = mod_all[l, :b][:, None, :]
        mod_c = mod_all[l, b][None, None, :]
        w = w_in[l]
        o_k1, o_v1, o_q2, o_k2, o_v2 = Q_WIDTH, Q_WIDTH + PAIR, Q_WIDTH + 2 * PAIR, 2 * Q_WIDTH + 2 * PAIR, \
            3 * Q_WIDTH + 2 * PAIR
        wt = jnp.concatenate([_pair_gqa_heads(w[:, 0:o_k1], 1), w[:, o_v1:o_q2], w[:, o_q2:o_k2], w[:, o_v2:]],
                             axis=1).T.astype(BF16)
        wk = jnp.concatenate([w[:, o_k1:o_v1], w[:, o_k2:o_v2]], axis=1).astype(BF16)
        wo = jnp.concatenate([_pair_gqa_heads(w_out[l, 0:Q_WIDTH], 0), w_out[l, Q_WIDTH:]], axis=0).astype(BF16)
        w1 = w_mlp_in[l].astype(BF16)
        w2 = w_mlp_out[l].astype(BF16)
        g0 = norm_g[l, 0][None, :]

        def norm_args(tm):
            if not even:
                return None
            return (jnp.broadcast_to(qk_norm_a[i, 0][:, None], (HEAD_DIM, tm)),
                    jnp.tile(qk_norm_a[i, 1], PAIR // HEAD_DIM)[None, :], seg)

        mod_xc = mod_all[l, :b + 1][:, None, :]
        q1, k1, v1, q2, k2, v2 = _in_proj(x, ctx, mod_xc, g0, wt, wk, norm_args(tm_x), rope_args,
                                          even=even, tm=tm_x)
        qx, kv = (q1, q2), (k1, v1, k2, v2)
        if even:
            lam0 = _lambda_init(l)
            sub = diff_subln[i][None, :]
            m_x = _attn_even(qx, kv, diff_lambda[i], sub, with_x=True, n_ctx=n_ctx, lam0=lam0, tq=256)
            if not last:
                m_c = _attn_even(qx, kv, diff_lambda[i], sub, with_x=False, n_ctx=n_ctx, lam0=lam0, tq=256)
        else:
            if not last:
                raise NotImplementedError("context update after an odd layer is not needed for depth 2")
            tp = _bias_table(rpb_d[i])
            m_x = _attn_odd(qx, kv, sink_c[i], tp, n_ctx=n_ctx, tq=256, sink_order=GQA_SLOT_HEADS)
        x = _post(m_x, x, mod_x, norm_g[l], wo, w1, w2, tm=512, ff_chunk=1024)
        if not last:
            ctx = _post(m_c.reshape(1, b * n_ctx, -1), ctx, mod_c, norm_g[l], wo, w1, w2, tm=512, ff_chunk=1024)
    return x
```

```python
import functools
import math
from typing import Any, Callable, NamedTuple, Sequence

import numpy as np
import jax
import jax.numpy as jnp
from jax import lax
from jax.experimental import pallas as pl
from jax.experimental.pallas import tpu as pltpu

F32 = jnp.float32
BF16 = jnp.bfloat16

GRID_W = 64
HEAD_DIM = 64
ROPE_THETA = 10000.0
ROPE_FREQS = HEAD_DIM // 4
N_SLOTS = 8
PAIR = 2 * HEAD_DIM
Q_WIDTH = N_SLOTS * HEAD_DIM
LANES = 128
MXU_QUERY_COLS = 512
EVEN_A_QUERY_COLS = MXU_QUERY_COLS
EVEN_KEY_CHUNK = 1024
EVEN_EXTRA_LEAD = 2
ODD_EXTRA_LEAD = 3
IN_SUB_ROWS = MXU_QUERY_COLS
POST_SUB_ROWS = 256
C_SLOTS_PER_JOB = 8
NA_CHUNK_ROWS = 4
NA_INVALID_COLUMN = 1 << 20
C_WINDOW = 128
NA_ROWS = 8
NA_COLS = 16
EPS = 1e-6
NEG = -1e30
LOG2E = math.log2(math.e)
Q_SCALE = HEAD_DIM ** -0.5 * LOG2E
GQA_SLOT_HEADS = tuple(half * (N_SLOTS // 2) + p for p in range(N_SLOTS // 2) for half in range(2))

VMEM_LIMIT_BYTES = 56 * 1024 * 1024


def _compiler_params(n_grid_dims):
    return pltpu.CompilerParams(dimension_semantics=("arbitrary",) * n_grid_dims,
                                vmem_limit_bytes=VMEM_LIMIT_BYTES)


def _dot(a, b):
    return jnp.dot(a, b, preferred_element_type=F32)


def _dot_nt(a, b):
    return lax.dot_general(a, b, (((1,), (1,)), ((), ())), preferred_element_type=F32)


def _log2(n):
    assert n & (n - 1) == 0
    return n.bit_length() - 1


def _rms(x, gain):
    ms = jnp.mean(x * x, axis=-1, keepdims=True)
    return x * lax.rsqrt(ms + EPS) * gain


def _mod_kernel(c_ref, w_ref, b_ref, o_ref):
    c = c_ref[...]
    s = c * jax.nn.sigmoid(c)
    o_ref[0] = _dot(s.astype(BF16), w_ref[0].astype(BF16)) + b_ref[0]


def _modulation(cc, w_mod, b_mod):
    depth, d, n = w_mod.shape
    rows = cc.shape[0]
    tn = n // 4
    return pl.pallas_call(
        _mod_kernel,
        grid=(depth, n // tn),
        in_specs=[pl.BlockSpec((rows, d), lambda l, j: (0, 0)),
                  pl.BlockSpec((1, d, tn), lambda l, j: (l, 0, j)),
                  pl.BlockSpec((1, 1, tn), lambda l, j: (l, 0, j))],
        out_specs=pl.BlockSpec((1, rows, tn), lambda l, j: (l, 0, j)),
        out_shape=jax.ShapeDtypeStruct((depth, rows, n), F32),
        compiler_params=_compiler_params(2),
        name="modulation",
    )(cc, w_mod, b_mod.reshape(depth, 1, n))


def _in_kernel(*refs, even, n_batch, sub_rows):
    refs = list(refs)
    tok_ref, ctx_ref, mod_ref, g_ref, wt_ref, wk_ref = refs[:6]
    del refs[:6]
    if even:
        gq_ref, gk_ref, seg_ref = refs[:3]
        del refs[:3]
    rope_t_ref, rope_ref = refs[:2]
    q1_ref, k1_ref, v1_ref, q2_ref, k2_ref, v2_ref = refs[2:]
    d = tok_ref.shape[-1]
    tm = tok_ref.shape[1]
    sub = min(sub_rows, tm)
    is_ctx = pl.program_id(0) == n_batch

    def normed(rows):
        h = _rms(jnp.where(is_ctx, ctx_ref[0, rows], tok_ref[0, rows]), g_ref[...])
        h = h * (1.0 + mod_ref[0, :, d:2 * d]) + mod_ref[0, :, 0:d]
        return h.astype(BF16)

    def project(rows, hb):
        def rotary_t(c):
            f = ROPE_FREQS
            out = []
            for a in range(2):
                x1, x2 = c[2 * a * f:(2 * a + 1) * f], c[(2 * a + 1) * f:(2 * a + 2) * f]
                cos, sin = rope_t_ref[0, 0, a * f:(a + 1) * f, rows], rope_t_ref[0, 1, a * f:(a + 1) * f, rows]
                out += [x1 * cos - x2 * sin, x2 * cos + x1 * sin]
            return jnp.concatenate(out, axis=0)

        def store_slots(q_ref, qt, use_norm, use_rope):
            zeros = jnp.zeros((HEAD_DIM, sub), F32)
            for j in range(N_SLOTS):
                c = qt[j * HEAD_DIM:(j + 1) * HEAD_DIM]
                if use_norm:
                    ms = jnp.mean(c * c, axis=0, keepdims=True)
                    c = c * lax.rsqrt(ms + EPS) * gq_ref[:, 0:sub]
                if use_rope:
                    c = rotary_t(c)
                c = c * Q_SCALE
                slot = jnp.concatenate([c, zeros] if j % 2 == 0 else [zeros, c], axis=0)
                q_ref[0, j, :, rows] = slot.astype(BF16)

        def store_vt(v_ref, vt):
            if even:
                v_ref[0, :, rows] = vt.astype(BF16)
            else:
                for c in range(sub // LANES):
                    v_ref[0, rows.start // LANES + c] = vt[:, c * LANES:(c + 1) * LANES].astype(BF16)

        def rotary(c):
            up = pltpu.roll(c, PAIR - ROPE_FREQS, axis=1)
            down = pltpu.roll(c, ROPE_FREQS, axis=1)
            return c * rope_ref[0, 0, rows] + up * rope_ref[0, 1, rows] + down * rope_ref[0, 2, rows]

        o_v1, o_q2, o_v2 = Q_WIDTH, Q_WIDTH + PAIR, 2 * Q_WIDTH + PAIR
        project_keys(rows, hb, rotary)
        store_slots(q1_ref, _dot_nt(wt_ref[0:o_v1], hb), even, True)
        store_slots(q2_ref, _dot_nt(wt_ref[o_q2:o_v2], hb), False, even)
        store_vt(v1_ref, _dot_nt(wt_ref[o_v1:o_q2], hb))
        store_vt(v2_ref, _dot_nt(wt_ref[o_v2:o_v2 + Q_WIDTH], hb))

    def project_keys(rows, hb, rotary):
        k = _dot(hb, wk_ref[...])
        k1 = k[:, 0:PAIR]
        if even:
            sq = k1 * k1
            hi = sq.astype(BF16)
            lo = (sq - hi.astype(F32)).astype(BF16)
            ms = (_dot(hi, seg_ref[...]) + _dot(lo, seg_ref[...])) * (1.0 / HEAD_DIM)
            k1 = k1 * lax.rsqrt(ms + EPS) * gk_ref[...]
        k1 = rotary(k1)
        k1_ref[0, rows] = k1.astype(BF16)
        for j in range(Q_WIDTH // PAIR):
            c = k[:, PAIR + j * PAIR:PAIR + (j + 1) * PAIR]
            if even:
                c = rotary(c)
            k2_ref[0, rows, j * PAIR:(j + 1) * PAIR] = c.astype(BF16)

    blocks = [slice(r, r + sub) for r in range(0, tm, sub)]
    hbs = [normed(rows) for rows in blocks]
    for rows, hb in zip(blocks, hbs):
        project(rows, hb)


def _in_proj(tok, ctx, mod, gain, wt, wk, norm_args, rope_args, *, even, tm):
    n, s, d = tok.shape
    assert ctx.shape == (1, s, d) and mod.shape[0] == n + 1
    b = n + 1

    def const(a):
        return pl.BlockSpec(a.shape, lambda i, j: (0,) * a.ndim)

    def is_ctx(i):
        return (i == n).astype(jnp.int32)

    in_specs = [pl.BlockSpec((1, tm, d), lambda i, j: (jnp.minimum(i, n - 1), j, 0)),
                pl.BlockSpec((1, tm, d), lambda i, j: (0, is_ctx(i) * j, 0)),
                pl.BlockSpec((1, 1, mod.shape[-1]), lambda i, j: (i, 0, 0)),
                const(gain), const(wt), const(wk)]
    args = [tok, ctx, mod, gain, wt, wk]
    if even:
        gq, gk, seg = norm_args
        in_specs += [const(gq), const(gk), const(seg)]
        args += [gq, gk, seg]
    rope_t, rope_n = rope_args
    in_specs += [pl.BlockSpec((1, 2, 2 * ROPE_FREQS, tm), lambda i, j: (is_ctx(i), 0, 0, j)),
                 pl.BlockSpec((1, 3, tm, PAIR), lambda i, j: (is_ctx(i), 0, j, 0))]
    args += [rope_t, rope_n]

    def vt_out(rows):
        if even:
            return (pl.BlockSpec((1, rows, tm), lambda i, j: (i, 0, j)),
                    jax.ShapeDtypeStruct((b, rows, s), BF16))
        return (pl.BlockSpec((1, tm // LANES, rows, LANES), lambda i, j: (i, j, 0, 0)),
                jax.ShapeDtypeStruct((b, s // LANES, rows, LANES), BF16))

    slot_out = (pl.BlockSpec((1, N_SLOTS, PAIR, tm), lambda i, j: (i, 0, 0, j)),
                jax.ShapeDtypeStruct((b, N_SLOTS, PAIR, s), BF16))

    def nat_out(cols):
        return (pl.BlockSpec((1, tm, cols), lambda i, j: (i, j, 0)), jax.ShapeDtypeStruct((b, s, cols), BF16))

    outs = [slot_out, nat_out(PAIR), vt_out(PAIR), slot_out, nat_out(Q_WIDTH), vt_out(Q_WIDTH)]
    return pl.pallas_call(
        functools.partial(_in_kernel, even=even, n_batch=n, sub_rows=IN_SUB_ROWS),
        grid=(b, s // tm),
        in_specs=in_specs,
        out_specs=[o[0] for o in outs],
        out_shape=[o[1] for o in outs],
        compiler_params=_compiler_params(2),
        name="in_proj_" + ("even" if even else "odd"),
    )(*args)


class AttnJob(NamedTuple):
    query: Callable[[], Any]
    scores: Sequence[Callable[[Any], Any]]
    values: Sequence[Callable[[Any], Any]]
    finish: Callable[[Any], None]
    sink: Any = None


def _attend_jobs(jobs, s_ref):
    items = [(j, c) for j, job in enumerate(jobs) for c in range(len(job.scores))]
    n_slots = s_ref.shape[0]
    lead = n_slots - 1
    assert lead > max(len(job.scores) for job in jobs)
    qt, shape, m, l, o = {}, {}, {}, {}, {}
    for t in range(len(items) + lead):
        if t < len(items):
            j, c = items[t]
            job = jobs[j]
            if c == 0:
                qt[j] = job.query()
            x = job.scores[c](qt[j])
            shape[t] = x.shape
            s_ref[t % n_slots, 0:x.shape[0], 0:x.shape[1]] = x
            part_m = jnp.max(x, axis=0, keepdims=True)
            if c == 0:
                m[j] = part_m if job.sink is None else jnp.maximum(part_m, job.sink)
            else:
                m[j] = jnp.maximum(m[j], part_m)
        if t >= lead:
            j, c = items[t - lead]
            job = jobs[j]
            rows, cols = shape.pop(t - lead)
            e = jnp.exp2(s_ref[(t - lead) % n_slots, 0:rows, 0:cols] - m[j])
            part_l = jnp.sum(e, axis=0, keepdims=True)
            part_o = job.values[c](e.astype(BF16))
            l[j] = part_l if c == 0 else l[j] + part_l
            o[j] = part_o if c == 0 else o[j] + part_o
            if c == len(job.scores) - 1:
                denom = l.pop(j) if job.sink is None else l.pop(j) + jnp.exp2(job.sink - m[j])
                job.finish(o.pop(j) * (1.0 / denom))


def _attn_even_kernel(*refs, with_x, lam0, tq, key_chunk):
    if with_x:
        q1_ref, q2_ref, k1c_ref, v1c_ref, k2c_ref, v2c_ref, k1x_ref, v1x_ref, k2x_ref, v2x_ref = refs[:10]
    else:
        q1_ref, q2_ref, k1c_ref, v1c_ref, k2c_ref, v2c_ref = refs[:6]
    lam_ref, sub_ref, o_ref, s_ref = refs[-4:]

    def chunk_fns(c_ref, x_ref, cols, transposed):
        def make(ref, lo, hi):
            if transposed:
                return lambda e: _dot(ref[0, cols, lo:hi], e)
            return lambda qt: _dot(ref[0, lo:hi, cols], qt)
        n_ctx = c_ref.shape[2] if transposed else c_ref.shape[1]
        fns = [make(c_ref, 0, n_ctx)]
        if with_x:
            n_x = x_ref.shape[2] if transposed else x_ref.shape[1]
            fns += [make(x_ref, lo, lo + key_chunk) for lo in range(0, n_x, key_chunk)]
        return fns

    group = max(2, EVEN_A_QUERY_COLS // tq)
    jobs = []
    pair_cols = slice(0, PAIR)

    def a_query(g):
        return jnp.concatenate([q1_ref[0, g * group + j] for j in range(group)], axis=1)

    def a_finish(g, o):
        for pp in range(group // 2):
            pair_t = jnp.concatenate([o[0:HEAD_DIM, (2 * pp) * tq:(2 * pp + 1) * tq],
                                      o[HEAD_DIM:PAIR, (2 * pp + 1) * tq:(2 * pp + 2) * tq]], axis=0)
            col = (g * (group // 2) + pp) * PAIR
            o_ref[0, :, col:col + PAIR] = pair_t.T.astype(BF16)

    for g in range(N_SLOTS // group):
        jobs.append(AttnJob(functools.partial(a_query, g),
                            chunk_fns(k1c_ref, k1x_ref if with_x else None, pair_cols, False),
                            chunk_fns(v1c_ref, v1x_ref if with_x else None, pair_cols, True),
                            functools.partial(a_finish, g)))

    def b_query(h):
        return jnp.concatenate([q2_ref[0, 2 * h], q2_ref[0, 2 * h + 1]], axis=1)

    def b_finish(h, o):
        lp = lam_ref[...]
        lam = (jnp.exp(jnp.sum(lp[0:1] * lp[1:2], axis=-1, keepdims=True))
               - jnp.exp(jnp.sum(lp[2:3] * lp[3:4], axis=-1, keepdims=True)) + lam0)
        ob = (o[:, 0:tq] - lam * o[:, tq:2 * tq]).T
        ob = _rms(ob, sub_ref[...]) * (1.0 - lam0)
        col = Q_WIDTH + h * PAIR
        o_ref[0, :, col:col + PAIR] = ob.astype(BF16)

    for h in range(N_SLOTS // 2):
        cols = slice(h * PAIR, (h + 1) * PAIR)
        jobs.append(AttnJob(functools.partial(b_query, h),
                            chunk_fns(k2c_ref, k2x_ref if with_x else None, cols, False),
                            chunk_fns(v2c_ref, v2x_ref if with_x else None, cols, True),
                            functools.partial(b_finish, h)))
    _attend_jobs(jobs, s_ref)


def _ctx_specs(kv, n_ctx, blocked):
    row = kv[0].shape[0] - 1

    def natural(a):
        return pl.BlockSpec((1, n_ctx, a.shape[2]), lambda i, j: (row, i, 0))

    def transposed(a):
        if blocked:
            return pl.BlockSpec((1, n_ctx // LANES) + a.shape[2:], lambda i, j: (row, i, 0, 0))
        return pl.BlockSpec((1, a.shape[1], n_ctx), lambda i, j: (row, 0, i))

    k1, v1, k2, v2 = kv
    return [natural(k1), transposed(v1), natural(k2), transposed(v2)]


def _attn_even(qx, kv, lam_p, subln, *, with_x, n_ctx, lam0, tq):
    q1, q2 = qx
    row = q1.shape[0] - 1
    if with_x:
        b, s = row, q1.shape[3]
        slot_spec = pl.BlockSpec((1, N_SLOTS, PAIR, tq), lambda i, j: (i, 0, 0, j))
    else:
        b, s = q1.shape[3] // n_ctx, n_ctx
        slot_spec = pl.BlockSpec((1, N_SLOTS, PAIR, tq), lambda i, j: (row, 0, 0, i * (n_ctx // tq) + j))

    def full(a):
        return pl.BlockSpec((1,) + a.shape[1:], lambda i, j: (i, 0, 0))

    n_x = kv[0].shape[1]
    key_chunk = min(EVEN_KEY_CHUNK, n_x) if with_x else n_ctx
    chunks_per_job = 1 + (n_x // key_chunk if with_x else 0)
    score_ring = pltpu.VMEM((chunks_per_job + EVEN_EXTRA_LEAD + 1, max(n_ctx, key_chunk),
                             max(2 * tq, EVEN_A_QUERY_COLS)), F32)
    return pl.pallas_call(
        functools.partial(_attn_even_kernel, with_x=with_x, lam0=lam0, tq=tq, key_chunk=key_chunk),
        grid=(b, s // tq),
        scratch_shapes=[score_ring],
        in_specs=[slot_spec, slot_spec] + _ctx_specs(kv, n_ctx, blocked=False)
                 + ([full(a) for a in kv] if with_x else [])
                 + [pl.BlockSpec(lam_p.shape, lambda i, j: (0, 0)), pl.BlockSpec(subln.shape, lambda i, j: (0, 0))],
        out_specs=pl.BlockSpec((1, tq, 2 * Q_WIDTH), lambda i, j: (i, j, 0)),
        out_shape=jax.ShapeDtypeStruct((b, s, 2 * Q_WIDTH), BF16),
        compiler_params=_compiler_params(2),
        name="attn_even" + ("_x" if with_x else "_ctx"),
    )(q1, q2, *kv, *(kv if with_x else ()), lam_p, subln)


N_BIAS_ROWS = 2 * NA_ROWS


def _bias_kernel(rpb_ref, o_ref):
    head = pl.program_id(0)
    n_dr = 2 * NA_ROWS - 1
    n_dc = 2 * NA_COLS - 1
    shape = (GRID_W, LANES)
    ck = lax.broadcasted_iota(jnp.int32, shape, 0)
    lane = lax.broadcasted_iota(jnp.int32, shape, 1)
    second = lane >= GRID_W
    dcl = jnp.clip(ck - (lane & (GRID_W - 1)), -(NA_COLS - 1), NA_COLS - 1) + (NA_COLS - 1)
    tiles = []
    for d in range(n_dr):
        acc = jnp.zeros(shape, F32)
        for j in range(n_dc):
            acc = jnp.where(dcl == j, rpb_ref[(head * n_dr + d) * n_dc + j], acc)
        tiles.append(acc * LOG2E)
    for t in range(N_BIAS_ROWS):
        o_ref[0, t] = jnp.where(second, tiles[max(t - 1, 0)], tiles[min(t, n_dr - 1)])


def _bias_table(rpb):
    heads = rpb.shape[0]
    return pl.pallas_call(
        _bias_kernel,
        grid=(heads,),
        in_specs=[pl.BlockSpec(memory_space=pltpu.SMEM)],
        out_specs=pl.BlockSpec((1, N_BIAS_ROWS, GRID_W, LANES), lambda h: (h, 0, 0, 0)),
        out_shape=jax.ShapeDtypeStruct((heads, N_BIAS_ROWS, GRID_W, LANES), F32),
        compiler_params=_compiler_params(1),
        name="rel_bias_table",
    )(rpb.reshape(-1))


def _attn_odd_kernel(q1_ref, q2_ref, k1x_ref, v1x_ref, k2x_ref, v2x_ref, k1c_ref, v1c_ref, k2c_ref, v2c_ref,
                     sink_ref, tp_ref, win_ref, o_ref, s_ref, *, tq, sink_order):
    i = pl.program_id(1)
    t = k1x_ref.shape[1]
    n_ctx_blk = v1c_ref.shape[1]
    rows = t // GRID_W
    kh = min(NA_ROWS, rows)
    n_blk = tq // LANES

    jobs = []

    n_q = N_SLOTS * C_WINDOW
    n_seq_blk = t // C_WINDOW
    lane_q = lax.broadcasted_iota(jnp.int32, (1, n_q), 1)
    sink = jnp.zeros((1, n_q), F32)
    for j in range(N_SLOTS):
        sink = jnp.where(lane_q >> _log2(C_WINDOW) == j, sink_ref[sink_order[j]], sink)
    sink = sink * LOG2E
    n_qj = C_SLOTS_PER_JOB * C_WINDOW
    tri = (lax.broadcasted_iota(jnp.int32, (C_WINDOW, n_qj), 0)
           - (lax.broadcasted_iota(jnp.int32, (C_WINDOW, n_qj), 1) & (C_WINDOW - 1)))

    def c_blocks(jb):
        blk = i * n_blk + jb
        return blk, jnp.maximum(blk - 1, 0), jnp.minimum(blk + 1, n_seq_blk - 1)

    def c_query(jb, s0):
        return jnp.concatenate([q1_ref[0, j, :, jb * LANES:(jb + 1) * LANES]
                                for j in range(s0, s0 + C_SLOTS_PER_JOB)], axis=1)

    def c_local_scores(jb, qt):
        blk, prev, nxt = c_blocks(jb)
        k_loc = jnp.concatenate([k1x_ref[0, pl.ds(pl.multiple_of(n * C_WINDOW, C_WINDOW), C_WINDOW), :]
                                 for n in (prev, blk, nxt)], axis=0)
        s = _dot(k_loc, qt)
        lo_bound = jnp.where(blk > 0, 0, C_WINDOW)
        hi_bound = jnp.where(blk < n_seq_blk - 1, 0, -C_WINDOW)
        return jnp.concatenate([jnp.where(tri >= lo_bound, s[0:C_WINDOW], NEG),
                                s[C_WINDOW:2 * C_WINDOW],
                                jnp.where(tri <= hi_bound, s[2 * C_WINDOW:3 * C_WINDOW], NEG)], axis=0)

    def c_local_values(jb, e):
        blk, prev, nxt = c_blocks(jb)
        return _dot(jnp.concatenate([v1x_ref[0, n] for n in (prev, blk, nxt)], axis=1), e)

    def c_finish(jb, s0, o):
        for p in range(C_SLOTS_PER_JOB // 2):
            pair_t = jnp.concatenate([o[0:HEAD_DIM, (2 * p) * LANES:(2 * p + 1) * LANES],
                                      o[HEAD_DIM:PAIR, (2 * p + 1) * LANES:(2 * p + 2) * LANES]], axis=0)
            col = (s0 // 2 + p) * PAIR
            o_ref[0, jb * LANES:(jb + 1) * LANES, col:col + PAIR] = pair_t.T.astype(BF16)

    for jb in range(n_blk):
        for s0 in range(0, N_SLOTS, C_SLOTS_PER_JOB):
            jobs.append(AttnJob(
                functools.partial(c_query, jb, s0),
                [functools.partial(c_local_scores, jb), lambda qt: _dot(k1c_ref[0], qt)],
                [functools.partial(c_local_values, jb),
                 lambda e: _dot(jnp.concatenate([v1c_ref[0, c] for c in range(n_ctx_blk)], axis=1), e)],
                functools.partial(c_finish, jb, s0), sink[:, s0 * C_WINDOW:(s0 + C_SLOTS_PER_JOB) * C_WINDOW]))

    blk_rows = LANES // GRID_W
    span_rows = -(-(blk_rows - 1 + kh) // 2) * 2
    n_loc = span_rows * GRID_W
    q_row = (lax.broadcasted_iota(jnp.int32, (1, 4 * LANES), 1) >> _log2(GRID_W)) & (blk_rows - 1)

    def d_rows(bk):
        r0 = (i * n_blk + bk) * blk_rows
        return r0, jnp.minimum(jnp.clip(r0 - kh // 2, 0, rows - kh), rows - span_rows)

    def d_query(bk, p0):
        tok = slice(bk * LANES, (bk + 1) * LANES)
        blocks = [jnp.concatenate([q2_ref[0, 2 * p, :, tok], q2_ref[0, 2 * p + 1, :, tok]], axis=1)
                  for p in (p0, p0 + 1)]
        zeros = jnp.zeros_like(blocks[0])
        return jnp.concatenate([jnp.concatenate([blocks[0], zeros], axis=1),
                                jnp.concatenate([zeros, blocks[1]], axis=1)], axis=0)

    def d_local_scores(bk, p0, qt):
        r0, u = d_rows(bk)
        k_loc = k2x_ref[0, pl.ds(pl.multiple_of(u * GRID_W, LANES), n_loc), p0 * PAIR:(p0 + 2) * PAIR]
        bias = jnp.concatenate(
            [jnp.concatenate([tp_ref[2 * p0 + lb, jnp.clip(u + a - r0 + NA_ROWS - 1, 0, N_BIAS_ROWS - 1)]
                              for lb in range(4)], axis=1)
             for a in range(span_rows)], axis=0)
        first = jnp.zeros((1, 4 * LANES), jnp.int32)
        for j in range(blk_rows):
            first = jnp.where(q_row == j, jnp.clip(r0 + j - kh // 2, 0, rows - kh) - u, first)
        valid = lax.bitcast_convert_type(win_ref[...] - first, jnp.uint32) < jnp.uint32(kh)
        return jnp.where(valid, _dot(k_loc, qt) + bias, NEG)

    def d_values(bk, p0, local, e):
        _, u = d_rows(bk)
        out = []
        for n, p in enumerate((p0, p0 + 1)):
            cols = slice(p * PAIR, (p + 1) * PAIR)
            if local:
                vt = jnp.concatenate([v2x_ref[0, u // blk_rows + b, cols, :] for b in range(n_loc // LANES)],
                                     axis=1)
            else:
                vt = jnp.concatenate([v2c_ref[0, b, cols, :] for b in range(n_ctx_blk)], axis=1)
            out.append(_dot(vt, e[:, n * 2 * LANES:(n + 1) * 2 * LANES]))
        return jnp.concatenate(out, axis=1)

    def d_finish(bk, p0, o):
        for n, p in enumerate((p0, p0 + 1)):
            lo = o[0:HEAD_DIM, (2 * n) * LANES:(2 * n + 1) * LANES]
            hi = o[HEAD_DIM:PAIR, (2 * n + 1) * LANES:(2 * n + 2) * LANES]
            o_ref[0, bk * LANES:(bk + 1) * LANES, Q_WIDTH + p * PAIR:Q_WIDTH + (p + 1) * PAIR] = (
                jnp.concatenate([lo, hi], axis=0).T.astype(BF16))

    for bk in range(n_blk):
        for p0 in range(0, N_SLOTS // 2, 2):
            jobs.append(AttnJob(
                functools.partial(d_query, bk, p0),
                [functools.partial(d_local_scores, bk, p0),
                 lambda qt, p0=p0: _dot(k2c_ref[0, :, p0 * PAIR:(p0 + 2) * PAIR], qt)],
                [functools.partial(d_values, bk, p0, True), functools.partial(d_values, bk, p0, False)],
                functools.partial(d_finish, bk, p0)))
    _attend_jobs(jobs, s_ref)


def _attn_odd(qx, kv, sink, tp, *, n_ctx, tq, sink_order):
    q1, q2 = qx
    b, s = q1.shape[0] - 1, q1.shape[3]
    slot_spec = pl.BlockSpec((1, N_SLOTS, PAIR, tq), lambda i, j: (i, 0, 0, j))

    def full(a):
        return pl.BlockSpec((1,) + a.shape[1:], lambda i, j: (i,) + (0,) * (a.ndim - 1))

    rows = s // GRID_W
    span_rows = -(-(LANES // GRID_W - 1 + min(NA_ROWS, rows)) // 2) * 2
    chunks_per_job = 2
    key_i = np.arange(span_rows * GRID_W)[:, None]
    cq = np.arange(4 * LANES)[None, :] % GRID_W
    cs = np.clip(cq - NA_COLS // 2, 0, GRID_W - NA_COLS)
    ck = key_i % GRID_W
    col_ok = (ck >= cs) & (ck < cs + NA_COLS)
    window = jnp.asarray(key_i // GRID_W + np.where(col_ok, 0, NA_INVALID_COLUMN), jnp.int32)
    score_ring = pltpu.VMEM((chunks_per_job + ODD_EXTRA_LEAD + 1, max(3 * C_WINDOW, span_rows * GRID_W, n_ctx),
                             max(C_SLOTS_PER_JOB * C_WINDOW, 4 * LANES)), F32)
    return pl.pallas_call(
        functools.partial(_attn_odd_kernel, tq=tq, sink_order=sink_order),
        grid=(b, s // tq),
        scratch_shapes=[score_ring],
        in_specs=[slot_spec, slot_spec] + [full(a) for a in kv] + _ctx_specs(kv, n_ctx, blocked=True)
                 + [pl.BlockSpec(memory_space=pltpu.SMEM), pl.BlockSpec(tp.shape, lambda i, j: (0, 0, 0, 0)),
                    pl.BlockSpec(window.shape, lambda i, j: (0, 0))],
        out_specs=pl.BlockSpec((1, tq, 2 * Q_WIDTH), lambda i, j: (i, j, 0)),
        out_shape=jax.ShapeDtypeStruct((b, s, 2 * Q_WIDTH), BF16),
        compiler_params=_compiler_params(2),
        name="attn_odd",
    )(q1, q2, *kv, *kv, sink, tp, window)


def _post_kernel(m_ref, x_ref, mod_ref, g_ref, wo_ref, w1_ref, w2_ref, o_ref, *, ff_chunk, sub_rows):
    d = x_ref.shape[-1]

    def mod(k):
        return mod_ref[0, :, k * d:(k + 1) * d]

    def mixer_residual(rows):
        x1 = x_ref[0, rows] + mod(2) * _rms(_dot(m_ref[0, rows], wo_ref[...]), g_ref[1:2])
        return x1, (_rms(x1, g_ref[2:3]) * (1.0 + mod(4)) + mod(3)).astype(BF16)

    def mlp(h):
        f = None
        for c in range(w1_ref.shape[1] // ff_chunk):
            cols = slice(c * ff_chunk, (c + 1) * ff_chunk)
            a = jnp.maximum(_dot(h, w1_ref[:, cols]), 0.0)
            part = _dot((a * a).astype(BF16), w2_ref[cols, :])
            f = part if f is None else f + part
        return f

    tm = x_ref.shape[1]
    sub = min(sub_rows, tm)
    blocks = [slice(r, r + sub) for r in range(0, tm, sub)]
    stage1 = [mixer_residual(rows) for rows in blocks]
    stage2 = [mlp(h) for _, h in stage1]
    for rows, (x1, _), f in zip(blocks, stage1, stage2):
        o_ref[0, rows] = x1 + mod(5) * _rms(f, g_ref[3:4])


def _post(m, x, mod, gains, wo, w1, w2, *, tm, ff_chunk):
    b, s, d = x.shape
    bm = mod.shape[0]
    mod_map = (lambda i, j: (i, 0, 0)) if bm > 1 else (lambda i, j: (0, 0, 0))

    def const(a):
        return pl.BlockSpec(a.shape, lambda i, j: (0, 0), pipeline_mode=pl.Buffered(1))

    return pl.pallas_call(
        functools.partial(_post_kernel, ff_chunk=ff_chunk, sub_rows=POST_SUB_ROWS),
        grid=(b, s // tm),
        in_specs=[pl.BlockSpec((1, tm, m.shape[-1]), lambda i, j: (i, j, 0)),
                  pl.BlockSpec((1, tm, d), lambda i, j: (i, j, 0)),
                  pl.BlockSpec((1, 1, mod.shape[-1]), mod_map),
                  pl.BlockSpec(gains.shape, lambda i, j: (0, 0)),
                  const(wo), const(w1), const(w2)],
        out_specs=pl.BlockSpec((1, tm, d), lambda i, j: (i, j, 0)),
        out_shape=jax.ShapeDtypeStruct((b, s, d), F32),
        compiler_params=_compiler_params(2),
        name="post",
    )(m, x, mod, gains, wo, w1, w2)


def _rope_tables(t):
    tok = jnp.arange(t, dtype=jnp.int32)
    pos = jnp.stack([tok // GRID_W, tok % GRID_W], axis=-1).astype(F32)
    inv = ROPE_THETA ** (-jnp.arange(ROPE_FREQS, dtype=F32) / ROPE_FREQS)
    ang = pos[..., None] * inv
    cos, sin = jnp.cos(ang), jnp.sin(ang)
    rope_t = jnp.stack([cos.reshape(t, -1).T, sin.reshape(t, -1).T])
    zero = jnp.zeros_like(sin[:, 0])
    c = jnp.concatenate([cos[:, 0], cos[:, 0], cos[:, 1], cos[:, 1]], axis=-1)
    up = jnp.concatenate([-sin[:, 0], zero, -sin[:, 1], zero], axis=-1)
    down = jnp.concatenate([zero, sin[:, 0], zero, sin[:, 1]], axis=-1)
    rope_n = jnp.stack([jnp.tile(a, (1, PAIR // HEAD_DIM)) for a in (c, up, down)])
    one_t, one_n = jnp.ones_like(rope_t[0]), jnp.ones_like(rope_n[0])
    identity_t = jnp.stack([one_t, jnp.zeros_like(one_t)])
    identity_n = jnp.stack([one_n, jnp.zeros_like(one_n), jnp.zeros_like(one_n)])
    return jnp.stack([rope_t, identity_t]), jnp.stack([rope_n, identity_n])


def _pair_gqa_heads(w, axis):
    shape = w.shape
    w = w.reshape(shape[:axis] + (2, N_SLOTS // 2, HEAD_DIM) + shape[axis + 1:])
    return jnp.swapaxes(w, axis, axis + 1).reshape(shape)


def _lambda_init(layer):
    return 0.8 - 0.6 * math.exp(-0.3 * layer)


def kernel(x, c, ctx, c_ctx, w_mod, b_mod, norm_g, w_in, w_out, w_mlp_in, w_mlp_out, qk_norm_a, diff_lambda,
           diff_subln, sink_c, rpb_d):
    b, t, d = x.shape
    n_ctx = ctx.shape[1]
    depth = w_mod.shape[0]
    mod_rows = 16
    cc = jnp.concatenate([c, c_ctx[None], jnp.zeros((mod_rows - b - 1, d), F32)], axis=0)
    mod_all = _modulation(cc, w_mod, b_mod)

    rope_args = _rope_tables(t)
    seg = jnp.kron(jnp.eye(2, dtype=F32), jnp.ones((HEAD_DIM, HEAD_DIM), F32)).astype(BF16)
    tm_x = 2 * IN_SUB_ROWS
    assert b * n_ctx == t
    ctx = ctx.reshape(1, b * n_ctx, d)

    for l in range(depth):
        last = l == depth - 1
        even = l % 2 == 0
        i = l // 2
        mod_x = mod_all[l, :b][:, None, :]
        mod_c = mod_all[l, b][None, None, :]
        w = w_in[l]
        o_k1, o_v1, o_q2, o_k2, o_v2 = Q_WIDTH, Q_WIDTH + PAIR, Q_WIDTH + 2 * PAIR, 2 * Q_WIDTH + 2 * PAIR, \
            3 * Q_WIDTH + 2 * PAIR
        wt = jnp.concatenate([_pair_gqa_heads(w[:, 0:o_k1], 1), w[:, o_v1:o_q2], w[:, o_q2:o_k2], w[:, o_v2:]],
                             axis=1).T.astype(BF16)
        wk = jnp.concatenate([w[:, o_k1:o_v1], w[:, o_k2:o_v2]], axis=1).astype(BF16)
        wo = jnp.concatenate([_pair_gqa_heads(w_out[l, 0:Q_WIDTH], 0), w_out[l, Q_WIDTH:]], axis=0).astype(BF16)
        w1 = w_mlp_in[l].astype(BF16)
        w2 = w_mlp_out[l].astype(BF16)
        g0 = norm_g[l, 0][None, :]

        def norm_args(tm):
            if not even:
                return None
            return (jnp.broadcast_to(qk_norm_a[i, 0][:, None], (HEAD_DIM, tm)),
                    jnp.tile(qk_norm_a[i, 1], PAIR // HEAD_DIM)[None, :], seg)

        mod_xc = mod_all[l, :b + 1][:, None, :]
        q1, k1, v1, q2, k2, v2 = _in_proj(x, ctx, mod_xc, g0, wt, wk, norm_args(tm_x), rope_args,
                                          even=even, tm=tm_x)
        qx, kv = (q1, q2), (k1, v1, k2, v2)
        if even:
            lam0 = _lambda_init(l)
            sub = diff_subln[i][None, :]
            m_x = _attn_even(qx, kv, diff_lambda[i], sub, with_x=True, n_ctx=n_ctx, lam0=lam0, tq=256)
            if not last:
                m_c = _attn_even(qx, kv, diff_lambda[i], sub, with_x=False, n_ctx=n_ctx, lam0=lam0, tq=256)
        else:
            if not last:
                raise NotImplementedError("context update after an odd layer is not needed for depth 2")
            tp = _bias_table(rpb_d[i])
            m_x = _attn_odd(qx, kv, sink_c[i], tp, n_ctx=n_ctx, tq=512, sink_order=GQA_SLOT_HEADS)
        x = _post(m_x, x, mod_x, norm_g[l], wo, w1, w2, tm=512, ff_chunk=1024)
        if not last:
            ctx = _post(m_c.reshape(1, b * n_ctx, -1), ctx, mod_c, norm_g[l], wo, w1, w2, tm=512, ff_chunk=1024)
    return x
```

```python
import functools
import math
from typing import Any, Callable, NamedTuple, Sequence

import numpy as np
import jax
import jax.numpy as jnp
from jax import lax
from jax.experimental import pallas as pl
from jax.experimental.pallas import tpu as pltpu

F32 = jnp.float32
BF16 = jnp.bfloat16

GRID_W = 64
HEAD_DIM = 64
ROPE_THETA = 10000.0
ROPE_FREQS = HEAD_DIM // 4
N_SLOTS = 8
PAIR = 2 * HEAD_DIM
Q_WIDTH = N_SLOTS * HEAD_DIM
LANES = 128
MXU_QUERY_COLS = 512
EVEN_A_QUERY_COLS = MXU_QUERY_COLS
EVEN_KEY_CHUNK = 1024
EVEN_EXTRA_LEAD = 2
ODD_EXTRA_LEAD = 3
IN_SUB_ROWS = MXU_QUERY_COLS
POST_SUB_ROWS = 256
C_SLOTS_PER_JOB = 8
NA_CHUNK_ROWS = 4
NA_INVALID_COLUMN = 1 << 20
C_WINDOW = 128
NA_ROWS = 8
NA_COLS = 16
EPS = 1e-6
NEG = -1e30
LOG2E = math.log2(math.e)
Q_SCALE = HEAD_DIM ** -0.5 * LOG2E
GQA_SLOT_HEADS = tuple(half * (N_SLOTS // 2) + p for p in range(N_SLOTS // 2) for half in range(2))

VMEM_LIMIT_BYTES = 56 * 1024 * 1024


def _compiler_params(n_grid_dims):
    return pltpu.CompilerParams(dimension_semantics=("arbitrary",) * n_grid_dims,
                                vmem_limit_bytes=VMEM_LIMIT_BYTES)


def _dot(a, b):
    return jnp.dot(a, b, preferred_element_type=F32)


def _dot_nt(a, b):
    return lax.dot_general(a, b, (((1,), (1,)), ((), ())), preferred_element_type=F32)


def _log2(n):
    assert n & (n - 1) == 0
    return n.bit_length() - 1


def _rms(x, gain):
    ms = jnp.mean(x * x, axis=-1, keepdims=True)
    return x * lax.rsqrt(ms + EPS) * gain


def _mod_kernel(c_ref, w_ref, b_ref, o_ref):
    c = c_ref[...]
    s = c * jax.nn.sigmoid(c)
    o_ref[0] = _dot(s.astype(BF16), w_ref[0].astype(BF16)) + b_ref[0]


def _modulation(cc, w_mod, b_mod):
    depth, d, n = w_mod.shape
    rows = cc.shape[0]
    tn = n // 4
    return pl.pallas_call(
        _mod_kernel,
        grid=(depth, n // tn),
        in_specs=[pl.BlockSpec((rows, d), lambda l, j: (0, 0)),
                  pl.BlockSpec((1, d, tn), lambda l, j: (l, 0, j)),
                  pl.BlockSpec((1, 1, tn), lambda l, j: (l, 0, j))],
        out_specs=pl.BlockSpec((1, rows, tn), lambda l, j: (l, 0, j)),
        out_shape=jax.ShapeDtypeStruct((depth, rows, n), F32),
        compiler_params=_compiler_params(2),
        name="modulation",
    )(cc, w_mod, b_mod.reshape(depth, 1, n))


def _in_kernel(*refs, even, n_batch, sub_rows):
    refs = list(refs)
    tok_ref, ctx_ref, mod_ref, g_ref, wt_ref, wk_ref = refs[:6]
    del refs[:6]
    if even:
        gq_ref, gk_ref, seg_ref = refs[:3]
        del refs[:3]
    rope_t_ref, rope_ref = refs[:2]
    q1_ref, k1_ref, v1_ref, q2_ref, k2_ref, v2_ref = refs[2:]
    d = tok_ref.shape[-1]
    tm = tok_ref.shape[1]
    sub = min(sub_rows, tm)
    is_ctx = pl.program_id(0) == n_batch

    def normed(rows):
        h = _rms(jnp.where(is_ctx, ctx_ref[0, rows], tok_ref[0, rows]), g_ref[...])
        h = h * (1.0 + mod_ref[0, :, d:2 * d]) + mod_ref[0, :, 0:d]
        return h.astype(BF16)

    def project(rows, hb):
        def rotary_t(c):
            f = ROPE_FREQS
            out = []
            for a in range(2):
                x1, x2 = c[2 * a * f:(2 * a + 1) * f], c[(2 * a + 1) * f:(2 * a + 2) * f]
                cos, sin = rope_t_ref[0, 0, a * f:(a + 1) * f, rows], rope_t_ref[0, 1, a * f:(a + 1) * f, rows]
                out += [x1 * cos - x2 * sin, x2 * cos + x1 * sin]
            return jnp.concatenate(out, axis=0)

        def store_slots(q_ref, qt, use_norm, use_rope):
            zeros = jnp.zeros((HEAD_DIM, sub), F32)
            for j in range(N_SLOTS):
                c = qt[j * HEAD_DIM:(j + 1) * HEAD_DIM]
                if use_norm:
                    ms = jnp.mean(c * c, axis=0, keepdims=True)
                    c = c * lax.rsqrt(ms + EPS) * gq_ref[:, 0:sub]
                if use_rope:
                    c = rotary_t(c)
                c = c * Q_SCALE
                slot = jnp.concatenate([c, zeros] if j % 2 == 0 else [zeros, c], axis=0)
                q_ref[0, j, :, rows] = slot.astype(BF16)

        def store_vt(v_ref, vt):
            if even:
                v_ref[0, :, rows] = vt.astype(BF16)
            else:
                for c in range(sub // LANES):
                    v_ref[0, rows.start // LANES + c] = vt[:, c * LANES:(c + 1) * LANES].astype(BF16)

        def rotary(c):
            up = pltpu.roll(c, PAIR - ROPE_FREQS, axis=1)
            down = pltpu.roll(c, ROPE_FREQS, axis=1)
            return c * rope_ref[0, 0, rows] + up * rope_ref[0, 1, rows] + down * rope_ref[0, 2, rows]

        o_v1, o_q2, o_v2 = Q_WIDTH, Q_WIDTH + PAIR, 2 * Q_WIDTH + PAIR
        project_keys(rows, hb, rotary)
        store_slots(q1_ref, _dot_nt(wt_ref[0:o_v1], hb), even, True)
        store_slots(q2_ref, _dot_nt(wt_ref[o_q2:o_v2], hb), False, even)
        store_vt(v1_ref, _dot_nt(wt_ref[o_v1:o_q2], hb))
        store_vt(v2_ref, _dot_nt(wt_ref[o_v2:o_v2 + Q_WIDTH], hb))

    def project_keys(rows, hb, rotary):
        k = _dot(hb, wk_ref[...])
        k1 = k[:, 0:PAIR]
        if even:
            sq = k1 * k1
            hi = sq.astype(BF16)
            lo = (sq - hi.astype(F32)).astype(BF16)
            ms = (_dot(hi, seg_ref[...]) + _dot(lo, seg_ref[...])) * (1.0 / HEAD_DIM)
            k1 = k1 * lax.rsqrt(ms + EPS) * gk_ref[...]
        k1 = rotary(k1)
        k1_ref[0, rows] = k1.astype(BF16)
        for j in range(Q_WIDTH // PAIR):
            c = k[:, PAIR + j * PAIR:PAIR + (j + 1) * PAIR]
            if even:
                c = rotary(c)
            k2_ref[0, rows, j * PAIR:(j + 1) * PAIR] = c.astype(BF16)

    blocks = [slice(r, r + sub) for r in range(0, tm, sub)]
    hbs = [normed(rows) for rows in blocks]
    for rows, hb in zip(blocks, hbs):
        project(rows, hb)


def _in_proj(tok, ctx, mod, gain, wt, wk, norm_args, rope_args, *, even, tm):
    n, s, d = tok.shape
    assert ctx.shape == (1, s, d) and mod.shape[0] == n + 1
    b = n + 1

    def const(a):
        return pl.BlockSpec(a.shape, lambda i, j: (0,) * a.ndim)

    def is_ctx(i):
        return (i == n).astype(jnp.int32)

    in_specs = [pl.BlockSpec((1, tm, d), lambda i, j: (jnp.minimum(i, n - 1), j, 0)),
                pl.BlockSpec((1, tm, d), lambda i, j: (0, is_ctx(i) * j, 0)),
                pl.BlockSpec((1, 1, mod.shape[-1]), lambda i, j: (i, 0, 0)),
                const(gain), const(wt), const(wk)]
    args = [tok, ctx, mod, gain, wt, wk]
    if even:
        gq, gk, seg = norm_args
        in_specs += [const(gq), const(gk), const(seg)]
        args += [gq, gk, seg]
    rope_t, rope_n = rope_args
    in_specs += [pl.BlockSpec((1, 2, 2 * ROPE_FREQS, tm), lambda i, j: (is_ctx(i), 0, 0, j)),
                 pl.BlockSpec((1, 3, tm, PAIR), lambda i, j: (is_ctx(i), 0, j, 0))]
    args += [rope_t, rope_n]

    def vt_out(rows):
        if even:
            return (pl.BlockSpec((1, rows, tm), lambda i, j: (i, 0, j)),
                    jax.ShapeDtypeStruct((b, rows, s), BF16))
        return (pl.BlockSpec((1, tm // LANES, rows, LANES), lambda i, j: (i, j, 0, 0)),
                jax.ShapeDtypeStruct((b, s // LANES, rows, LANES), BF16))

    slot_out = (pl.BlockSpec((1, N_SLOTS, PAIR, tm), lambda i, j: (i, 0, 0, j)),
                jax.ShapeDtypeStruct((b, N_SLOTS, PAIR, s), BF16))

    def nat_out(cols):
        return (pl.BlockSpec((1, tm, cols), lambda i, j: (i, j, 0)), jax.ShapeDtypeStruct((b, s, cols), BF16))

    outs = [slot_out, nat_out(PAIR), vt_out(PAIR), slot_out, nat_out(Q_WIDTH), vt_out(Q_WIDTH)]
    return pl.pallas_call(
        functools.partial(_in_kernel, even=even, n_batch=n, sub_rows=IN_SUB_ROWS),
        grid=(b, s // tm),
        in_specs=in_specs,
        out_specs=[o[0] for o in outs],
        out_shape=[o[1] for o in outs],
        compiler_params=_compiler_params(2),
        name="in_proj_" + ("even" if even else "odd"),
    )(*args)


class AttnJob(NamedTuple):
    query: Callable[[], Any]
    scores: Sequence[Callable[[Any], Any]]
    values: Sequence[Callable[[Any], Any]]
    finish: Callable[[Any], None]
    sink: Any = None


def _attend_jobs(jobs, s_ref):
    items = [(j, c) for j, job in enumerate(jobs) for c in range(len(job.scores))]
    n_slots = s_ref.shape[0]
    lead = n_slots - 1
    assert lead > max(len(job.scores) for job in jobs)
    qt, shape, m, l, o = {}, {}, {}, {}, {}
    for t in range(len(items) + lead):
        if t < len(items):
            j, c = items[t]
            job = jobs[j]
            if c == 0:
                qt[j] = job.query()
            x = job.scores[c](qt[j])
            shape[t] = x.shape
            s_ref[t % n_slots, 0:x.shape[0], 0:x.shape[1]] = x
            part_m = jnp.max(x, axis=0, keepdims=True)
            if c == 0:
                m[j] = part_m if job.sink is None else jnp.maximum(part_m, job.sink)
            else:
                m[j] = jnp.maximum(m[j], part_m)
        if t >= lead:
            j, c = items[t - lead]
            job = jobs[j]
            rows, cols = shape.pop(t - lead)
            e = jnp.exp2(s_ref[(t - lead) % n_slots, 0:rows, 0:cols] - m[j])
            part_l = jnp.sum(e, axis=0, keepdims=True)
            part_o = job.values[c](e.astype(BF16))
            l[j] = part_l if c == 0 else l[j] + part_l
            o[j] = part_o if c == 0 else o[j] + part_o
            if c == len(job.scores) - 1:
                denom = l.pop(j) if job.sink is None else l.pop(j) + jnp.exp2(job.sink - m[j])
                job.finish(o.pop(j) * (1.0 / denom))


def _attn_even_kernel(*refs, with_x, lam0, tq, key_chunk):
    if with_x:
        q1_ref, q2_ref, k1c_ref, v1c_ref, k2c_ref, v2c_ref, k1x_ref, v1x_ref, k2x_ref, v2x_ref = refs[:10]
    else:
        q1_ref, q2_ref, k1c_ref, v1c_ref, k2c_ref, v2c_ref = refs[:6]
    lam_ref, sub_ref, o_ref, s_ref = refs[-4:]

    def chunk_fns(c_ref, x_ref, cols, transposed):
        def make(ref, lo, hi):
            if transposed:
                return lambda e: _dot(ref[0, cols, lo:hi], e)
            return lambda qt: _dot(ref[0, lo:hi, cols], qt)
        n_ctx = c_ref.shape[2] if transposed else c_ref.shape[1]
        fns = [make(c_ref, 0, n_ctx)]
        if with_x:
            n_x = x_ref.shape[2] if transposed else x_ref.shape[1]
            fns += [make(x_ref, lo, lo + key_chunk) for lo in range(0, n_x, key_chunk)]
        return fns

    group = max(2, EVEN_A_QUERY_COLS // tq)
    jobs = []
    pair_cols = slice(0, PAIR)

    def a_query(g):
        return jnp.concatenate([q1_ref[0, g * group + j] for j in range(group)], axis=1)

    def a_finish(g, o):
        for pp in range(group // 2):
            pair_t = jnp.concatenate([o[0:HEAD_DIM, (2 * pp) * tq:(2 * pp + 1) * tq],
                                      o[HEAD_DIM:PAIR, (2 * pp + 1) * tq:(2 * pp + 2) * tq]], axis=0)
            col = (g * (group // 2) + pp) * PAIR
            o_ref[0, :, col:col + PAIR] = pair_t.T.astype(BF16)

    for g in range(N_SLOTS // group):
        jobs.append(AttnJob(functools.partial(a_query, g),
                            chunk_fns(k1c_ref, k1x_ref if with_x else None, pair_cols, False),
                            chunk_fns(v1c_ref, v1x_ref if with_x else None, pair_cols, True),
                            functools.partial(a_finish, g)))

    def b_query(h):
        return jnp.concatenate([q2_ref[0, 2 * h], q2_ref[0, 2 * h + 1]], axis=1)

    def b_finish(h, o):
        lp = lam_ref[...]
        lam = (jnp.exp(jnp.sum(lp[0:1] * lp[1:2], axis=-1, keepdims=True))
               - jnp.exp(jnp.sum(lp[2:3] * lp[3:4], axis=-1, keepdims=True)) + lam0)
        ob = (o[:, 0:tq] - lam * o[:, tq:2 * tq]).T
        ob = _rms(ob, sub_ref[...]) * (1.0 - lam0)
        col = Q_WIDTH + h * PAIR
        o_ref[0, :, col:col + PAIR] = ob.astype(BF16)

    for h in range(N_SLOTS // 2):
        cols = slice(h * PAIR, (h + 1) * PAIR)
        jobs.append(AttnJob(functools.partial(b_query, h),
                            chunk_fns(k2c_ref, k2x_ref if with_x else None, cols, False),
                            chunk_fns(v2c_ref, v2x_ref if with_x else None, cols, True),
                            functools.partial(b_finish, h)))
    _attend_jobs(jobs, s_ref)


def _ctx_specs(kv, n_ctx, blocked):
    row = kv[0].shape[0] - 1

    def natural(a):
        return pl.BlockSpec((1, n_ctx, a.shape[2]), lambda i, j: (row, i, 0))

    def transposed(a):
        if blocked:
            return pl.BlockSpec((1, n_ctx // LANES) + a.shape[2:], lambda i, j: (row, i, 0, 0))
        return pl.BlockSpec((1, a.shape[1], n_ctx), lambda i, j: (row, 0, i))

    k1, v1, k2, v2 = kv
    return [natural(k1), transposed(v1), natural(k2), transposed(v2)]


def _attn_even(qx, kv, lam_p, subln, *, with_x, n_ctx, lam0, tq):
    q1, q2 = qx
    row = q1.shape[0] - 1
    if with_x:
        b, s = row, q1.shape[3]
        slot_spec = pl.BlockSpec((1, N_SLOTS, PAIR, tq), lambda i, j: (i, 0, 0, j))
    else:
        b, s = q1.shape[3] // n_ctx, n_ctx
        slot_spec = pl.BlockSpec((1, N_SLOTS, PAIR, tq), lambda i, j: (row, 0, 0, i * (n_ctx // tq) + j))

    def full(a):
        return pl.BlockSpec((1,) + a.shape[1:], lambda i, j: (i, 0, 0))

    n_x = kv[0].shape[1]
    key_chunk = min(EVEN_KEY_CHUNK, n_x) if with_x else n_ctx
    chunks_per_job = 1 + (n_x // key_chunk if with_x else 0)
    score_ring = pltpu.VMEM((chunks_per_job + EVEN_EXTRA_LEAD + 1, max(n_ctx, key_chunk),
                             max(2 * tq, EVEN_A_QUERY_COLS)), F32)
    return pl.pallas_call(
        functools.partial(_attn_even_kernel, with_x=with_x, lam0=lam0, tq=tq, key_chunk=key_chunk),
        grid=(b, s // tq),
        scratch_shapes=[score_ring],
        in_specs=[slot_spec, slot_spec] + _ctx_specs(kv, n_ctx, blocked=False)
                 + ([full(a) for a in kv] if with_x else [])
                 + [pl.BlockSpec(lam_p.shape, lambda i, j: (0, 0)), pl.BlockSpec(subln.shape, lambda i, j: (0, 0))],
        out_specs=pl.BlockSpec((1, tq, 2 * Q_WIDTH), lambda i, j: (i, j, 0)),
        out_shape=jax.ShapeDtypeStruct((b, s, 2 * Q_WIDTH), BF16),
        compiler_params=_compiler_params(2),
        name="attn_even" + ("_x" if with_x else "_ctx"),
    )(q1, q2, *kv, *(kv if with_x else ()), lam_p, subln)


N_BIAS_ROWS = 2 * NA_ROWS


def _bias_kernel(rpb_ref, o_ref):
    head = pl.program_id(0)
    n_dr = 2 * NA_ROWS - 1
    n_dc = 2 * NA_COLS - 1
    shape = (GRID_W, LANES)
    ck = lax.broadcasted_iota(jnp.int32, shape, 0)
    lane = lax.broadcasted_iota(jnp.int32, shape, 1)
    second = lane >= GRID_W
    dcl = jnp.clip(ck - (lane & (GRID_W - 1)), -(NA_COLS - 1), NA_COLS - 1) + (NA_COLS - 1)
    tiles = []
    for d in range(n_dr):
        acc = jnp.zeros(shape, F32)
        for j in range(n_dc):
            acc = jnp.where(dcl == j, rpb_ref[(head * n_dr + d) * n_dc + j], acc)
        tiles.append(acc * LOG2E)
    for t in range(N_BIAS_ROWS):
        o_ref[0, t] = jnp.where(second, tiles[max(t - 1, 0)], tiles[min(t, n_dr - 1)])


def _bias_table(rpb):
    heads = rpb.shape[0]
    return pl.pallas_call(
        _bias_kernel,
        grid=(heads,),
        in_specs=[pl.BlockSpec(memory_space=pltpu.SMEM)],
        out_specs=pl.BlockSpec((1, N_BIAS_ROWS, GRID_W, LANES), lambda h: (h, 0, 0, 0)),
        out_shape=jax.ShapeDtypeStruct((heads, N_BIAS_ROWS, GRID_W, LANES), F32),
        compiler_params=_compiler_params(1),
        name="rel_bias_table",
    )(rpb.reshape(-1))


def _attn_odd_kernel(q1_ref, q2_ref, k1x_ref, v1x_ref, k2x_ref, v2x_ref, k1c_ref, v1c_ref, k2c_ref, v2c_ref,
                     sink_ref, tp_ref, win_ref, o_ref, s_ref, *, tq, sink_order):
    i = pl.program_id(1)
    t = k1x_ref.shape[1]
    n_ctx_blk = v1c_ref.shape[1]
    rows = t // GRID_W
    kh = min(NA_ROWS, rows)
    n_blk = tq // LANES

    jobs = []

    n_q = N_SLOTS * C_WINDOW
    n_seq_blk = t // C_WINDOW
    lane_q = lax.broadcasted_iota(jnp.int32, (1, n_q), 1)
    sink = jnp.zeros((1, n_q), F32)
    for j in range(N_SLOTS):
        sink = jnp.where(lane_q >> _log2(C_WINDOW) == j, sink_ref[sink_order[j]], sink)
    sink = sink * LOG2E
    n_qj = C_SLOTS_PER_JOB * C_WINDOW
    tri = (lax.broadcasted_iota(jnp.int32, (C_WINDOW, n_qj), 0)
           - (lax.broadcasted_iota(jnp.int32, (C_WINDOW, n_qj), 1) & (C_WINDOW - 1)))

    def c_blocks(jb):
        blk = i * n_blk + jb
        return blk, jnp.maximum(blk - 1, 0), jnp.minimum(blk + 1, n_seq_blk - 1)

    def c_query(jb, s0):
        return jnp.concatenate([q1_ref[0, j, :, jb * LANES:(jb + 1) * LANES]
                                for j in range(s0, s0 + C_SLOTS_PER_JOB)], axis=1)

    def c_local_scores(jb, qt):
        blk, prev, nxt = c_blocks(jb)
        k_loc = jnp.concatenate([k1x_ref[0, pl.ds(pl.multiple_of(n * C_WINDOW, C_WINDOW), C_WINDOW), :]
                                 for n in (prev, blk, nxt)], axis=0)
        s = _dot(k_loc, qt)
        lo_bound = jnp.where(blk > 0, 0, C_WINDOW)
        hi_bound = jnp.where(blk < n_seq_blk - 1, 0, -C_WINDOW)
        return jnp.concatenate([jnp.where(tri >= lo_bound, s[0:C_WINDOW], NEG),
                                s[C_WINDOW:2 * C_WINDOW],
                                jnp.where(tri <= hi_bound, s[2 * C_WINDOW:3 * C_WINDOW], NEG)], axis=0)

    def c_local_values(jb, e):
        blk, prev, nxt = c_blocks(jb)
        return _dot(jnp.concatenate([v1x_ref[0, n] for n in (prev, blk, nxt)], axis=1), e)

    def c_finish(jb, s0, o):
        for p in range(C_SLOTS_PER_JOB // 2):
            pair_t = jnp.concatenate([o[0:HEAD_DIM, (2 * p) * LANES:(2 * p + 1) * LANES],
                                      o[HEAD_DIM:PAIR, (2 * p + 1) * LANES:(2 * p + 2) * LANES]], axis=0)
            col = (s0 // 2 + p) * PAIR
            o_ref[0, jb * LANES:(jb + 1) * LANES, col:col + PAIR] = pair_t.T.astype(BF16)

    for jb in range(n_blk):
        for s0 in range(0, N_SLOTS, C_SLOTS_PER_JOB):
            jobs.append(AttnJob(
                functools.partial(c_query, jb, s0),
                [functools.partial(c_local_scores, jb), lambda qt: _dot(k1c_ref[0], qt)],
                [functools.partial(c_local_values, jb),
                 lambda e: _dot(jnp.concatenate([v1c_ref[0, c] for c in range(n_ctx_blk)], axis=1), e)],
                functools.partial(c_finish, jb, s0), sink[:, s0 * C_WINDOW:(s0 + C_SLOTS_PER_JOB) * C_WINDOW]))

    blk_rows = LANES // GRID_W
    span_rows = -(-(blk_rows - 1 + kh) // 2) * 2
    n_loc = span_rows * GRID_W
    q_row = (lax.broadcasted_iota(jnp.int32, (1, 4 * LANES), 1) >> _log2(GRID_W)) & (blk_rows - 1)

    def d_rows(bk):
        r0 = (i * n_blk + bk) * blk_rows
        return r0, jnp.minimum(jnp.clip(r0 - kh // 2, 0, rows - kh), rows - span_rows)

    def d_query(bk, p0):
        tok = slice(bk * LANES, (bk + 1) * LANES)
        blocks = [jnp.concatenate([q2_ref[0, 2 * p, :, tok], q2_ref[0, 2 * p + 1, :, tok]], axis=1)
                  for p in (p0, p0 + 1)]
        zeros = jnp.zeros_like(blocks[0])
        return jnp.concatenate([jnp.concatenate([blocks[0], zeros], axis=1),
                                jnp.concatenate([zeros, blocks[1]], axis=1)], axis=0)

    def d_local_scores(bk, p0, qt):
        r0, u = d_rows(bk)
        k_loc = k2x_ref[0, pl.ds(pl.multiple_of(u * GRID_W, LANES), n_loc), p0 * PAIR:(p0 + 2) * PAIR]
        bias = jnp.concatenate(
            [jnp.concatenate([tp_ref[2 * p0 + lb, jnp.clip(u + a - r0 + NA_ROWS - 1, 0, N_BIAS_ROWS - 1)]
                              for lb in range(4)], axis=1)
             for a in range(span_rows)], axis=0)
        first = jnp.zeros((1, 4 * LANES), jnp.int32)
        for j in range(blk_rows):
            first = jnp.where(q_row == j, jnp.clip(r0 + j - kh // 2, 0, rows - kh) - u, first)
        valid = lax.bitcast_convert_type(win_ref[...] - first, jnp.uint32) < jnp.uint32(kh)
        return jnp.where(valid, _dot(k_loc, qt) + bias, NEG)

    def d_values(bk, p0, local, e):
        _, u = d_rows(bk)
        out = []
        for n, p in enumerate((p0, p0 + 1)):
            cols = slice(p * PAIR, (p + 1) * PAIR)
            if local:
                vt = jnp.concatenate([v2x_ref[0, u // blk_rows + b, cols, :] for b in range(n_loc // LANES)],
                                     axis=1)
            else:
                vt = jnp.concatenate([v2c_ref[0, b, cols, :] for b in range(n_ctx_blk)], axis=1)
            out.append(_dot(vt, e[:, n * 2 * LANES:(n + 1) * 2 * LANES]))
        return jnp.concatenate(out, axis=1)

    def d_finish(bk, p0, o):
        for n, p in enumerate((p0, p0 + 1)):
            lo = o[0:HEAD_DIM, (2 * n) * LANES:(2 * n + 1) * LANES]
            hi = o[HEAD_DIM:PAIR, (2 * n + 1) * LANES:(2 * n + 2) * LANES]
            o_ref[0, bk * LANES:(bk + 1) * LANES, Q_WIDTH + p * PAIR:Q_WIDTH + (p + 1) * PAIR] = (
                jnp.concatenate([lo, hi], axis=0).T.astype(BF16))

    for bk in range(n_blk):
        for p0 in range(0, N_SLOTS // 2, 2):
            jobs.append(AttnJob(
                functools.partial(d_query, bk, p0),
                [functools.partial(d_local_scores, bk, p0),
                 lambda qt, p0=p0: _dot(k2c_ref[0, :, p0 * PAIR:(p0 + 2) * PAIR], qt)],
                [functools.partial(d_values, bk, p0, True), functools.partial(d_values, bk, p0, False)],
                functools.partial(d_finish, bk, p0)))
    _attend_jobs(jobs, s_ref)


def _attn_odd(qx, kv, sink, tp, *, n_ctx, tq, sink_order):
    q1, q2 = qx
    b, s = q1.shape[0] - 1, q1.shape[3]
    slot_spec = pl.BlockSpec((1, N_SLOTS, PAIR, tq), lambda i, j: (i, 0, 0, j))

    def full(a):
        return pl.BlockSpec((1,) + a.shape[1:], lambda i, j: (i,) + (0,) * (a.ndim - 1))

    rows = s // GRID_W
    span_rows = -(-(LANES // GRID_W - 1 + min(NA_ROWS, rows)) // 2) * 2
    chunks_per_job = 2
    key_i = np.arange(span_rows * GRID_W)[:, None]
    cq = np.arange(4 * LANES)[None, :] % GRID_W
    cs = np.clip(cq - NA_COLS // 2, 0, GRID_W - NA_COLS)
    ck = key_i % GRID_W
    col_ok = (ck >= cs) & (ck < cs + NA_COLS)
    window = jnp.asarray(key_i // GRID_W + np.where(col_ok, 0, NA_INVALID_COLUMN), jnp.int32)
    score_ring = pltpu.VMEM((chunks_per_job + ODD_EXTRA_LEAD + 1, max(3 * C_WINDOW, span_rows * GRID_W, n_ctx),
                             max(C_SLOTS_PER_JOB * C_WINDOW, 4 * LANES)), F32)
    return pl.pallas_call(
        functools.partial(_attn_odd_kernel, tq=tq, sink_order=sink_order),
        grid=(b, s // tq),
        scratch_shapes=[score_ring],
        in_specs=[slot_spec, slot_spec] + [full(a) for a in kv] + _ctx_specs(kv, n_ctx, blocked=True)
                 + [pl.BlockSpec(memory_space=pltpu.SMEM), pl.BlockSpec(tp.shape, lambda i, j: (0, 0, 0, 0)),
                    pl.BlockSpec(window.shape, lambda i, j: (0, 0))],
        out_specs=pl.BlockSpec((1, tq, 2 * Q_WIDTH), lambda i, j: (i, j, 0)),
        out_shape=jax.ShapeDtypeStruct((b, s, 2 * Q_WIDTH), BF16),
        compiler_params=_compiler_params(2),
        name="attn_odd",
    )(q1, q2, *kv, *kv, sink, tp, window)


def _post_kernel(m_ref, x_ref, mod_ref, g_ref, wo_ref, w1_ref, w2_ref, o_ref, *, ff_chunk, sub_rows):
    d = x_ref.shape[-1]

    def mod(k):
        return mod_ref[0, :, k * d:(k + 1) * d]

    def mixer_residual(rows):
        x1 = x_ref[0, rows] + mod(2) * _rms(_dot(m_ref[0, rows], wo_ref[...]), g_ref[1:2])
        return x1, (_rms(x1, g_ref[2:3]) * (1.0 + mod(4)) + mod(3)).astype(BF16)

    def mlp(h):
        f = None
        for c in range(w1_ref.shape[1] // ff_chunk):
            cols = slice(c * ff_chunk, (c + 1) * ff_chunk)
            a = jnp.maximum(_dot(h, w1_ref[:, cols]), 0.0)
            part = _dot((a * a).astype(BF16), w2_ref[cols, :])
            f = part if f is None else f + part
        return f

    tm = x_ref.shape[1]
    sub = min(sub_rows, tm)
    blocks = [slice(r, r + sub) for r in range(0, tm, sub)]
    stage1 = [mixer_residual(rows) for rows in blocks]
    stage2 = [mlp(h) for _, h in stage1]
    for rows, (x1, _), f in zip(blocks, stage1, stage2):
        o_ref[0, rows] = x1 + mod(5) * _rms(f, g_ref[3:4])


def _post(m, x, mod, gains, wo, w1, w2, *, tm, ff_chunk):
    b, s, d = x.shape
    bm = mod.shape[0]
    mod_map = (lambda i, j: (i, 0, 0)) if bm > 1 else (lambda i, j: (0, 0, 0))

    def const(a):
        return pl.BlockSpec(a.shape, lambda i, j: (0, 0), pipeline_mode=pl.Buffered(1))

    return pl.pallas_call(
        functools.partial(_post_kernel, ff_chunk=ff_chunk, sub_rows=POST_SUB_ROWS),
        grid=(b, s // tm),
        in_specs=[pl.BlockSpec((1, tm, m.shape[-1]), lambda i, j: (i, j, 0)),
                  pl.BlockSpec((1, tm, d), lambda i, j: (i, j, 0)),
                  pl.BlockSpec((1, 1, mod.shape[-1]), mod_map),
                  pl.BlockSpec(gains.shape, lambda i, j: (0, 0)),
                  const(wo), const(w1), const(w2)],
        out_specs=pl.BlockSpec((1, tm, d), lambda i, j: (i, j, 0)),
        out_shape=jax.ShapeDtypeStruct((b, s, d), F32),
        compiler_params=_compiler_params(2),
        name="post",
    )(m, x, mod, gains, wo, w1, w2)


def _rope_tables(t):
    tok = jnp.arange(t, dtype=jnp.int32)
    pos = jnp.stack([tok // GRID_W, tok % GRID_W], axis=-1).astype(F32)
    inv = ROPE_THETA ** (-jnp.arange(ROPE_FREQS, dtype=F32) / ROPE_FREQS)
    ang = pos[..., None] * inv
    cos, sin = jnp.cos(ang), jnp.sin(ang)
    rope_t = jnp.stack([cos.reshape(t, -1).T, sin.reshape(t, -1).T])
    zero = jnp.zeros_like(sin[:, 0])
    c = jnp.concatenate([cos[:, 0], cos[:, 0], cos[:, 1], cos[:, 1]], axis=-1)
    up = jnp.concatenate([-sin[:, 0], zero, -sin[:, 1], zero], axis=-1)
    down = jnp.concatenate([zero, sin[:, 0], zero, sin[:, 1]], axis=-1)
    rope_n = jnp.stack([jnp.tile(a, (1, PAIR // HEAD_DIM)) for a in (c, up, down)])
    one_t, one_n = jnp.ones_like(rope_t[0]), jnp.ones_like(rope_n[0])
    identity_t = jnp.stack([one_t, jnp.zeros_like(one_t)])
    identity_n = jnp.stack([one_n, jnp.zeros_like(one_n), jnp.zeros_like(one_n)])
    return jnp.stack([rope_t, identity_t]), jnp.stack([rope_n, identity_n])


def _pair_gqa_heads(w, axis):
    shape = w.shape
    w = w.reshape(shape[:axis] + (2, N_SLOTS // 2, HEAD_DIM) + shape[axis + 1:])
    return jnp.swapaxes(w, axis, axis + 1).reshape(shape)


def _lambda_init(layer):
    return 0.8 - 0.6 * math.exp(-0.3 * layer)


def kernel(x, c, ctx, c_ctx, w_mod, b_mod, norm_g, w_in, w_out, w_mlp_in, w_mlp_out, qk_norm_a, diff_lambda,
           diff_subln, sink_c, rpb_d):
    b, t, d = x.shape
    n_ctx = ctx.shape[1]
    depth = w_mod.shape[0]
    mod_rows = 16
    cc = jnp.concatenate([c, c_ctx[None], jnp.zeros((mod_rows - b - 1, d), F32)], axis=0)
    mod_all = _modulation(cc, w_mod, b_mod)

    rope_args = _rope_tables(t)
    seg = jnp.kron(jnp.eye(2, dtype=F32), jnp.ones((HEAD_DIM, HEAD_DIM), F32)).astype(BF16)
    tm_x = 2 * IN_SUB_ROWS
    assert b * n_ctx == t
    ctx = ctx.reshape(1, b * n_ctx, d)

    for l in range(depth):
        last = l == depth - 1
        even = l % 2 == 0
        i = l // 2
        mod_x = mod_all[l, :b][:, None, :]
        mod_c = mod_all[l, b][None, None, :]
        w = w_in[l]
        o_k1, o_v1, o_q2, o_k2, o_v2 = Q_WIDTH, Q_WIDTH + PAIR, Q_WIDTH + 2 * PAIR, 2 * Q_WIDTH + 2 * PAIR, \
            3 * Q_WIDTH + 2 * PAIR
        wt = jnp.concatenate([_pair_gqa_heads(w[:, 0:o_k1], 1), w[:, o_v1:o_q2], w[:, o_q2:o_k2], w[:, o_v2:]],
                             axis=1).T.astype(BF16)
        wk = jnp.concatenate([w[:, o_k1:o_v1], w[:, o_k2:o_v2]], axis=1).astype(BF16)
        wo = jnp.concatenate([_pair_gqa_heads(w_out[l, 0:Q_WIDTH], 0), w_out[l, Q_WIDTH:]], axis=0).astype(BF16)
        w1 = w_mlp_in[l].astype(BF16)
        w2 = w_mlp_out[l].astype(BF16)
        g0 = norm_g[l, 0][None, :]

        def norm_args(tm):
            if not even:
                return None
            return (jnp.broadcast_to(qk_norm_a[i, 0][:, None], (HEAD_DIM, tm)),
                    jnp.tile(qk_norm_a[i, 1], PAIR // HEAD_DIM)[None, :], seg)

        mod_xc = mod_all[l, :b + 1][:, None, :]
        q1, k1, v1, q2, k2, v2 = _in_proj(x, ctx, mod_xc, g0, wt, wk, norm_args(tm_x), rope_args,
                                          even=even, tm=tm_x)
        qx, kv = (q1, q2), (k1, v1, k2, v2)
        if even:
            lam0 = _lambda_init(l)
            sub = diff_subln[i][None, :]
            m_x = _attn_even(qx, kv, diff_lambda[i], sub, with_x=True, n_ctx=n_ctx, lam0=lam0, tq=256)
            if not last:
                m_c = _attn_even(qx, kv, diff_lambda[i], sub, with_x=False, n_ctx=n_ctx, lam0=lam0, tq=256)
        else:
            if not last:
                raise NotImplementedError("context update after an odd layer is not needed for depth 2")
            tp = _bias_table(rpb_d[i])
            m_x = _attn_odd(qx, kv, sink_c[i], tp, n_ctx=n_ctx, tq=1024, sink_order=GQA_SLOT_HEADS)
        x = _post(m_x, x, mod_x, norm_g[l], wo, w1, w2, tm=512, ff_chunk=1024)
        if not last:
            ctx = _post(m_c.reshape(1, b * n_ctx, -1), ctx, mod_c, norm_g[l], wo, w1, w2, tm=512, ff_chunk=1024)
    return x
```

```python
import functools
import math
from typing import Any, Callable, NamedTuple, Sequence

import numpy as np
import jax
import jax.numpy as jnp
from jax import lax
from jax.experimental import pallas as pl
from jax.experimental.pallas import tpu as pltpu

F32 = jnp.float32
BF16 = jnp.bfloat16

GRID_W = 64
HEAD_DIM = 64
ROPE_THETA = 10000.0
ROPE_FREQS = HEAD_DIM // 4
N_SLOTS = 8
PAIR = 2 * HEAD_DIM
Q_WIDTH = N_SLOTS * HEAD_DIM
LANES = 128
MXU_QUERY_COLS = 512
EVEN_A_QUERY_COLS = MXU_QUERY_COLS
EVEN_KEY_CHUNK = 1024
EVEN_EXTRA_LEAD = 2
ODD_EXTRA_LEAD = 3
IN_SUB_ROWS = MXU_QUERY_COLS
POST_SUB_ROWS = 256
C_SLOTS_PER_JOB = 8
NA_CHUNK_ROWS = 4
NA_INVALID_COLUMN = 1 << 20
C_WINDOW = 128
NA_ROWS = 8
NA_COLS = 16
EPS = 1e-6
NEG = -1e30
LOG2E = math.log2(math.e)
Q_SCALE = HEAD_DIM ** -0.5 * LOG2E
GQA_SLOT_HEADS = tuple(half * (N_SLOTS // 2) + p for p in range(N_SLOTS // 2) for half in range(2))

VMEM_LIMIT_BYTES = 56 * 1024 * 1024


def _compiler_params(n_grid_dims):
    return pltpu.CompilerParams(dimension_semantics=("arbitrary",) * n_grid_dims,
                                vmem_limit_bytes=VMEM_LIMIT_BYTES)


def _dot(a, b):
    return jnp.dot(a, b, preferred_element_type=F32)


def _dot_nt(a, b):
    return lax.dot_general(a, b, (((1,), (1,)), ((), ())), preferred_element_type=F32)


def _log2(n):
    assert n & (n - 1) == 0
    return n.bit_length() - 1


def _rms(x, gain):
    ms = jnp.mean(x * x, axis=-1, keepdims=True)
    return x * lax.rsqrt(ms + EPS) * gain


def _mod_kernel(c_ref, w_ref, b_ref, o_ref):
    c = c_ref[...]
    s = c * jax.nn.sigmoid(c)
    o_ref[0] = _dot(s.astype(BF16), w_ref[0].astype(BF16)) + b_ref[0]


def _modulation(cc, w_mod, b_mod):
    depth, d, n = w_mod.shape
    rows = cc.shape[0]
    tn = n // 2
    return pl.pallas_call(
        _mod_kernel,
        grid=(depth, n // tn),
        in_specs=[pl.BlockSpec((rows, d), lambda l, j: (0, 0)),
                  pl.BlockSpec((1, d, tn), lambda l, j: (l, 0, j)),
                  pl.BlockSpec((1, 1, tn), lambda l, j: (l, 0, j))],
        out_specs=pl.BlockSpec((1, rows, tn), lambda l, j: (l, 0, j)),
        out_shape=jax.ShapeDtypeStruct((depth, rows, n), F32),
        compiler_params=_compiler_params(2),
        name="modulation",
    )(cc, w_mod, b_mod.reshape(depth, 1, n))


def _in_kernel(*refs, even, n_batch, sub_rows):
    refs = list(refs)
    tok_ref, ctx_ref, mod_ref, g_ref, wt_ref, wk_ref = refs[:6]
    del refs[:6]
    if even:
        gq_ref, gk_ref, seg_ref = refs[:3]
        del refs[:3]
    rope_t_ref, rope_ref = refs[:2]
    q1_ref, k1_ref, v1_ref, q2_ref, k2_ref, v2_ref = refs[2:]
    d = tok_ref.shape[-1]
    tm = tok_ref.shape[1]
    sub = min(sub_rows, tm)
    is_ctx = pl.program_id(0) == n_batch

    def normed(rows):
        h = _rms(jnp.where(is_ctx, ctx_ref[0, rows], tok_ref[0, rows]), g_ref[...])
        h = h * (1.0 + mod_ref[0, :, d:2 * d]) + mod_ref[0, :, 0:d]
        return h.astype(BF16)

    def project(rows, hb):
        def rotary_t(c):
            f = ROPE_FREQS
            out = []
            for a in range(2):
                x1, x2 = c[2 * a * f:(2 * a + 1) * f], c[(2 * a + 1) * f:(2 * a + 2) * f]
                cos, sin = rope_t_ref[0, 0, a * f:(a + 1) * f, rows], rope_t_ref[0, 1, a * f:(a + 1) * f, rows]
                out += [x1 * cos - x2 * sin, x2 * cos + x1 * sin]
            return jnp.concatenate(out, axis=0)

        def store_slots(q_ref, qt, use_norm, use_rope):
            zeros = jnp.zeros((HEAD_DIM, sub), F32)
            for j in range(N_SLOTS):
                c = qt[j * HEAD_DIM:(j + 1) * HEAD_DIM]
                if use_norm:
                    ms = jnp.mean(c * c, axis=0, keepdims=True)
                    c = c * lax.rsqrt(ms + EPS) * gq_ref[:, 0:sub]
                if use_rope:
                    c = rotary_t(c)
                c = c * Q_SCALE
                slot = jnp.concatenate([c, zeros] if j % 2 == 0 else [zeros, c], axis=0)
                q_ref[0, j, :, rows] = slot.astype(BF16)

        def store_vt(v_ref, vt):
            if even:
                v_ref[0, :, rows] = vt.astype(BF16)
            else:
                for c in range(sub // LANES):
                    v_ref[0, rows.start // LANES + c] = vt[:, c * LANES:(c + 1) * LANES].astype(BF16)

        def rotary(c):
            up = pltpu.roll(c, PAIR - ROPE_FREQS, axis=1)
            down = pltpu.roll(c, ROPE_FREQS, axis=1)
            return c * rope_ref[0, 0, rows] + up * rope_ref[0, 1, rows] + down * rope_ref[0, 2, rows]

        o_v1, o_q2, o_v2 = Q_WIDTH, Q_WIDTH + PAIR, 2 * Q_WIDTH + PAIR
        project_keys(rows, hb, rotary)
        store_slots(q1_ref, _dot_nt(wt_ref[0:o_v1], hb), even, True)
        store_slots(q2_ref, _dot_nt(wt_ref[o_q2:o_v2], hb), False, even)
        store_vt(v1_ref, _dot_nt(wt_ref[o_v1:o_q2], hb))
        store_vt(v2_ref, _dot_nt(wt_ref[o_v2:o_v2 + Q_WIDTH], hb))

    def project_keys(rows, hb, rotary):
        k = _dot(hb, wk_ref[...])
        k1 = k[:, 0:PAIR]
        if even:
            sq = k1 * k1
            hi = sq.astype(BF16)
            lo = (sq - hi.astype(F32)).astype(BF16)
            ms = (_dot(hi, seg_ref[...]) + _dot(lo, seg_ref[...])) * (1.0 / HEAD_DIM)
            k1 = k1 * lax.rsqrt(ms + EPS) * gk_ref[...]
        k1 = rotary(k1)
        k1_ref[0, rows] = k1.astype(BF16)
        for j in range(Q_WIDTH // PAIR):
            c = k[:, PAIR + j * PAIR:PAIR + (j + 1) * PAIR]
            if even:
                c = rotary(c)
            k2_ref[0, rows, j * PAIR:(j + 1) * PAIR] = c.astype(BF16)

    blocks = [slice(r, r + sub) for r in range(0, tm, sub)]
    hbs = [normed(rows) for rows in blocks]
    for rows, hb in zip(blocks, hbs):
        project(rows, hb)


def _in_proj(tok, ctx, mod, gain, wt, wk, norm_args, rope_args, *, even, tm):
    n, s, d = tok.shape
    assert ctx.shape == (1, s, d) and mod.shape[0] == n + 1
    b = n + 1

    def const(a):
        return pl.BlockSpec(a.shape, lambda i, j: (0,) * a.ndim)

    def is_ctx(i):
        return (i == n).astype(jnp.int32)

    in_specs = [pl.BlockSpec((1, tm, d), lambda i, j: (jnp.minimum(i, n - 1), j, 0)),
                pl.BlockSpec((1, tm, d), lambda i, j: (0, is_ctx(i) * j, 0)),
                pl.BlockSpec((1, 1, mod.shape[-1]), lambda i, j: (i, 0, 0)),
                const(gain), const(wt), const(wk)]
    args = [tok, ctx, mod, gain, wt, wk]
    if even:
        gq, gk, seg = norm_args
        in_specs += [const(gq), const(gk), const(seg)]
        args += [gq, gk, seg]
    rope_t, rope_n = rope_args
    in_specs += [pl.BlockSpec((1, 2, 2 * ROPE_FREQS, tm), lambda i, j: (is_ctx(i), 0, 0, j)),
                 pl.BlockSpec((1, 3, tm, PAIR), lambda i, j: (is_ctx(i), 0, j, 0))]
    args += [rope_t, rope_n]

    def vt_out(rows):
        if even:
            return (pl.BlockSpec((1, rows, tm), lambda i, j: (i, 0, j)),
                    jax.ShapeDtypeStruct((b, rows, s), BF16))
        return (pl.BlockSpec((1, tm // LANES, rows, LANES), lambda i, j: (i, j, 0, 0)),
                jax.ShapeDtypeStruct((b, s // LANES, rows, LANES), BF16))

    slot_out = (pl.BlockSpec((1, N_SLOTS, PAIR, tm), lambda i, j: (i, 0, 0, j)),
                jax.ShapeDtypeStruct((b, N_SLOTS, PAIR, s), BF16))

    def nat_out(cols):
        return (pl.BlockSpec((1, tm, cols), lambda i, j: (i, j, 0)), jax.ShapeDtypeStruct((b, s, cols), BF16))

    outs = [slot_out, nat_out(PAIR), vt_out(PAIR), slot_out, nat_out(Q_WIDTH), vt_out(Q_WIDTH)]
    return pl.pallas_call(
        functools.partial(_in_kernel, even=even, n_batch=n, sub_rows=IN_SUB_ROWS),
        grid=(b, s // tm),
        in_specs=in_specs,
        out_specs=[o[0] for o in outs],
        out_shape=[o[1] for o in outs],
        compiler_params=_compiler_params(2),
        name="in_proj_" + ("even" if even else "odd"),
    )(*args)


class AttnJob(NamedTuple):
    query: Callable[[], Any]
    scores: Sequence[Callable[[Any], Any]]
    values: Sequence[Callable[[Any], Any]]
    finish: Callable[[Any], None]
    sink: Any = None


def _attend_jobs(jobs, s_ref):
    items = [(j, c) for j, job in enumerate(jobs) for c in range(len(job.scores))]
    n_slots = s_ref.shape[0]
    lead = n_slots - 1
    assert lead > max(len(job.scores) for job in jobs)
    qt, shape, m, l, o = {}, {}, {}, {}, {}
    for t in range(len(items) + lead):
        if t < len(items):
            j, c = items[t]
            job = jobs[j]
            if c == 0:
                qt[j] = job.query()
            x = job.scores[c](qt[j])
            shape[t] = x.shape
            s_ref[t % n_slots, 0:x.shape[0], 0:x.shape[1]] = x
            part_m = jnp.max(x, axis=0, keepdims=True)
            if c == 0:
                m[j] = part_m if job.sink is None else jnp.maximum(part_m, job.sink)
            else:
                m[j] = jnp.maximum(m[j], part_m)
        if t >= lead:
            j, c = items[t - lead]
            job = jobs[j]
            rows, cols = shape.pop(t - lead)
            e = jnp.exp2(s_ref[(t - lead) % n_slots, 0:rows, 0:cols] - m[j])
            part_l = jnp.sum(e, axis=0, keepdims=True)
            part_o = job.values[c](e.astype(BF16))
            l[j] = part_l if c == 0 else l[j] + part_l
            o[j] = part_o if c == 0 else o[j] + part_o
            if c == len(job.scores) - 1:
                denom = l.pop(j) if job.sink is None else l.pop(j) + jnp.exp2(job.sink - m[j])
                job.finish(o.pop(j) * (1.0 / denom))


def _attn_even_kernel(*refs, with_x, lam0, tq, key_chunk):
    if with_x:
        q1_ref, q2_ref, k1c_ref, v1c_ref, k2c_ref, v2c_ref, k1x_ref, v1x_ref, k2x_ref, v2x_ref = refs[:10]
    else:
        q1_ref, q2_ref, k1c_ref, v1c_ref, k2c_ref, v2c_ref = refs[:6]
    lam_ref, sub_ref, o_ref, s_ref = refs[-4:]

    def chunk_fns(c_ref, x_ref, cols, transposed):
        def make(ref, lo, hi):
            if transposed:
                return lambda e: _dot(ref[0, cols, lo:hi], e)
            return lambda qt: _dot(ref[0, lo:hi, cols], qt)
        n_ctx = c_ref.shape[2] if transposed else c_ref.shape[1]
        fns = [make(c_ref, 0, n_ctx)]
        if with_x:
            n_x = x_ref.shape[2] if transposed else x_ref.shape[1]
            fns += [make(x_ref, lo, lo + key_chunk) for lo in range(0, n_x, key_chunk)]
        return fns

    group = max(2, EVEN_A_QUERY_COLS // tq)
    jobs = []
    pair_cols = slice(0, PAIR)

    def a_query(g):
        return jnp.concatenate([q1_ref[0, g * group + j] for j in range(group)], axis=1)

    def a_finish(g, o):
        for pp in range(group // 2):
            pair_t = jnp.concatenate([o[0:HEAD_DIM, (2 * pp) * tq:(2 * pp + 1) * tq],
                                      o[HEAD_DIM:PAIR, (2 * pp + 1) * tq:(2 * pp + 2) * tq]], axis=0)
            col = (g * (group // 2) + pp) * PAIR
            o_ref[0, :, col:col + PAIR] = pair_t.T.astype(BF16)

    for g in range(N_SLOTS // group):
        jobs.append(AttnJob(functools.partial(a_query, g),
                            chunk_fns(k1c_ref, k1x_ref if with_x else None, pair_cols, False),
                            chunk_fns(v1c_ref, v1x_ref if with_x else None, pair_cols, True),
                            functools.partial(a_finish, g)))

    def b_query(h):
        return jnp.concatenate([q2_ref[0, 2 * h], q2_ref[0, 2 * h + 1]], axis=1)

    def b_finish(h, o):
        lp = lam_ref[...]
        lam = (jnp.exp(jnp.sum(lp[0:1] * lp[1:2], axis=-1, keepdims=True))
               - jnp.exp(jnp.sum(lp[2:3] * lp[3:4], axis=-1, keepdims=True)) + lam0)
        ob = (o[:, 0:tq] - lam * o[:, tq:2 * tq]).T
        ob = _rms(ob, sub_ref[...]) * (1.0 - lam0)
        col = Q_WIDTH + h * PAIR
        o_ref[0, :, col:col + PAIR] = ob.astype(BF16)

    for h in range(N_SLOTS // 2):
        cols = slice(h * PAIR, (h + 1) * PAIR)
        jobs.append(AttnJob(functools.partial(b_query, h),
                            chunk_fns(k2c_ref, k2x_ref if with_x else None, cols, False),
                            chunk_fns(v2c_ref, v2x_ref if with_x else None, cols, True),
                            functools.partial(b_finish, h)))
    _attend_jobs(jobs, s_ref)


def _ctx_specs(kv, n_ctx, blocked):
    row = kv[0].shape[0] - 1

    def natural(a):
        return pl.BlockSpec((1, n_ctx, a.shape[2]), lambda i, j: (row, i, 0))

    def transposed(a):
        if blocked:
            return pl.BlockSpec((1, n_ctx // LANES) + a.shape[2:], lambda i, j: (row, i, 0, 0))
        return pl.BlockSpec((1, a.shape[1], n_ctx), lambda i, j: (row, 0, i))

    k1, v1, k2, v2 = kv
    return [natural(k1), transposed(v1), natural(k2), transposed(v2)]


def _attn_even(qx, kv, lam_p, subln, *, with_x, n_ctx, lam0, tq):
    q1, q2 = qx
    row = q1.shape[0] - 1
    if with_x:
        b, s = row, q1.shape[3]
        slot_spec = pl.BlockSpec((1, N_SLOTS, PAIR, tq), lambda i, j: (i, 0, 0, j))
    else:
        b, s = q1.shape[3] // n_ctx, n_ctx
        slot_spec = pl.BlockSpec((1, N_SLOTS, PAIR, tq), lambda i, j: (row, 0, 0, i * (n_ctx // tq) + j))

    def full(a):
        return pl.BlockSpec((1,) + a.shape[1:], lambda i, j: (i, 0, 0))

    n_x = kv[0].shape[1]
    key_chunk = min(EVEN_KEY_CHUNK, n_x) if with_x else n_ctx
    chunks_per_job = 1 + (n_x // key_chunk if with_x else 0)
    score_ring = pltpu.VMEM((chunks_per_job + EVEN_EXTRA_LEAD + 1, max(n_ctx, key_chunk),
                             max(2 * tq, EVEN_A_QUERY_COLS)), F32)
    return pl.pallas_call(
        functools.partial(_attn_even_kernel, with_x=with_x, lam0=lam0, tq=tq, key_chunk=key_chunk),
        grid=(b, s // tq),
        scratch_shapes=[score_ring],
        in_specs=[slot_spec, slot_spec] + _ctx_specs(kv, n_ctx, blocked=False)
                 + ([full(a) for a in kv] if with_x else [])
                 + [pl.BlockSpec(lam_p.shape, lambda i, j: (0, 0)), pl.BlockSpec(subln.shape, lambda i, j: (0, 0))],
        out_specs=pl.BlockSpec((1, tq, 2 * Q_WIDTH), lambda i, j: (i, j, 0)),
        out_shape=jax.ShapeDtypeStruct((b, s, 2 * Q_WIDTH), BF16),
        compiler_params=_compiler_params(2),
        name="attn_even" + ("_x" if with_x else "_ctx"),
    )(q1, q2, *kv, *(kv if with_x else ()), lam_p, subln)


N_BIAS_ROWS = 2 * NA_ROWS


def _bias_kernel(rpb_ref, o_ref):
    head = pl.program_id(0)
    n_dr = 2 * NA_ROWS - 1
    n_dc = 2 * NA_COLS - 1
    shape = (GRID_W, LANES)
    ck = lax.broadcasted_iota(jnp.int32, shape, 0)
    lane = lax.broadcasted_iota(jnp.int32, shape, 1)
    second = lane >= GRID_W
    dcl = jnp.clip(ck - (lane & (GRID_W - 1)), -(NA_COLS - 1), NA_COLS - 1) + (NA_COLS - 1)
    tiles = []
    for d in range(n_dr):
        acc = jnp.zeros(shape, F32)
        for j in range(n_dc):
            acc = jnp.where(dcl == j, rpb_ref[(head * n_dr + d) * n_dc + j], acc)
        tiles.append(acc * LOG2E)
    for t in range(N_BIAS_ROWS):
        o_ref[0, t] = jnp.where(second, tiles[max(t - 1, 0)], tiles[min(t, n_dr - 1)])


def _bias_table(rpb):
    heads = rpb.shape[0]
    return pl.pallas_call(
        _bias_kernel,
        grid=(heads,),
        in_specs=[pl.BlockSpec(memory_space=pltpu.SMEM)],
        out_specs=pl.BlockSpec((1, N_BIAS_ROWS, GRID_W, LANES), lambda h: (h, 0, 0, 0)),
        out_shape=jax.ShapeDtypeStruct((heads, N_BIAS_ROWS, GRID_W, LANES), F32),
        compiler_params=_compiler_params(1),
        name="rel_bias_table",
    )(rpb.reshape(-1))


def _attn_odd_kernel(q1_ref, q2_ref, k1x_ref, v1x_ref, k2x_ref, v2x_ref, k1c_ref, v1c_ref, k2c_ref, v2c_ref,
                     sink_ref, tp_ref, win_ref, o_ref, s_ref, *, tq, sink_order):
    i = pl.program_id(1)
    t = k1x_ref.shape[1]
    n_ctx_blk = v1c_ref.shape[1]
    rows = t // GRID_W
    kh = min(NA_ROWS, rows)
    n_blk = tq // LANES

    jobs = []

    n_q = N_SLOTS * C_WINDOW
    n_seq_blk = t // C_WINDOW
    lane_q = lax.broadcasted_iota(jnp.int32, (1, n_q), 1)
    sink = jnp.zeros((1, n_q), F32)
    for j in range(N_SLOTS):
        sink = jnp.where(lane_q >> _log2(C_WINDOW) == j, sink_ref[sink_order[j]], sink)
    sink = sink * LOG2E
    n_qj = C_SLOTS_PER_JOB * C_WINDOW
    tri = (lax.broadcasted_iota(jnp.int32, (C_WINDOW, n_qj), 0)
           - (lax.broadcasted_iota(jnp.int32, (C_WINDOW, n_qj), 1) & (C_WINDOW - 1)))

    def c_blocks(jb):
        blk = i * n_blk + jb
        return blk, jnp.maximum(blk - 1, 0), jnp.minimum(blk + 1, n_seq_blk - 1)

    def c_query(jb, s0):
        return jnp.concatenate([q1_ref[0, j, :, jb * LANES:(jb + 1) * LANES]
                                for j in range(s0, s0 + C_SLOTS_PER_JOB)], axis=1)

    def c_local_scores(jb, qt):
        blk, prev, nxt = c_blocks(jb)
        k_loc = jnp.concatenate([k1x_ref[0, pl.ds(pl.multiple_of(n * C_WINDOW, C_WINDOW), C_WINDOW), :]
                                 for n in (prev, blk, nxt)], axis=0)
        s = _dot(k_loc, qt)
        lo_bound = jnp.where(blk > 0, 0, C_WINDOW)
        hi_bound = jnp.where(blk < n_seq_blk - 1, 0, -C_WINDOW)
        return jnp.concatenate([jnp.where(tri >= lo_bound, s[0:C_WINDOW], NEG),
                                s[C_WINDOW:2 * C_WINDOW],
                                jnp.where(tri <= hi_bound, s[2 * C_WINDOW:3 * C_WINDOW], NEG)], axis=0)

    def c_local_values(jb, e):
        blk, prev, nxt = c_blocks(jb)
        return _dot(jnp.concatenate([v1x_ref[0, n] for n in (prev, blk, nxt)], axis=1), e)

    def c_finish(jb, s0, o):
        for p in range(C_SLOTS_PER_JOB // 2):
            pair_t = jnp.concatenate([o[0:HEAD_DIM, (2 * p) * LANES:(2 * p + 1) * LANES],
                                      o[HEAD_DIM:PAIR, (2 * p + 1) * LANES:(2 * p + 2) * LANES]], axis=0)
            col = (s0 // 2 + p) * PAIR
            o_ref[0, jb * LANES:(jb + 1) * LANES, col:col + PAIR] = pair_t.T.astype(BF16)

    for jb in range(n_blk):
        for s0 in range(0, N_SLOTS, C_SLOTS_PER_JOB):
            jobs.append(AttnJob(
                functools.partial(c_query, jb, s0),
                [functools.partial(c_local_scores, jb), lambda qt: _dot(k1c_ref[0], qt)],
                [functools.partial(c_local_values, jb),
                 lambda e: _dot(jnp.concatenate([v1c_ref[0, c] for c in range(n_ctx_blk)], axis=1), e)],
                functools.partial(c_finish, jb, s0), sink[:, s0 * C_WINDOW:(s0 + C_SLOTS_PER_JOB) * C_WINDOW]))

    blk_rows = LANES // GRID_W
    span_rows = -(-(blk_rows - 1 + kh) // 2) * 2
    n_loc = span_rows * GRID_W
    q_row = (lax.broadcasted_iota(jnp.int32, (1, 4 * LANES), 1) >> _log2(GRID_W)) & (blk_rows - 1)

    def d_rows(bk):
        r0 = (i * n_blk + bk) * blk_rows
        return r0, jnp.minimum(jnp.clip(r0 - kh // 2, 0, rows - kh), rows - span_rows)

    def d_query(bk, p0):
        tok = slice(bk * LANES, (bk + 1) * LANES)
        blocks = [jnp.concatenate([q2_ref[0, 2 * p, :, tok], q2_ref[0, 2 * p + 1, :, tok]], axis=1)
                  for p in (p0, p0 + 1)]
        zeros = jnp.zeros_like(blocks[0])
        return jnp.concatenate([jnp.concatenate([blocks[0], zeros], axis=1),
                                jnp.concatenate([zeros, blocks[1]], axis=1)], axis=0)

    def d_local_scores(bk, p0, qt):
        r0, u = d_rows(bk)
        k_loc = k2x_ref[0, pl.ds(pl.multiple_of(u * GRID_W, LANES), n_loc), p0 * PAIR:(p0 + 2) * PAIR]
        bias = jnp.concatenate(
            [jnp.concatenate([tp_ref[2 * p0 + lb, jnp.clip(u + a - r0 + NA_ROWS - 1, 0, N_BIAS_ROWS - 1)]
                              for lb in range(4)], axis=1)
             for a in range(span_rows)], axis=0)
        first = jnp.zeros((1, 4 * LANES), jnp.int32)
        for j in range(blk_rows):
            first = jnp.where(q_row == j, jnp.clip(r0 + j - kh // 2, 0, rows - kh) - u, first)
        valid = lax.bitcast_convert_type(win_ref[...] - first, jnp.uint32) < jnp.uint32(kh)
        return jnp.where(valid, _dot(k_loc, qt) + bias, NEG)

    def d_values(bk, p0, local, e):
        _, u = d_rows(bk)
        out = []
        for n, p in enumerate((p0, p0 + 1)):
            cols = slice(p * PAIR, (p + 1) * PAIR)
            if local:
                vt = jnp.concatenate([v2x_ref[0, u // blk_rows + b, cols, :] for b in range(n_loc // LANES)],
                                     axis=1)
            else:
                vt = jnp.concatenate([v2c_ref[0, b, cols, :] for b in range(n_ctx_blk)], axis=1)
            out.append(_dot(vt, e[:, n * 2 * LANES:(n + 1) * 2 * LANES]))
        return jnp.concatenate(out, axis=1)

    def d_finish(bk, p0, o):
        for n, p in enumerate((p0, p0 + 1)):
            lo = o[0:HEAD_DIM, (2 * n) * LANES:(2 * n + 1) * LANES]
            hi = o[HEAD_DIM:PAIR, (2 * n + 1) * LANES:(2 * n + 2) * LANES]
            o_ref[0, bk * LANES:(bk + 1) * LANES, Q_WIDTH + p * PAIR:Q_WIDTH + (p + 1) * PAIR] = (
                jnp.concatenate([lo, hi], axis=0).T.astype(BF16))

    for bk in range(n_blk):
        for p0 in range(0, N_SLOTS // 2, 2):
            jobs.append(AttnJob(
                functools.partial(d_query, bk, p0),
                [functools.partial(d_local_scores, bk, p0),
                 lambda qt, p0=p0: _dot(k2c_ref[0, :, p0 * PAIR:(p0 + 2) * PAIR], qt)],
                [functools.partial(d_values, bk, p0, True), functools.partial(d_values, bk, p0, False)],
                functools.partial(d_finish, bk, p0)))
    _attend_jobs(jobs, s_ref)


def _attn_odd(qx, kv, sink, tp, *, n_ctx, tq, sink_order):
    q1, q2 = qx
    b, s = q1.shape[0] - 1, q1.shape[3]
    slot_spec = pl.BlockSpec((1, N_SLOTS, PAIR, tq), lambda i, j: (i, 0, 0, j))

    def full(a):
        return pl.BlockSpec((1,) + a.shape[1:], lambda i, j: (i,) + (0,) * (a.ndim - 1))

    rows = s // GRID_W
    span_rows = -(-(LANES // GRID_W - 1 + min(NA_ROWS, rows)) // 2) * 2
    chunks_per_job = 2
    key_i = np.arange(span_rows * GRID_W)[:, None]
    cq = np.arange(4 * LANES)[None, :] % GRID_W
    cs = np.clip(cq - NA_COLS // 2, 0, GRID_W - NA_COLS)
    ck = key_i % GRID_W
    col_ok = (ck >= cs) & (ck < cs + NA_COLS)
    window = jnp.asarray(key_i // GRID_W + np.where(col_ok, 0, NA_INVALID_COLUMN), jnp.int32)
    score_ring = pltpu.VMEM((chunks_per_job + ODD_EXTRA_LEAD + 1, max(3 * C_WINDOW, span_rows * GRID_W, n_ctx),
                             max(C_SLOTS_PER_JOB * C_WINDOW, 4 * LANES)), F32)
    return pl.pallas_call(
        functools.partial(_attn_odd_kernel, tq=tq, sink_order=sink_order),
        grid=(b, s // tq),
        scratch_shapes=[score_ring],
        in_specs=[slot_spec, slot_spec] + [full(a) for a in kv] + _ctx_specs(kv, n_ctx, blocked=True)
                 + [pl.BlockSpec(memory_space=pltpu.SMEM), pl.BlockSpec(tp.shape, lambda i, j: (0, 0, 0, 0)),
                    pl.BlockSpec(window.shape, lambda i, j: (0, 0))],
        out_specs=pl.BlockSpec((1, tq, 2 * Q_WIDTH), lambda i, j: (i, j, 0)),
        out_shape=jax.ShapeDtypeStruct((b, s, 2 * Q_WIDTH), BF16),
        compiler_params=_compiler_params(2),
        name="attn_odd",
    )(q1, q2, *kv, *kv, sink, tp, window)


def _post_kernel(m_ref, x_ref, mod_ref, g_ref, wo_ref, w1_ref, w2_ref, o_ref, *, ff_chunk, sub_rows):
    d = x_ref.shape[-1]

    def mod(k):
        return mod_ref[0, :, k * d:(k + 1) * d]

    def mixer_residual(rows):
        x1 = x_ref[0, rows] + mod(2) * _rms(_dot(m_ref[0, rows], wo_ref[...]), g_ref[1:2])
        return x1, (_rms(x1, g_ref[2:3]) * (1.0 + mod(4)) + mod(3)).astype(BF16)

    def mlp(h):
        f = None
        for c in range(w1_ref.shape[1] // ff_chunk):
            cols = slice(c * ff_chunk, (c + 1) * ff_chunk)
            a = jnp.maximum(_dot(h, w1_ref[:, cols]), 0.0)
            part = _dot((a * a).astype(BF16), w2_ref[cols, :])
            f = part if f is None else f + part
        return f

    tm = x_ref.shape[1]
    sub = min(sub_rows, tm)
    blocks = [slice(r, r + sub) for r in range(0, tm, sub)]
    stage1 = [mixer_residual(rows) for rows in blocks]
    stage2 = [mlp(h) for _, h in stage1]
    for rows, (x1, _), f in zip(blocks, stage1, stage2):
        o_ref[0, rows] = x1 + mod(5) * _rms(f, g_ref[3:4])


def _post(m, x, mod, gains, wo, w1, w2, *, tm, ff_chunk):
    b, s, d = x.shape
    bm = mod.shape[0]
    mod_map = (lambda i, j: (i, 0, 0)) if bm > 1 else (lambda i, j: (0, 0, 0))

    def const(a):
        return pl.BlockSpec(a.shape, lambda i, j: (0, 0), pipeline_mode=pl.Buffered(1))

    return pl.pallas_call(
        functools.partial(_post_kernel, ff_chunk=ff_chunk, sub_rows=POST_SUB_ROWS),
        grid=(b, s // tm),
        in_specs=[pl.BlockSpec((1, tm, m.shape[-1]), lambda i, j: (i, j, 0)),
                  pl.BlockSpec((1, tm, d), lambda i, j: (i, j, 0)),
                  pl.BlockSpec((1, 1, mod.shape[-1]), mod_map),
                  pl.BlockSpec(gains.shape, lambda i, j: (0, 0)),
                  const(wo), const(w1), const(w2)],
        out_specs=pl.BlockSpec((1, tm, d), lambda i, j: (i, j, 0)),
        out_shape=jax.ShapeDtypeStruct((b, s, d), F32),
        compiler_params=_compiler_params(2),
        name="post",
    )(m, x, mod, gains, wo, w1, w2)


def _rope_tables(t):
    tok = jnp.arange(t, dtype=jnp.int32)
    pos = jnp.stack([tok // GRID_W, tok % GRID_W], axis=-1).astype(F32)
    inv = ROPE_THETA ** (-jnp.arange(ROPE_FREQS, dtype=F32) / ROPE_FREQS)
    ang = pos[..., None] * inv
    cos, sin = jnp.cos(ang), jnp.sin(ang)
    rope_t = jnp.stack([cos.reshape(t, -1).T, sin.reshape(t, -1).T])
    zero = jnp.zeros_like(sin[:, 0])
    c = jnp.concatenate([cos[:, 0], cos[:, 0], cos[:, 1], cos[:, 1]], axis=-1)
    up = jnp.concatenate([-sin[:, 0], zero, -sin[:, 1], zero], axis=-1)
    down = jnp.concatenate([zero, sin[:, 0], zero, sin[:, 1]], axis=-1)
    rope_n = jnp.stack([jnp.tile(a, (1, PAIR // HEAD_DIM)) for a in (c, up, down)])
    one_t, one_n = jnp.ones_like(rope_t[0]), jnp.ones_like(rope_n[0])
    identity_t = jnp.stack([one_t, jnp.zeros_like(one_t)])
    identity_n = jnp.stack([one_n, jnp.zeros_like(one_n), jnp.zeros_like(one_n)])
    return jnp.stack([rope_t, identity_t]), jnp.stack([rope_n, identity_n])


def _pair_gqa_heads(w, axis):
    shape = w.shape
    w = w.reshape(shape[:axis] + (2, N_SLOTS // 2, HEAD_DIM) + shape[axis + 1:])
    return jnp.swapaxes(w, axis, axis + 1).reshape(shape)


def _lambda_init(layer):
    return 0.8 - 0.6 * math.exp(-0.3 * layer)


def kernel(x, c, ctx, c_ctx, w_mod, b_mod, norm_g, w_in, w_out, w_mlp_in, w_mlp_out, qk_norm_a, diff_lambda,
           diff_subln, sink_c, rpb_d):
    b, t, d = x.shape
    n_ctx = ctx.shape[1]
    depth = w_mod.shape[0]
    mod_rows = 16
    cc = jnp.concatenate([c, c_ctx[None], jnp.zeros((mod_rows - b - 1, d), F32)], axis=0)
    mod_all = _modulation(cc, w_mod, b_mod)

    rope_args = _rope_tables(t)
    seg = jnp.kron(jnp.eye(2, dtype=F32), jnp.ones((HEAD_DIM, HEAD_DIM), F32)).astype(BF16)
    tm_x = 2 * IN_SUB_ROWS
    assert b * n_ctx == t
    ctx = ctx.reshape(1, b * n_ctx, d)

    for l in range(depth):
        last = l == depth - 1
        even = l % 2 == 0
        i = l // 2
        mod_x = mod_all[l, :b][:, None, :]
        mod_c = mod_all[l, b][None, None, :]
        w = w_in[l]
        o_k1, o_v1, o_q2, o_k2, o_v2 = Q_WIDTH, Q_WIDTH + PAIR, Q_WIDTH + 2 * PAIR, 2 * Q_WIDTH + 2 * PAIR, \
            3 * Q_WIDTH + 2 * PAIR
        wt = jnp.concatenate([_pair_gqa_heads(w[:, 0:o_k1], 1), w[:, o_v1:o_q2], w[:, o_q2:o_k2], w[:, o_v2:]],
                             axis=1).T.astype(BF16)
        wk = jnp.concatenate([w[:, o_k1:o_v1], w[:, o_k2:o_v2]], axis=1).astype(BF16)
        wo = jnp.concatenate([_pair_gqa_heads(w_out[l, 0:Q_WIDTH], 0), w_out[l, Q_WIDTH:]], axis=0).astype(BF16)
        w1 = w_mlp_in[l].astype(BF16)
        w2 = w_mlp_out[l].astype(BF16)
        g0 = norm_g[l, 0][None, :]

        def norm_args(tm):
            if not even:
                return None
            return (jnp.broadcast_to(qk_norm_a[i, 0][:, None], (HEAD_DIM, tm)),
                    jnp.tile(qk_norm_a[i, 1], PAIR // HEAD_DIM)[None, :], seg)

        mod_xc = mod_all[l, :b + 1][:, None, :]
        q1, k1, v1, q2, k2, v2 = _in_proj(x, ctx, mod_xc, g0, wt, wk, norm_args(tm_x), rope_args,
                                          even=even, tm=tm_x)
        qx, kv = (q1, q2), (k1, v1, k2, v2)
        if even:
            lam0 = _lambda_init(l)
            sub = diff_subln[i][None, :]
            m_x = _attn_even(qx, kv, diff_lambda[i], sub, with_x=True, n_ctx=n_ctx, lam0=lam0, tq=256)
            if not last:
                m_c = _attn_even(qx, kv, diff_lambda[i], sub, with_x=False, n_ctx=n_ctx, lam0=lam0, tq=256)
        else:
            if not last:
                raise NotImplementedError("context update after an odd layer is not needed for depth 2")
            tp = _bias_table(rpb_d[i])
            m_x = _attn_odd(qx, kv, sink_c[i], tp, n_ctx=n_ctx, tq=1024, sink_order=GQA_SLOT_HEADS)
        x = _post(m_x, x, mod_x, norm_g[l], wo, w1, w2, tm=512, ff_chunk=1024)
        if not last:
            ctx = _post(m_c.reshape(1, b * n_ctx, -1), ctx, mod_c, norm_g[l], wo, w1, w2, tm=512, ff_chunk=1024)
    return x
```
